```python
import math
import jax
import jax.numpy as jnp
from jax import lax
import numpy as np

D_MODEL = 2048
BATCH = 8
SEQ = 4096
DEPTH = 4

N_A_LAYERS = DEPTH // 2
N_B_LAYERS = DEPTH - N_A_LAYERS

GDN_QK_HEADS = 16
GDN_V_HEADS = 32
GDN_HEAD_DIM = 128
GDN_CONV = 4
GDN_CHUNK = 64
GDN_QK_DIM = GDN_QK_HEADS * GDN_HEAD_DIM
GDN_V_DIM = GDN_V_HEADS * GDN_HEAD_DIM
GDN_CONV_DIM = 2 * GDN_QK_DIM + GDN_V_DIM
GDN_IN_DIM = GDN_CONV_DIM + GDN_V_DIM + 2 * GDN_V_HEADS

DIL_CONFIGS = ((128, 1), (512, 4), (2048, 16))
N_DIL = len(DIL_CONFIGS)
DIL_HEADS = 8
DIL_HEAD_DIM = 128
DIL_BLOCK = 128
DIL_KV_DIM = DIL_HEADS * DIL_HEAD_DIM
DIL_Q_DIM = N_DIL * DIL_KV_DIM

REL_BUCKETS = 32
REL_MAX_DIST = 2048

MOE_GROUPS = 4
MOE_EXPERTS_PER_GROUP = 8
MOE_EXPERTS = MOE_GROUPS * MOE_EXPERTS_PER_GROUP
MOE_TOP_K = 2
MOE_FF = 512
MOE_ROW_BLOCK = 256

DEEPNORM_ALPHA = (2 * DEPTH) ** 0.25
DEEPNORM_BETA = (8 * DEPTH) ** -0.25
LN_EPS = 1e-5
RMS_EPS = 1e-6
L2_EPS = 1e-6

kernel_name = 'yoco_gdn_dilated_hmoe_deepnorm'


def layer_norm(x, gain, bias):
    xf = x.astype(jnp.float32)
    mu = xf.mean(-1, keepdims=True)
    var = jnp.square(xf - mu).mean(-1, keepdims=True)
    y = (xf - mu) * lax.rsqrt(var + LN_EPS) * gain.astype(jnp.float32) + bias.astype(jnp.float32)
    return y.astype(x.dtype)


def l2_normalize(x):
    xf = x.astype(jnp.float32)
    return xf * lax.rsqrt(jnp.sum(xf * xf, -1, keepdims=True) + L2_EPS)


def causal_depthwise_conv(x, w):
    k, c = w.shape
    return lax.conv_general_dilated(
        x, w.astype(x.dtype)[:, None, :], window_strides=(1,), padding=((k - 1, 0),),
        dimension_numbers=('NWC', 'WIO', 'NWC'), feature_group_count=c)


def chunk_gated_delta_rule(q, k, v, g, beta):
    b, h, s, dk = q.shape
    dv = v.shape[-1]
    c = GDN_CHUNK
    n = s // c
    q = (q * dk ** -0.5).reshape(b, h, n, c, dk)
    k = k.reshape(b, h, n, c, dk)
    v = v.reshape(b, h, n, c, dv)
    beta = beta.reshape(b, h, n, c, 1)
    g = jnp.cumsum(g.reshape(b, h, n, c), axis=-1)
    causal = jnp.tril(jnp.ones((c, c), dtype=bool))
    strict = jnp.tril(jnp.ones((c, c), dtype=bool), -1)
    decay = jnp.exp(jnp.where(causal, g[..., :, None] - g[..., None, :], -jnp.inf))
    kb = k * beta
    lower = jnp.where(strict, jnp.einsum('bhnid,bhnjd->bhnij', kb, k) * decay, 0.0)
    tmat = lower + jnp.eye(c, dtype=lower.dtype)
    u = lax.linalg.triangular_solve(tmat, v * beta, left_side=True, lower=True)
    w = lax.linalg.triangular_solve(tmat, kb * jnp.exp(g)[..., None], left_side=True, lower=True)
    qk = jnp.einsum('bhnid,bhnjd->bhnij', q, k) * decay
    g_last = g[..., -1:]
    k_dec = k * jnp.exp(g_last - g)[..., None]
    q_dec = q * jnp.exp(g)[..., None]
    chunk_decay = jnp.exp(g_last)[..., None]

    def step(state, xs):
        qd, kd, wi, ui, qki, cd = xs
        v_new = ui - jnp.einsum('bhcd,bhde->bhce', wi, state)
        out = jnp.einsum('bhcd,bhde->bhce', qd, state) + jnp.einsum('bhij,bhje->bhie', qki, v_new)
        state = state * cd + jnp.einsum('bhcd,bhce->bhde', kd, v_new)
        return state, out

    xs = tuple(jnp.moveaxis(t, 2, 0) for t in (q_dec, k_dec, w, u, qk, chunk_decay))
    state0 = jnp.zeros((b, h, dk, dv), jnp.float32)
    _, out = lax.scan(step, state0, xs)
    return jnp.moveaxis(out, 0, 2).reshape(b, h, s, dv)


def gated_deltanet(x, w_in, conv_w, a_log, dt_bias, norm_w, w_out):
    b, s, _ = x.shape
    proj = x @ w_in
    o1 = GDN_CONV_DIM
    o2 = o1 + GDN_V_DIM
    o3 = o2 + GDN_V_HEADS
    qkv = jax.nn.silu(causal_depthwise_conv(proj[..., :o1], conv_w))
    z = proj[..., o1:o2].reshape(b, s, GDN_V_HEADS, GDN_HEAD_DIM).astype(jnp.float32)
    beta = jax.nn.sigmoid(proj[..., o2:o3].astype(jnp.float32))
    g = -jnp.exp(a_log.astype(jnp.float32)) * jax.nn.softplus(
        proj[..., o3:].astype(jnp.float32) + dt_bias.astype(jnp.float32))
    rep = GDN_V_HEADS // GDN_QK_HEADS
    q = jnp.repeat(l2_normalize(qkv[..., :GDN_QK_DIM].reshape(b, s, GDN_QK_HEADS, GDN_HEAD_DIM)), rep, axis=2)
    k = jnp.repeat(l2_normalize(qkv[..., GDN_QK_DIM:2 * GDN_QK_DIM].reshape(b, s, GDN_QK_HEADS, GDN_HEAD_DIM)), rep, axis=2)
    v = qkv[..., 2 * GDN_QK_DIM:].reshape(b, s, GDN_V_HEADS, GDN_HEAD_DIM).astype(jnp.float32)
    o = chunk_gated_delta_rule(q.transpose(0, 2, 1, 3), k.transpose(0, 2, 1, 3), v.transpose(0, 2, 1, 3),
                               g.transpose(0, 2, 1), beta.transpose(0, 2, 1))
    o = o.transpose(0, 2, 1, 3)
    o = o * lax.rsqrt(jnp.mean(o * o, -1, keepdims=True) + RMS_EPS) * norm_w.astype(jnp.float32)
    o = o * jax.nn.silu(z)
    return o.reshape(b, s, GDN_V_DIM).astype(x.dtype) @ w_out


def to_dilated_blocks(t, dil):
    b, s = t.shape[:2]
    rest = t.shape[2:]
    length = s // dil
    nb = -(-length // DIL_BLOCK)
    t = jnp.swapaxes(t.reshape(b, length, dil, *rest), 1, 2)
    pad = [(0, 0), (0, 0), (0, nb * DIL_BLOCK - length)] + [(0, 0)] * len(rest)
    return jnp.pad(t, pad).reshape(b, dil, nb, DIL_BLOCK, *rest)


def from_dilated_blocks(t, s):
    b, dil = t.shape[:2]
    rest = t.shape[4:]
    length = s // dil
    t = t.reshape(b, dil, -1, *rest)[:, :, :length]
    return jnp.swapaxes(t, 1, 2).reshape(b, s, *rest)


def with_previous_block(t):
    prev = jnp.pad(t, [(0, 0), (0, 0), (1, 0)] + [(0, 0)] * (t.ndim - 3))[:, :, :-1]
    return jnp.concatenate([prev, t], axis=3)


def t5_bucket(dist):
    exact = REL_BUCKETS // 2
    distf = jnp.maximum(dist, exact).astype(jnp.float32)
    large = exact + (jnp.log(distf / exact) / math.log(REL_MAX_DIST / exact)
                     * (REL_BUCKETS - exact)).astype(jnp.int32)
    return jnp.where(dist < exact, dist, jnp.minimum(large, REL_BUCKETS - 1))


def dilated_band(rel_bias_g, window, dil, nb):
    qi = jnp.arange(DIL_BLOCK)[:, None]
    kj = jnp.arange(2 * DIL_BLOCK)[None, :]
    steps = DIL_BLOCK + qi - kj
    bias = rel_bias_g[t5_bucket(jnp.maximum(steps, 0) * dil)]
    bias = jnp.transpose(bias, (2, 0, 1)).astype(jnp.float32)
    key_index = jnp.arange(nb)[:, None, None] * DIL_BLOCK - DIL_BLOCK + kj[None]
    mask = (steps >= 0) & (steps <= window // dil) & (key_index >= 0)
    return bias, mask


def dilated_group_attention(q, k_blk, v_blk, bias, mask, dil):
    s = q.shape[1]
    qb = to_dilated_blocks(q, dil)
    k_band = with_previous_block(k_blk)
    v_band = with_previous_block(v_blk)
    scores = jnp.einsum('bdnqhe,bdnkhe->bdnhqk', qb, k_band,
                        preferred_element_type=jnp.float32) * DIL_HEAD_DIM ** -0.5 + bias
    scores = jnp.where(mask[:, None], scores, -jnp.inf)
    m = scores.max(-1, keepdims=True)
    p = jnp.exp(scores - m)
    den = p.sum(-1)
    o = jnp.einsum('bdnhqk,bdnkhe->bdnqhe', p, v_band.astype(jnp.float32)) / jnp.swapaxes(den, -1, -2)[..., None]
    lse = jnp.swapaxes(m[..., 0] + jnp.log(den), -1, -2)
    return from_dilated_blocks(o, s), from_dilated_blocks(lse, s)


def shared_dilated_kv(h, w_k, w_v, rel_bias):
    b, s, _ = h.shape
    k = (h @ w_k).reshape(b, s, DIL_HEADS, DIL_HEAD_DIM)
    v = (h @ w_v).reshape(b, s, DIL_HEADS, DIL_HEAD_DIM)
    k_blks, v_blks, bands = [], [], []
    for gi, (window, dil) in enumerate(DIL_CONFIGS):
        kb = to_dilated_blocks(k, dil)
        k_blks.append(kb)
        v_blks.append(to_dilated_blocks(v, dil))
        bands.append(dilated_band(rel_bias[:, gi * DIL_HEADS:(gi + 1) * DIL_HEADS], window, dil, kb.shape[2]))
    return k_blks, v_blks, bands


def dilated_mixer(x, w_q, w_o, k_blks, v_blks, bands):
    b, s, _ = x.shape
    q = (x @ w_q).reshape(b, s, N_DIL, DIL_HEADS, DIL_HEAD_DIM)
    outs, lses = [], []
    for gi, (window, dil) in enumerate(DIL_CONFIGS):
        bias, mask = bands[gi]
        o, lse = dilated_group_attention(q[:, :, gi], k_blks[gi], v_blks[gi], bias, mask, dil)
        outs.append(o)
        lses.append(lse)
    wts = jax.nn.softmax(jnp.stack(lses), axis=0)
    o = jnp.sum(wts[..., None] * jnp.stack(outs), axis=0)
    return o.reshape(b, s, DIL_KV_DIM).astype(x.dtype) @ w_o


def grouped_expert_mlp(xt, expert_ids, gates, w1, w3, w2):
    t, d = xt.shape
    n_pairs = expert_ids.shape[0]
    token_of_pair = (jnp.arange(n_pairs) // MOE_TOP_K).astype(jnp.int32)
    onehot = (expert_ids[:, None] == jnp.arange(MOE_EXPERTS)[None, :]).astype(jnp.int32)
    counts = onehot.sum(0)
    rank = jnp.sum((jnp.cumsum(onehot, 0) - 1) * onehot, axis=1)
    padded = (counts + MOE_ROW_BLOCK - 1) // MOE_ROW_BLOCK * MOE_ROW_BLOCK
    seg_end = jnp.cumsum(padded)
    dest = (seg_end - padded)[expert_ids] + rank
    n_blocks = -(-n_pairs // MOE_ROW_BLOCK) + MOE_EXPERTS
    n_rows = n_blocks * MOE_ROW_BLOCK
    row_token = jnp.full((n_rows,), t, jnp.int32).at[dest].set(token_of_pair)
    row_gate = jnp.zeros((n_rows,), xt.dtype).at[dest].set(gates.astype(xt.dtype))
    block_expert = jnp.minimum(
        jnp.searchsorted(seg_end, jnp.arange(n_blocks) * MOE_ROW_BLOCK, side='right'), MOE_EXPERTS - 1)
    x_rows = jnp.concatenate([xt, jnp.zeros((1, d), xt.dtype)])[row_token].reshape(n_blocks, MOE_ROW_BLOCK, d)

    def expert_block(args):
        xb, e = args
        hmid = jax.nn.silu(xb @ w1[e]) * (xb @ w3[e])
        return hmid @ w2[e]

    y_rows = lax.map(expert_block, (x_rows, block_expert)).reshape(n_rows, d)
    return jnp.zeros((t + 1, d), y_rows.dtype).at[row_token].add(y_rows * row_gate[:, None])[:t]


def hierarchical_moe(x, w_group, b_group, w_expert, b_expert, w1, w3, w2):
    b, s, d = x.shape
    t = b * s
    xt = x.reshape(t, d)
    group_logits = (xt @ w_group + b_group).astype(jnp.float32)
    group_idx = jnp.argmax(group_logits, -1)
    group_gate = jnp.take_along_axis(jax.nn.softmax(group_logits, -1), group_idx[:, None], 1)
    expert_logits = (xt @ w_expert + b_expert).astype(jnp.float32).reshape(t, MOE_GROUPS, MOE_EXPERTS_PER_GROUP)
    local_logits = jnp.take_along_axis(expert_logits, group_idx[:, None, None], 1)[:, 0]
    top_p, top_i = lax.top_k(jax.nn.softmax(local_logits, -1), MOE_TOP_K)
    gates = group_gate * top_p / top_p.sum(-1, keepdims=True)
    expert_ids = (group_idx[:, None] * MOE_EXPERTS_PER_GROUP + top_i).astype(jnp.int32)
    y = grouped_expert_mlp(xt, expert_ids.reshape(-1), gates.reshape(-1), w1, w3, w2)
    return y.reshape(b, s, d)


def setup_inputs(seed: int = 0) -> dict:
    key = jax.random.key(seed)
    ks = jax.random.split(key, 24)
    f32 = jnp.float32

    def normal(k, shape, scale):
        return jax.random.normal(k, shape, f32) * scale

    dt = jnp.exp(jax.random.uniform(ks[4], (N_A_LAYERS, GDN_V_HEADS), f32, math.log(1e-3), math.log(1e-1)))
    return {
        'x': normal(ks[0], (BATCH, SEQ, D_MODEL), 1.0),
        'gdn_w_in': normal(ks[1], (N_A_LAYERS, D_MODEL, GDN_IN_DIM), D_MODEL ** -0.5),
        'gdn_conv': normal(ks[2], (N_A_LAYERS, GDN_CONV, GDN_CONV_DIM), GDN_CONV ** -0.5),
        'gdn_a_log': jnp.log(jax.random.uniform(ks[3], (N_A_LAYERS, GDN_V_HEADS), f32, 1.0, 16.0)),
        'gdn_dt_bias': dt + jnp.log(-jnp.expm1(-dt)),
        'gdn_norm': 1.0 + normal(ks[5], (N_A_LAYERS, GDN_HEAD_DIM), 0.02),
        'gdn_w_out': normal(ks[6], (N_A_LAYERS, GDN_V_DIM, D_MODEL), GDN_V_DIM ** -0.5 * DEEPNORM_BETA),
        'kv_w_k': normal(ks[7], (D_MODEL, DIL_KV_DIM), D_MODEL ** -0.5),
        'kv_w_v': normal(ks[8], (D_MODEL, DIL_KV_DIM), D_MODEL ** -0.5),
        'dil_w_q': normal(ks[9], (N_B_LAYERS, D_MODEL, DIL_Q_DIM), D_MODEL ** -0.5),
        'dil_w_o': normal(ks[10], (N_B_LAYERS, DIL_KV_DIM, D_MODEL), DIL_KV_DIM ** -0.5 * DEEPNORM_BETA),
        'rel_bias': normal(ks[11], (REL_BUCKETS, N_DIL * DIL_HEADS), 0.2),
        'ln_gain': 1.0 + normal(ks[12], (DEPTH, 2, D_MODEL), 0.02),
        'ln_bias': normal(ks[13], (DEPTH, 2, D_MODEL), 0.02),
        'moe_w_group': normal(ks[14], (DEPTH, D_MODEL, MOE_GROUPS), D_MODEL ** -0.5),
        'moe_b_group': normal(ks[15], (DEPTH, MOE_GROUPS), 0.01),
        'moe_w_expert': normal(ks[16], (DEPTH, D_MODEL, MOE_EXPERTS), D_MODEL ** -0.5),
        'moe_b_expert': normal(ks[17], (DEPTH, MOE_EXPERTS), 0.01),
        'moe_w1': normal(ks[18], (DEPTH, MOE_EXPERTS, D_MODEL, MOE_FF), D_MODEL ** -0.5),
        'moe_w3': normal(ks[19], (DEPTH, MOE_EXPERTS, D_MODEL, MOE_FF), D_MODEL ** -0.5),
        'moe_w2': normal(ks[20], (DEPTH, MOE_EXPERTS, MOE_FF, D_MODEL), MOE_FF ** -0.5 * DEEPNORM_BETA),
    }


def reference(x, gdn_w_in, gdn_conv, gdn_a_log, gdn_dt_bias, gdn_norm, gdn_w_out, kv_w_k, kv_w_v,
              dil_w_q, dil_w_o, rel_bias, ln_gain, ln_bias, moe_w_group, moe_b_group, moe_w_expert,
              moe_b_expert, moe_w1, moe_w3, moe_w2):
    h = x
    shared = None
    for layer in range(DEPTH):
        if layer < N_A_LAYERS:
            mix = gated_deltanet(h, gdn_w_in[layer], gdn_conv[layer], gdn_a_log[layer],
                                 gdn_dt_bias[layer], gdn_norm[layer], gdn_w_out[layer])
        else:
            if layer == N_A_LAYERS:
                shared = shared_dilated_kv(h, kv_w_k, kv_w_v, rel_bias)
            j = layer - N_A_LAYERS
            mix = dilated_mixer(h, dil_w_q[j], dil_w_o[j], *shared)
        h = layer_norm(DEEPNORM_ALPHA * h + mix, ln_gain[layer, 0], ln_bias[layer, 0])
        ffn = hierarchical_moe(h, moe_w_group[layer], moe_b_group[layer], moe_w_expert[layer],
                               moe_b_expert[layer], moe_w1[layer], moe_w3[layer], moe_w2[layer])
        h = layer_norm(DEEPNORM_ALPHA * h + ffn, ln_gain[layer, 1], ln_bias[layer, 1])
    return h
```

```python
import functools
import math

import jax
import jax.numpy as jnp
from jax import lax
from jax.experimental import pallas as pl
from jax.experimental.pallas import tpu as pltpu

F32 = jnp.float32
BF16 = jnp.bfloat16
HIGHEST = lax.Precision.HIGHEST

DEPTH = 4
N_A_LAYERS = DEPTH // 2
HEAD_DIM = 128
GDN_QK_HEADS = 16
GDN_V_HEADS = 32
GDN_CONV = 4
GDN_CHUNK = 64
GDN_QK_DIM = GDN_QK_HEADS * HEAD_DIM
GDN_V_DIM = GDN_V_HEADS * HEAD_DIM
GDN_CONV_DIM = 2 * GDN_QK_DIM + GDN_V_DIM
DIL_CONFIGS = ((128, 1), (512, 4), (2048, 16))
N_DIL = len(DIL_CONFIGS)
DIL_HEADS = 8
DIL_BLOCK = 128
DIL_KV_DIM = DIL_HEADS * HEAD_DIM
REL_BUCKETS = 32
REL_MAX_DIST = 2048
MOE_GROUPS = 4
MOE_EXPERTS_PER_GROUP = 8
MOE_EXPERTS = MOE_GROUPS * MOE_EXPERTS_PER_GROUP
MOE_TOP_K = 2
DEEPNORM_ALPHA = (2 * DEPTH) ** 0.25
LN_EPS = 1e-5
RMS_EPS = 1e-6
L2_EPS = 1e-6

LANES = 128
SUBLANES = 8
VMEM_LIMIT_BYTES = 56 * 1024 * 1024

MM_TM = 512
MM_TN = 1024
LN_TM = 256
GATE_TM = 512
GDN_TB = 512
MOE_TM = 256
ROUTER_TM = 512
COMBINE_TM = 256


def _params(*semantics):
    return pltpu.CompilerParams(dimension_semantics=semantics, vmem_limit_bytes=VMEM_LIMIT_BYTES)


def _layer_norm_rows(y, gain, bias):
    mu = jnp.mean(y, axis=-1, keepdims=True)
    yc = y - mu
    var = jnp.mean(yc * yc, axis=-1, keepdims=True)
    return yc * lax.rsqrt(var + LN_EPS) * gain + bias


def _matmul_kernel(a_ref, b_ref, o_ref):
    o_ref[...] = jnp.dot(a_ref[...].astype(BF16), b_ref[...],
                         preferred_element_type=F32).astype(o_ref.dtype)


def matmul(a, b, out_dtype=F32, tm=MM_TM, tn=MM_TN):
    m, k = a.shape
    n = b.shape[1]
    tm, tn = min(tm, m), min(tn, n)
    assert m % tm == 0 and n % tn == 0
    return pl.pallas_call(
        _matmul_kernel,
        out_shape=jax.ShapeDtypeStruct((m, n), out_dtype),
        grid=(n // tn, m // tm),
        in_specs=[pl.BlockSpec((tm, k), lambda j, i: (i, 0)),
                  pl.BlockSpec((k, tn), lambda j, i: (0, j))],
        out_specs=pl.BlockSpec((tm, tn), lambda j, i: (i, j)),
        compiler_params=_params("parallel", "parallel"),
        name="matmul",
    )(a, b)


def _proj_ln_kernel(a_ref, w_ref, h_ref, gain_ref, bias_ref, o32_ref, o16_ref):
    mix = jnp.dot(a_ref[...].astype(BF16), w_ref[...], preferred_element_type=F32)
    y = _layer_norm_rows(DEEPNORM_ALPHA * h_ref[...] + mix, gain_ref[...], bias_ref[...])
    o32_ref[...] = y
    o16_ref[...] = y.astype(BF16)


def proj_residual_ln(a, w, h, gain, bias, tm=LN_TM):
    m, k = a.shape
    d = w.shape[1]
    tm = min(tm, m)
    assert m % tm == 0
    return pl.pallas_call(
        _proj_ln_kernel,
        out_shape=(jax.ShapeDtypeStruct((m, d), F32), jax.ShapeDtypeStruct((m, d), BF16)),
        grid=(m // tm,),
        in_specs=[pl.BlockSpec((tm, k), lambda i: (i, 0)),
                  pl.BlockSpec((k, d), lambda i: (0, 0)),
                  pl.BlockSpec((tm, d), lambda i: (i, 0)),
                  pl.BlockSpec((1, d), lambda i: (0, 0)),
                  pl.BlockSpec((1, d), lambda i: (0, 0))],
        out_specs=(pl.BlockSpec((tm, d), lambda i: (i, 0)),
                   pl.BlockSpec((tm, d), lambda i: (i, 0))),
        compiler_params=_params("parallel"),
        name="proj_residual_ln",
    )(a, w, h, gain.reshape(1, d), bias.reshape(1, d))


def _gdn_gate_kernel(h_ref, w_ref, alog_ref, dt_ref, o_ref):
    tm = h_ref.shape[0]
    ba = jnp.dot(h_ref[...], w_ref[...], precision=HIGHEST, preferred_element_type=F32)
    xa = ba + dt_ref[...]
    softplus = jnp.maximum(xa, 0.0) + jnp.log(1.0 + jnp.exp(-jnp.abs(xa)))
    g = -jnp.exp(alog_ref[...]) * softplus
    row = lax.broadcasted_iota(jnp.int32, (tm, tm), 0)
    col = lax.broadcasted_iota(jnp.int32, (tm, tm), 1)
    tri = ((row // GDN_CHUNK == col // GDN_CHUNK) & (col <= row)).astype(F32)
    gc = jnp.dot(tri, g, precision=HIGHEST, preferred_element_type=F32)
    lane = lax.broadcasted_iota(jnp.int32, ba.shape, 1)
    o_ref[...] = jnp.where(lane < GDN_V_HEADS, jax.nn.sigmoid(ba), gc)


def gdn_gates(h, w_ba, a_log, dt_bias, tm=GATE_TM):
    m, d = h.shape
    tm = min(tm, m)
    pad = LANES - 2 * GDN_V_HEADS
    w = jnp.pad(w_ba, ((0, 0), (0, pad)))
    zeros = jnp.zeros((GDN_V_HEADS,), F32)
    alog = jnp.concatenate([zeros, a_log, jnp.zeros((pad,), F32)]).reshape(1, LANES)
    dt = jnp.concatenate([zeros, dt_bias, jnp.zeros((pad,), F32)]).reshape(1, LANES)
    return pl.pallas_call(
        _gdn_gate_kernel,
        out_shape=jax.ShapeDtypeStruct((m, LANES), F32),
        grid=(m // tm,),
        in_specs=[pl.BlockSpec((tm, d), lambda i: (i, 0)),
                  pl.BlockSpec((d, LANES), lambda i: (0, 0)),
                  pl.BlockSpec((1, LANES), lambda i: (0, 0)),
                  pl.BlockSpec((1, LANES), lambda i: (0, 0))],
        out_specs=pl.BlockSpec((tm, LANES), lambda i: (i, 0)),
        compiler_params=_params("parallel"),
        name="gdn_gates",
    )(h, w, alog, dt)


def _dot_f32(a, b):
    return jnp.dot(a, b, precision=HIGHEST, preferred_element_type=F32)


def _dot_bf16(a, b):
    return jnp.dot(a.astype(BF16), b.astype(BF16), preferred_element_type=F32)


def _dot_nt(a, b):
    return lax.dot_general(a.astype(BF16), b.astype(BF16), (((1,), (1,)), ((), ())),
                           preferred_element_type=F32)


def _dot_tn(a, b):
    return lax.dot_general(a.astype(BF16), b.astype(BF16), (((0,), (0,)), ((), ())),
                           preferred_element_type=F32)


def _unit_lower_inverse(a):
    c = a.shape[0]
    row = lax.broadcasted_iota(jnp.int32, (c, c), 0)
    col = lax.broadcasted_iota(jnp.int32, (c, c), 1)
    eye = (row == col).astype(F32)
    in_block = (row // 16) == (col // 16)
    d = jnp.where(in_block, a, 0.0)
    e = a - d
    n1 = -d
    p = eye + n1
    n2 = _dot_f32(n1, n1)
    p = p + _dot_f32(p, n2)
    n4 = _dot_f32(n2, n2)
    p = p + _dot_f32(p, n4)
    n8 = _dot_f32(n4, n4)
    dinv = p + _dot_f32(p, n8)
    nb = _dot_f32(dinv, e)
    nb2 = _dot_f32(nb, nb)
    q = eye - nb
    q = q + _dot_f32(q, nb2)
    return _dot_f32(q, dinv)


def _gdn_kernel(q_ref, k_ref, v_ref, z_ref, wq_ref, wk_ref, wv_ref, gcol_ref, grow_ref, nw_ref,
                o_ref, qx, kx, vx, qn, kn, vn, state):
    blk = pl.program_id(2)
    tb = q_ref.shape[0]
    n_chunks = tb // GDN_CHUNK
    c = GDN_CHUNK
    tail = SUBLANES

    @pl.when(blk == 0)
    def _():
        qx[0:tail, :] = jnp.zeros((tail, qx.shape[1]), F32)
        kx[0:tail, :] = jnp.zeros((tail, kx.shape[1]), F32)
        vx[0:tail, :] = jnp.zeros((tail, vx.shape[1]), F32)
        state[...] = jnp.zeros(state.shape, F32)

    def conv_silu(x_ref, ext, w_ref):
        ext[tail:tail + tb, :] = x_ref[...].astype(F32)
        y = None
        for j in range(GDN_CONV):
            term = w_ref[j:j + 1, :] * ext[pl.ds(tail - (GDN_CONV - 1) + j, tb), :]
            y = term if y is None else y + term
        ext[0:tail, :] = ext[tb:tb + tail, :]
        return y * jax.nn.sigmoid(y)

    def l2n(x):
        return x * lax.rsqrt(jnp.sum(x * x, axis=-1, keepdims=True) + L2_EPS)

    qn[...] = l2n(conv_silu(q_ref, qx, wq_ref)) * (HEAD_DIM ** -0.5)
    kn[...] = l2n(conv_silu(k_ref, kx, wk_ref))
    vn[...] = conv_silu(v_ref, vx, wv_ref)

    row = lax.broadcasted_iota(jnp.int32, (c, c), 0)
    col = lax.broadcasted_iota(jnp.int32, (c, c), 1)
    causal = col <= row
    strict = col < row
    norm_w = nw_ref[...]

    def chunk_body(ci, carry):
        r0 = pl.multiple_of(ci * c, c)
        qc = qn[pl.ds(r0, c), :]
        kc = kn[pl.ds(r0, c), :]
        gcol = gcol_ref[0, 0, ci]
        grow = grow_ref[0, 0, ci]
        gram = _dot_nt(kc, kc)
        qk = _dot_nt(qc, kc)
        for hh in range(2):
            gc = gcol[:, hh:hh + 1]
            beta = gcol[:, 2 + hh:3 + hh]
            gr = grow[hh:hh + 1, :]
            decay = jnp.exp(jnp.where(causal, gc - gr, -jnp.inf))
            a = jnp.where(strict, beta * gram * decay, 0.0)
            tinv = _unit_lower_inverse(a)
            vc = vn[pl.ds(r0, c), hh * HEAD_DIM:(hh + 1) * HEAD_DIM]
            eg = jnp.exp(gc)
            rhs = jnp.concatenate([vc * beta, kc * (beta * eg)], axis=1)
            uw = _dot_f32(tinv, rhs)
            u = uw[:, :HEAD_DIM]
            w = uw[:, HEAD_DIM:]
            g_last = gc[c - 1:c, :]
            s = state[hh]
            v_new = u - _dot_bf16(w, s)
            out = _dot_bf16(qc * eg, s) + _dot_bf16(qk * decay, v_new)
            state[hh] = s * jnp.exp(g_last) + _dot_tn(kc * jnp.exp(g_last - gc), v_new)
            out = out * lax.rsqrt(jnp.mean(out * out, axis=-1, keepdims=True) + RMS_EPS) * norm_w
            zc = z_ref[pl.ds(r0, c), hh * HEAD_DIM:(hh + 1) * HEAD_DIM].astype(F32)
            out = out * (zc * jax.nn.sigmoid(zc))
            o_ref[pl.ds(r0, c), hh * HEAD_DIM:(hh + 1) * HEAD_DIM] = out.astype(o_ref.dtype)
        return carry

    lax.fori_loop(0, n_chunks, chunk_body, 0)


def gdn_delta(proj, conv_w, gcol, grow, norm_w, batch, seq, tb=GDN_TB):
    t = proj.shape[0]
    tb = min(tb, seq)
    nblk = seq // tb
    ncb = tb // GDN_CHUNK
    hd = HEAD_DIM
    kq = GDN_QK_HEADS
    kv2 = GDN_CONV_DIM // (2 * hd)
    v0 = (2 * GDN_QK_DIM) // (2 * hd)
    z0 = GDN_CONV_DIM // (2 * hd)
    del kv2
    return pl.pallas_call(
        _gdn_kernel,
        out_shape=jax.ShapeDtypeStruct((t, GDN_V_DIM), BF16),
        grid=(batch, GDN_QK_HEADS, nblk),
        in_specs=[
            pl.BlockSpec((tb, hd), lambda b, h, i: (b * nblk + i, h)),
            pl.BlockSpec((tb, hd), lambda b, h, i: (b * nblk + i, kq + h)),
            pl.BlockSpec((tb, 2 * hd), lambda b, h, i: (b * nblk + i, v0 + h)),
            pl.BlockSpec((tb, 2 * hd), lambda b, h, i: (b * nblk + i, z0 + h)),
            pl.BlockSpec((GDN_CONV, hd), lambda b, h, i: (0, h)),
            pl.BlockSpec((GDN_CONV, hd), lambda b, h, i: (0, kq + h)),
            pl.BlockSpec((GDN_CONV, 2 * hd), lambda b, h, i: (0, v0 + h)),
            pl.BlockSpec((1, 1, ncb, GDN_CHUNK, 4), lambda b, h, i: (b, h, i, 0, 0)),
            pl.BlockSpec((1, 1, ncb, 2, GDN_CHUNK), lambda b, h, i: (b, h, i, 0, 0)),
            pl.BlockSpec((1, hd), lambda b, h, i: (0, 0)),
        ],
        out_specs=pl.BlockSpec((tb, 2 * hd), lambda b, h, i: (b * nblk + i, h)),
        scratch_shapes=[
            pltpu.VMEM((tb + SUBLANES, hd), F32),
            pltpu.VMEM((tb + SUBLANES, hd), F32),
            pltpu.VMEM((tb + SUBLANES, 2 * hd), F32),
            pltpu.VMEM((tb, hd), F32),
            pltpu.VMEM((tb, hd), F32),
            pltpu.VMEM((tb, 2 * hd), F32),
            pltpu.VMEM((2, hd, hd), F32),
        ],
        compiler_params=_params("parallel", "parallel", "arbitrary"),
        name="gdn_delta",
    )(proj, proj, proj, proj, conv_w, conv_w, conv_w, gcol, grow, norm_w.reshape(1, hd))


def gated_deltanet_layer(h32, h16, w_in, conv_w, a_log, dt_bias, norm_w, w_out, gain, bias, batch, seq):
    t = h32.shape[0]
    main = GDN_CONV_DIM + GDN_V_DIM
    proj = matmul(h16, w_in[:, :main].astype(BF16))
    gates = gdn_gates(h32, w_in[:, main:], a_log, dt_bias)
    nc = seq // GDN_CHUNK
    beta = gates[:, :GDN_V_HEADS].reshape(batch, nc, GDN_CHUNK, GDN_QK_HEADS, 2)
    gc = gates[:, GDN_V_HEADS:2 * GDN_V_HEADS].reshape(batch, nc, GDN_CHUNK, GDN_QK_HEADS, 2)
    gcol = jnp.transpose(jnp.concatenate([gc, beta], axis=-1), (0, 3, 1, 2, 4))
    grow = jnp.transpose(gc, (0, 3, 1, 4, 2))
    o = gdn_delta(proj, conv_w, gcol, grow, norm_w, batch, seq)
    del t
    return proj_residual_ln(o, w_out.astype(BF16), h32, gain, bias)


def _dil_attn_kernel(q_ref, kp_ref, ko_ref, vp_ref, vo_ref, bias_ref, o_ref, lse_ref):
    lb = pl.program_id(2)
    blk = DIL_BLOCK
    qi = lax.broadcasted_iota(jnp.int32, (blk, blk), 0)
    kj = lax.broadcasted_iota(jnp.int32, (blk, blk), 1)
    mask_prev = (qi <= kj) & (lb > 0)
    mask_own = kj <= qi
    lane = lax.broadcasted_iota(jnp.int32, (blk, LANES), 1)
    lse_all = jnp.zeros((blk, LANES), F32)
    scale = HEAD_DIM ** -0.5
    for hd in range(DIL_HEADS):
        sl = slice(hd * HEAD_DIM, (hd + 1) * HEAD_DIM)
        q = q_ref[0, :, sl]
        bias = bias_ref[hd]
        s_prev = _dot_nt(q, kp_ref[0, :, sl]) * scale + bias[:, :blk]
        s_own = _dot_nt(q, ko_ref[0, :, sl]) * scale + bias[:, blk:]
        s_prev = jnp.where(mask_prev, s_prev, -jnp.inf)
        s_own = jnp.where(mask_own, s_own, -jnp.inf)
        m = jnp.maximum(jnp.max(s_prev, axis=-1, keepdims=True), jnp.max(s_own, axis=-1, keepdims=True))
        p_prev = jnp.exp(s_prev - m)
        p_own = jnp.exp(s_own - m)
        den = jnp.sum(p_prev, axis=-1, keepdims=True) + jnp.sum(p_own, axis=-1, keepdims=True)
        o = _dot_bf16(p_prev, vp_ref[0, :, sl]) + _dot_bf16(p_own, vo_ref[0, :, sl])
        o_ref[0, :, sl] = o / den
        lse_all = jnp.where(lane == hd, m + jnp.log(den), lse_all)
    lse_ref[0] = lse_all


def dilated_group_attention(q, k, v, bias, gi, dil, batch, seq):
    length = seq // dil
    nb = length // DIL_BLOCK
    kvd = DIL_KV_DIM
    qv = q.reshape(batch, length, dil * N_DIL * kvd)
    kview = k.reshape(batch, length, dil * kvd)
    vview = v.reshape(batch, length, dil * kvd)
    blk = DIL_BLOCK
    own = lambda b, r, i: (b, i, r)
    prev = lambda b, r, i: (b, jnp.maximum(i - 1, 0), r)
    o, lse = pl.pallas_call(
        _dil_attn_kernel,
        out_shape=(jax.ShapeDtypeStruct((batch, length, dil * kvd), F32),
                   jax.ShapeDtypeStruct((batch, length, dil * LANES), F32)),
        grid=(batch, dil, nb),
        in_specs=[
            pl.BlockSpec((1, blk, kvd), lambda b, r, i: (b, i, r * N_DIL + gi)),
            pl.BlockSpec((1, blk, kvd), prev),
            pl.BlockSpec((1, blk, kvd), own),
            pl.BlockSpec((1, blk, kvd), prev),
            pl.BlockSpec((1, blk, kvd), own),
            pl.BlockSpec((DIL_HEADS, blk, 2 * blk), lambda b, r, i: (0, 0, 0)),
        ],
        out_specs=(pl.BlockSpec((1, blk, kvd), own), pl.BlockSpec((1, blk, LANES), own)),
        compiler_params=_params("parallel", "parallel", "parallel"),
        name=f"dilated_attention_g{gi}",
    )(qv, kview, kview, vview, vview, bias)
    return o.reshape(batch * seq, kvd), lse.reshape(batch * seq, LANES)


def _dil_out_kernel(o0_ref, o1_ref, o2_ref, l0_ref, l1_ref, l2_ref, w_ref, h_ref, gain_ref, bias_ref,
                    o32_ref, o16_ref, mix_ref):
    l0, l1, l2 = l0_ref[...], l1_ref[...], l2_ref[...]
    m = jnp.maximum(jnp.maximum(l0, l1), l2)
    e0, e1, e2 = jnp.exp(l0 - m), jnp.exp(l1 - m), jnp.exp(l2 - m)
    inv = 1.0 / (e0 + e1 + e2)
    w0, w1, w2 = e0 * inv, e1 * inv, e2 * inv
    for hd in range(DIL_HEADS):
        sl = slice(hd * HEAD_DIM, (hd + 1) * HEAD_DIM)
        mix_ref[:, sl] = (w0[:, hd:hd + 1] * o0_ref[:, sl] + w1[:, hd:hd + 1] * o1_ref[:, sl]
                          + w2[:, hd:hd + 1] * o2_ref[:, sl])
    mix = jnp.dot(mix_ref[...].astype(BF16), w_ref[...], preferred_element_type=F32)
    y = _layer_norm_rows(DEEPNORM_ALPHA * h_ref[...] + mix, gain_ref[...], bias_ref[...])
    o32_ref[...] = y
    o16_ref[...] = y.astype(BF16)


def dilated_out(outs, lses, w_o, h, gain, bias, tm=LN_TM):
    m, kvd = outs[0].shape
    d = w_o.shape[1]
    tm = min(tm, m)
    row = lambda i: (i, 0)
    fixed = lambda i: (0, 0)
    return pl.pallas_call(
        _dil_out_kernel,
        out_shape=(jax.ShapeDtypeStruct((m, d), F32), jax.ShapeDtypeStruct((m, d), BF16)),
        grid=(m // tm,),
        in_specs=[pl.BlockSpec((tm, kvd), row)] * 3 + [pl.BlockSpec((tm, LANES), row)] * 3 + [
            pl.BlockSpec((kvd, d), fixed), pl.BlockSpec((tm, d), row),
            pl.BlockSpec((1, d), fixed), pl.BlockSpec((1, d), fixed)],
        out_specs=(pl.BlockSpec((tm, d), row), pl.BlockSpec((tm, d), row)),
        scratch_shapes=[pltpu.VMEM((tm, kvd), F32)],
        compiler_params=_params("parallel"),
        name="dilated_out_ln",
    )(*outs, *lses, w_o, h, gain.reshape(1, d), bias.reshape(1, d))


def _t5_bucket(dist):
    exact = REL_BUCKETS // 2
    distf = jnp.maximum(dist, exact).astype(F32)
    large = exact + (jnp.log(distf / exact) / math.log(REL_MAX_DIST / exact)
                     * (REL_BUCKETS - exact)).astype(jnp.int32)
    return jnp.where(dist < exact, dist, jnp.minimum(large, REL_BUCKETS - 1))


def _band_bias(rel_bias_g, dil):
    qi = jnp.arange(DIL_BLOCK)[:, None]
    kj = jnp.arange(2 * DIL_BLOCK)[None, :]
    steps = DIL_BLOCK + qi - kj
    bias = rel_bias_g[_t5_bucket(jnp.maximum(steps, 0) * dil)]
    return jnp.transpose(bias, (2, 0, 1)).astype(F32)


def _router_kernel(h_ref, w_ref, b_ref, o_ref):
    logits = jnp.dot(h_ref[...], w_ref[...], precision=HIGHEST, preferred_element_type=F32) + b_ref[...]
    lane = lax.broadcasted_iota(jnp.int32, logits.shape, 1)
    neg = -jnp.inf
    big = jnp.int32(LANES)
    is_group = lane < MOE_GROUPS
    gl = jnp.where(is_group, logits, neg)
    gmax = jnp.max(gl, axis=-1, keepdims=True)
    gidx = jnp.min(jnp.where(gl == gmax, lane, big), axis=-1, keepdims=True)
    gden = jnp.sum(jnp.exp(gl - gmax), axis=-1, keepdims=True)
    group_gate = 1.0 / gden
    lo = MOE_GROUPS + gidx * MOE_EXPERTS_PER_GROUP
    in_group = (lane >= lo) & (lane < lo + MOE_EXPERTS_PER_GROUP)
    ll = jnp.where(in_group, logits, neg)
    m1 = jnp.max(ll, axis=-1, keepdims=True)
    i1 = jnp.min(jnp.where(ll == m1, lane, big), axis=-1, keepdims=True)
    rest = jnp.where(lane == i1, neg, ll)
    m2 = jnp.max(rest, axis=-1, keepdims=True)
    i2 = jnp.min(jnp.where(rest == m2, lane, big), axis=-1, keepdims=True)
    den = jnp.sum(jnp.exp(ll - m1), axis=-1, keepdims=True)
    p1 = 1.0 / den
    p2 = jnp.exp(m2 - m1) / den
    g1 = group_gate * p1 / (p1 + p2)
    g2 = group_gate * p2 / (p1 + p2)
    id1 = (i1 - MOE_GROUPS).astype(F32)
    id2 = (i2 - MOE_GROUPS).astype(F32)
    o_ref[...] = jnp.where(lane == 0, id1, jnp.where(lane == 1, id2, jnp.where(lane == 2, g1, g2)))


def moe_router(h, w_group, b_group, w_expert, b_expert, tm=ROUTER_TM):
    m, d = h.shape
    tm = min(tm, m)
    pad = LANES - MOE_GROUPS - MOE_EXPERTS
    w = jnp.pad(jnp.concatenate([w_group, w_expert], axis=1), ((0, 0), (0, pad)))
    b = jnp.pad(jnp.concatenate([b_group, b_expert]), (0, pad)).reshape(1, LANES)
    out = pl.pallas_call(
        _router_kernel,
        out_shape=jax.ShapeDtypeStruct((m, LANES), F32),
        grid=(m // tm,),
        in_specs=[pl.BlockSpec((tm, d), lambda i: (i, 0)),
                  pl.BlockSpec((d, LANES), lambda i: (0, 0)),
                  pl.BlockSpec((1, LANES), lambda i: (0, 0))],
        out_specs=pl.BlockSpec((tm, LANES), lambda i: (i, 0)),
        compiler_params=_params("parallel"),
        name="moe_router",
    )(h, w, b)
    return out[:, :MOE_TOP_K].astype(jnp.int32), out[:, MOE_TOP_K:2 * MOE_TOP_K]


def _row_copy(src_hbm, src_row, dst, slot, dst_row, sem):
    return pltpu.make_async_copy(src_hbm.at[pl.ds(src_row, 1)], dst.at[slot, pl.ds(dst_row, 1)], sem.at[slot])


def _expert_kernel(bexp_ref, rtok_ref, nused_ref, h_hbm, gate_ref, w1_ref, w3_ref, w2_ref, y_ref, xbuf, sem):
    del bexp_ref
    i = pl.program_id(0)
    tm = xbuf.shape[1]
    n_used = nused_ref[0]
    slot = i % 2

    def issue(block, to_slot):
        def body(r, carry):
            _row_copy(h_hbm, rtok_ref[block * tm + r], xbuf, to_slot, r, sem).start()
            return carry
        lax.fori_loop(0, tm, body, 0)

    @pl.when(i == 0)
    def _():
        issue(0, 0)

    @pl.when(i + 1 < n_used)
    def _():
        issue(i + 1, 1 - slot)

    @pl.when(i < n_used)
    def _():
        def wait_body(r, carry):
            _row_copy(h_hbm, 0, xbuf, slot, r, sem).wait()
            return carry
        lax.fori_loop(0, tm, wait_body, 0)
        x = xbuf[slot].astype(BF16)
        h1 = jnp.dot(x, w1_ref[0], preferred_element_type=F32)
        h3 = jnp.dot(x, w3_ref[0], preferred_element_type=F32)
        hmid = (h1 * jax.nn.sigmoid(h1) * h3).astype(BF16)
        y = jnp.dot(hmid, w2_ref[0], preferred_element_type=F32)
        y_ref[...] = y * gate_ref[...]

    @pl.when(i >= n_used)
    def _():
        y_ref[...] = jnp.zeros(y_ref.shape, y_ref.dtype)


def moe_experts(h, row_token, row_gate, block_expert, n_used, w1, w3, w2, tm=MOE_TM):
    n_rows = row_token.shape[0]
    n_blocks = n_rows // tm
    d = h.shape[1]
    ff = w1.shape[2]
    grid_spec = pltpu.PrefetchScalarGridSpec(
        num_scalar_prefetch=3,
        grid=(n_blocks,),
        in_specs=[
            pl.BlockSpec(memory_space=pl.ANY),
            pl.BlockSpec((tm, 1), lambda i, be, rt, nu: (i, 0)),
            pl.BlockSpec((1, d, ff), lambda i, be, rt, nu: (be[i], 0, 0)),
            pl.BlockSpec((1, d, ff), lambda i, be, rt, nu: (be[i], 0, 0)),
            pl.BlockSpec((1, ff, d), lambda i, be, rt, nu: (be[i], 0, 0)),
        ],
        out_specs=pl.BlockSpec((tm, d), lambda i, be, rt, nu: (i, 0)),
        scratch_shapes=[pltpu.VMEM((2, tm, d), F32), pltpu.SemaphoreType.DMA((2,))],
    )
    return pl.pallas_call(
        _expert_kernel,
        out_shape=jax.ShapeDtypeStruct((n_rows, d), F32),
        grid_spec=grid_spec,
        compiler_params=_params("arbitrary"),
        name="moe_experts",
    )(block_expert, row_token, n_used, h, row_gate.reshape(n_rows, 1), w1, w3, w2)


def _combine_kernel(dest_ref, y_hbm, h_ref, gain_ref, bias_ref, o32_ref, o16_ref, ybuf, sem):
    i = pl.program_id(0)
    n = pl.num_programs(0)
    tm = h_ref.shape[0]
    rows = MOE_TOP_K * tm
    slot = i % 2

    def issue(block, to_slot):
        def body(r, carry):
            _row_copy(y_hbm, dest_ref[block * rows + r], ybuf, to_slot, r, sem).start()
            return carry
        lax.fori_loop(0, rows, body, 0)

    @pl.when(i == 0)
    def _():
        issue(0, 0)

    @pl.when(i + 1 < n)
    def _():
        issue(i + 1, 1 - slot)

    def wait_body(r, carry):
        _row_copy(y_hbm, 0, ybuf, slot, r, sem).wait()
        return carry
    lax.fori_loop(0, rows, wait_body, 0)
    ffn = ybuf[slot, 0:tm, :] + ybuf[slot, tm:rows, :]
    y = _layer_norm_rows(DEEPNORM_ALPHA * h_ref[...] + ffn, gain_ref[...], bias_ref[...])
    o32_ref[...] = y
    o16_ref[...] = y.astype(BF16)


def moe_combine_ln(y_rows, dest, h, gain, bias, tm=COMBINE_TM):
    m, d = h.shape
    tm = min(tm, m)
    grid_spec = pltpu.PrefetchScalarGridSpec(
        num_scalar_prefetch=1,
        grid=(m // tm,),
        in_specs=[
            pl.BlockSpec(memory_space=pl.ANY),
            pl.BlockSpec((tm, d), lambda i, de: (i, 0)),
            pl.BlockSpec((1, d), lambda i, de: (0, 0)),
            pl.BlockSpec((1, d), lambda i, de: (0, 0)),
        ],
        out_specs=(pl.BlockSpec((tm, d), lambda i, de: (i, 0)),
                   pl.BlockSpec((tm, d), lambda i, de: (i, 0))),
        scratch_shapes=[pltpu.VMEM((2, MOE_TOP_K * tm, d), F32), pltpu.SemaphoreType.DMA((2,))],
    )
    return pl.pallas_call(
        _combine_kernel,
        out_shape=(jax.ShapeDtypeStruct((m, d), F32), jax.ShapeDtypeStruct((m, d), BF16)),
        grid_spec=grid_spec,
        compiler_params=_params("arbitrary"),
        name="moe_combine_ln",
    )(dest.reshape(-1), y_rows, h, gain.reshape(1, d), bias.reshape(1, d))


def moe_layer(h32, w_group, b_group, w_expert, b_expert, w1, w3, w2, gain, bias, tm=MOE_TM):
    t = h32.shape[0]
    ids, gates = moe_router(h32, w_group, b_group, w_expert, b_expert)
    eid = ids.reshape(-1)
    n_pairs = eid.shape[0]
    onehot = (eid[:, None] == jnp.arange(MOE_EXPERTS, dtype=jnp.int32)[None, :]).astype(jnp.int32)
    counts = onehot.sum(0)
    rank = jnp.sum((jnp.cumsum(onehot, 0) - 1) * onehot, axis=1)
    padded = (counts + tm - 1) // tm * tm
    seg_end = jnp.cumsum(padded)
    dest = ((seg_end - padded)[eid] + rank).astype(jnp.int32)
    n_blocks = -(-n_pairs // tm) + MOE_EXPERTS
    n_rows = n_blocks * tm
    token_of_pair = (jnp.arange(n_pairs, dtype=jnp.int32) // MOE_TOP_K)
    row_token = jnp.zeros((n_rows,), jnp.int32).at[dest].set(token_of_pair)
    row_gate = jnp.zeros((n_rows,), F32).at[dest].set(gates.reshape(-1))
    block_expert = jnp.minimum(
        jnp.searchsorted(seg_end, jnp.arange(n_blocks, dtype=jnp.int32) * tm, side='right'),
        MOE_EXPERTS - 1).astype(jnp.int32)
    n_used = (seg_end[-1:] // tm).astype(jnp.int32)
    y_rows = moe_experts(h32, row_token, row_gate, block_expert, n_used, w1, w3, w2, tm=tm)
    ctm = min(COMBINE_TM, t)
    dest_tiles = jnp.transpose(dest.reshape(t // ctm, ctm, MOE_TOP_K), (0, 2, 1))
    return moe_combine_ln(y_rows, dest_tiles, h32, gain, bias, tm=ctm)


def kernel(x, gdn_w_in, gdn_conv, gdn_a_log, gdn_dt_bias, gdn_norm, gdn_w_out, kv_w_k, kv_w_v, dil_w_q, dil_w_o,
           rel_bias, ln_gain, ln_bias, moe_w_group, moe_b_group, moe_w_expert, moe_b_expert, moe_w1, moe_w3, moe_w2):
    batch, seq, d = x.shape
    t = batch * seq
    h32 = x.reshape(t, d)
    h16 = h32.astype(BF16)
    k = v = None
    biases = None
    for layer in range(DEPTH):
        if layer < N_A_LAYERS:
            h32, h16 = gated_deltanet_layer(
                h32, h16, gdn_w_in[layer], gdn_conv[layer], gdn_a_log[layer], gdn_dt_bias[layer],
                gdn_norm[layer], gdn_w_out[layer], ln_gain[layer, 0], ln_bias[layer, 0], batch, seq)
        else:
            if layer == N_A_LAYERS:
                kv = matmul(h16, jnp.concatenate([kv_w_k, kv_w_v], axis=1).astype(BF16), out_dtype=BF16)
                k, v = kv[:, :DIL_KV_DIM], kv[:, DIL_KV_DIM:]
                biases = [_band_bias(rel_bias[:, gi * DIL_HEADS:(gi + 1) * DIL_HEADS], dil)
                          for gi, (_, dil) in enumerate(DIL_CONFIGS)]
            j = layer - N_A_LAYERS
            q = matmul(h16, dil_w_q[j].astype(BF16), out_dtype=BF16)
            outs, lses = [], []
            for gi, (_, dil) in enumerate(DIL_CONFIGS):
                o, lse = dilated_group_attention(q, k, v, biases[gi], gi, dil, batch, seq)
                outs.append(o)
                lses.append(lse)
            h32, h16 = dilated_out(outs, lses, dil_w_o[j].astype(BF16), h32, ln_gain[layer, 0], ln_bias[layer, 0])
        h32, h16 = moe_layer(h32, moe_w_group[layer], moe_b_group[layer], moe_w_expert[layer], moe_b_expert[layer],
                             moe_w1[layer].astype(BF16), moe_w3[layer].astype(BF16), moe_w2[layer].astype(BF16),
                             ln_gain[layer, 1], ln_bias[layer, 1])
    return h32.reshape(batch, seq, d)
```

```python
import functools
import math

import jax
import jax.numpy as jnp
from jax import lax
from jax.experimental import pallas as pl
from jax.experimental.pallas import tpu as pltpu

F32 = jnp.float32
BF16 = jnp.bfloat16
HIGHEST = lax.Precision.HIGHEST

DEPTH = 4
N_A_LAYERS = DEPTH // 2
HEAD_DIM = 128
GDN_QK_HEADS = 16
GDN_V_HEADS = 32
GDN_CONV = 4
GDN_CHUNK = 64
GDN_QK_DIM = GDN_QK_HEADS * HEAD_DIM
GDN_V_DIM = GDN_V_HEADS * HEAD_DIM
GDN_CONV_DIM = 2 * GDN_QK_DIM + GDN_V_DIM
DIL_CONFIGS = ((128, 1), (512, 4), (2048, 16))
N_DIL = len(DIL_CONFIGS)
DIL_HEADS = 8
DIL_BLOCK = 128
DIL_KV_DIM = DIL_HEADS * HEAD_DIM
REL_BUCKETS = 32
REL_MAX_DIST = 2048
MOE_GROUPS = 4
MOE_EXPERTS_PER_GROUP = 8
MOE_EXPERTS = MOE_GROUPS * MOE_EXPERTS_PER_GROUP
MOE_TOP_K = 2
DEEPNORM_ALPHA = (2 * DEPTH) ** 0.25
LN_EPS = 1e-5
RMS_EPS = 1e-6
L2_EPS = 1e-6

LANES = 128
SUBLANES = 8
VMEM_LIMIT_BYTES = 56 * 1024 * 1024

MM_TM = 512
MM_TN = 1024
LN_TM = 256
GATE_TM = 512
GDN_TB = 512
MOE_TM = 256
ROUTER_TM = 512
COMBINE_TM = 256


def _params(*semantics):
    return pltpu.CompilerParams(dimension_semantics=semantics, vmem_limit_bytes=VMEM_LIMIT_BYTES)


def _layer_norm_rows(y, gain, bias):
    mu = jnp.mean(y, axis=-1, keepdims=True)
    yc = y - mu
    var = jnp.mean(yc * yc, axis=-1, keepdims=True)
    return yc * lax.rsqrt(var + LN_EPS) * gain + bias


def _matmul_kernel(a_ref, b_ref, o_ref):
    o_ref[...] = jnp.dot(a_ref[...].astype(BF16), b_ref[...],
                         preferred_element_type=F32).astype(o_ref.dtype)


def matmul(a, b, out_dtype=F32, tm=MM_TM, tn=MM_TN):
    m, k = a.shape
    n = b.shape[1]
    tm, tn = min(tm, m), min(tn, n)
    assert m % tm == 0 and n % tn == 0
    return pl.pallas_call(
        _matmul_kernel,
        out_shape=jax.ShapeDtypeStruct((m, n), out_dtype),
        grid=(n // tn, m // tm),
        in_specs=[pl.BlockSpec((tm, k), lambda j, i: (i, 0)),
                  pl.BlockSpec((k, tn), lambda j, i: (0, j))],
        out_specs=pl.BlockSpec((tm, tn), lambda j, i: (i, j)),
        compiler_params=_params("parallel", "parallel"),
        name="matmul",
    )(a, b)


def _proj_ln_kernel(a_ref, w_ref, h_ref, gain_ref, bias_ref, o32_ref, o16_ref):
    mix = jnp.dot(a_ref[...].astype(BF16), w_ref[...], preferred_element_type=F32)
    y = _layer_norm_rows(DEEPNORM_ALPHA * h_ref[...] + mix, gain_ref[...], bias_ref[...])
    o32_ref[...] = y
    o16_ref[...] = y.astype(BF16)


def proj_residual_ln(a, w, h, gain, bias, tm=LN_TM):
    m, k = a.shape
    d = w.shape[1]
    tm = min(tm, m)
    assert m % tm == 0
    return pl.pallas_call(
        _proj_ln_kernel,
        out_shape=(jax.ShapeDtypeStruct((m, d), F32), jax.ShapeDtypeStruct((m, d), BF16)),
        grid=(m // tm,),
        in_specs=[pl.BlockSpec((tm, k), lambda i: (i, 0)),
                  pl.BlockSpec((k, d), lambda i: (0, 0)),
                  pl.BlockSpec((tm, d), lambda i: (i, 0)),
                  pl.BlockSpec((1, d), lambda i: (0, 0)),
                  pl.BlockSpec((1, d), lambda i: (0, 0))],
        out_specs=(pl.BlockSpec((tm, d), lambda i: (i, 0)),
                   pl.BlockSpec((tm, d), lambda i: (i, 0))),
        compiler_params=_params("parallel"),
        name="proj_residual_ln",
    )(a, w, h, gain.reshape(1, d), bias.reshape(1, d))


def _gdn_gate_kernel(h_ref, w_ref, alog_ref, dt_ref, o_ref):
    tm = h_ref.shape[0]
    ba = jnp.dot(h_ref[...], w_ref[...], precision=HIGHEST, preferred_element_type=F32)
    xa = ba + dt_ref[...]
    softplus = jnp.maximum(xa, 0.0) + jnp.log(1.0 + jnp.exp(-jnp.abs(xa)))
    g = -jnp.exp(alog_ref[...]) * softplus
    row = lax.broadcasted_iota(jnp.int32, (tm, tm), 0)
    col = lax.broadcasted_iota(jnp.int32, (tm, tm), 1)
    tri = ((row // GDN_CHUNK == col // GDN_CHUNK) & (col <= row)).astype(F32)
    gc = jnp.dot(tri, g, precision=HIGHEST, preferred_element_type=F32)
    lane = lax.broadcasted_iota(jnp.int32, ba.shape, 1)
    o_ref[...] = jnp.where(lane < GDN_V_HEADS, jax.nn.sigmoid(ba), gc)


def gdn_gates(h, w_ba, a_log, dt_bias, tm=GATE_TM):
    m, d = h.shape
    tm = min(tm, m)
    pad = LANES - 2 * GDN_V_HEADS
    w = jnp.pad(w_ba, ((0, 0), (0, pad)))
    zeros = jnp.zeros((GDN_V_HEADS,), F32)
    alog = jnp.concatenate([zeros, a_log, jnp.zeros((pad,), F32)]).reshape(1, LANES)
    dt = jnp.concatenate([zeros, dt_bias, jnp.zeros((pad,), F32)]).reshape(1, LANES)
    return pl.pallas_call(
        _gdn_gate_kernel,
        out_shape=jax.ShapeDtypeStruct((m, LANES), F32),
        grid=(m // tm,),
        in_specs=[pl.BlockSpec((tm, d), lambda i: (i, 0)),
                  pl.BlockSpec((d, LANES), lambda i: (0, 0)),
                  pl.BlockSpec((1, LANES), lambda i: (0, 0)),
                  pl.BlockSpec((1, LANES), lambda i: (0, 0))],
        out_specs=pl.BlockSpec((tm, LANES), lambda i: (i, 0)),
        compiler_params=_params("parallel"),
        name="gdn_gates",
    )(h, w, alog, dt)


def _dot_bf16(a, b):
    return jnp.dot(a.astype(BF16), b.astype(BF16), preferred_element_type=F32)


def _dot_nt(a, b):
    return lax.dot_general(a.astype(BF16), b.astype(BF16), (((1,), (1,)), ((), ())),
                           preferred_element_type=F32)


def _dot_tn(a, b):
    return lax.dot_general(a.astype(BF16), b.astype(BF16), (((0,), (0,)), ((), ())),
                           preferred_element_type=F32)


def _block_diag2(x0, x1):
    z = jnp.zeros_like(x0)
    return jnp.concatenate([jnp.concatenate([x0, z], axis=1), jnp.concatenate([z, x1], axis=1)], axis=0)


def _gdn_kernel(q_ref, k_ref, v_ref, z_ref, wq_ref, wk_ref, wv_ref, gcol_ref, grow_ref, nw_ref,
                o_ref, qx, kx, vx, qn, kn, vn, state):
    blk = pl.program_id(2)
    tb = q_ref.shape[0]
    n_chunks = tb // GDN_CHUNK
    c = GDN_CHUNK
    hd = HEAD_DIM
    tail = SUBLANES

    @pl.when(blk == 0)
    def _():
        qx[0:tail, :] = jnp.zeros((tail, qx.shape[1]), F32)
        kx[0:tail, :] = jnp.zeros((tail, kx.shape[1]), F32)
        vx[0:tail, :] = jnp.zeros((tail, vx.shape[1]), F32)
        state[...] = jnp.zeros(state.shape, F32)

    def conv_silu(x_ref, ext, w_ref):
        ext[tail:tail + tb, :] = x_ref[...].astype(F32)
        y = None
        for j in range(GDN_CONV):
            term = w_ref[j:j + 1, :] * ext[pl.ds(tail - (GDN_CONV - 1) + j, tb), :]
            y = term if y is None else y + term
        ext[0:tail, :] = ext[tb:tb + tail, :]
        return y * jax.nn.sigmoid(y)

    def l2n(x):
        return x * lax.rsqrt(jnp.sum(x * x, axis=-1, keepdims=True) + L2_EPS)

    qn[...] = l2n(conv_silu(q_ref, qx, wq_ref)) * (HEAD_DIM ** -0.5)
    kn[...] = l2n(conv_silu(k_ref, kx, wk_ref))
    vn[...] = conv_silu(v_ref, vx, wv_ref)

    row = lax.broadcasted_iota(jnp.int32, (c, 2 * c), 0)
    lane = lax.broadcasted_iota(jnp.int32, (c, 2 * c), 1)
    col = lane & (c - 1)
    head0 = lane < c
    causal = col <= row
    strict = col < row
    in_block = (row // 16) == (col // 16)
    eye = (row == col).astype(F32)
    norm_w = nw_ref[...]

    def mm(x, y):
        yb = jnp.concatenate([jnp.where(head0, y, 0.0), jnp.where(head0, 0.0, y)], axis=0)
        return _dot_bf16(x, yb)

    chunks = range(n_chunks)
    each = lambda f, *lists: [f(*xs) for xs in zip(*lists)]
    qcs = [qn[ci * c:(ci + 1) * c, :] for ci in chunks]
    kcs = [kn[ci * c:(ci + 1) * c, :] for ci in chunks]
    gcols = [gcol_ref[0, 0, ci] for ci in chunks]
    grps = [grow_ref[0, 0, ci] for ci in chunks]
    gcs = [[g[:, hh:hh + 1] for hh in range(2)] for g in gcols]
    betas = [[g[:, 2 + hh:3 + hh] for hh in range(2)] for g in gcols]
    egs = [[jnp.exp(g[hh]) for hh in range(2)] for g in gcs]
    glast = [[g[hh][c - 1:c, :] for hh in range(2)] for g in gcs]
    both = each(lambda q, k: _dot_nt(jnp.concatenate([q, k], axis=0), jnp.concatenate([k, k], axis=0)), qcs, kcs)
    decayp = each(lambda g, gr: jnp.exp(jnp.where(causal, jnp.where(head0, g[0], g[1]) - gr, -jnp.inf)), gcs, grps)
    ap = each(lambda b, bo, dc: jnp.where(strict, jnp.where(head0, b[0], b[1]) * bo[c:] * dc, 0.0), betas, both, decayp)
    qkd = each(lambda bo, dc: bo[:c] * dc, both, decayp)
    d = each(lambda a: jnp.where(in_block, a, 0.0), ap)
    e = each(lambda a, dd: a - dd, ap, d)
    n1 = each(lambda dd: -dd, d)
    p = each(lambda n: eye + n, n1)
    n2 = each(mm, n1, n1)
    p = each(lambda pp, n: pp + mm(pp, n), p, n2)
    n4 = each(mm, n2, n2)
    p = each(lambda pp, n: pp + mm(pp, n), p, n4)
    n8 = each(mm, n4, n4)
    dinv = each(lambda pp, n: pp + mm(pp, n), p, n8)
    nb = each(mm, dinv, e)
    nb2 = each(mm, nb, nb)
    qq = each(lambda n: eye - n, nb)
    qq = each(lambda x, n: x + mm(x, n), qq, nb2)
    tinvp = each(mm, qq, dinv)
    rhs = [[jnp.concatenate([vn[ci * c:(ci + 1) * c, hh * hd:(hh + 1) * hd] * betas[ci][hh],
                             kcs[ci] * (betas[ci][hh] * egs[ci][hh])], axis=1) for hh in range(2)] for ci in chunks]
    uw = each(lambda t, r: _dot_bf16(t, _block_diag2(r[0], r[1])), tinvp, rhs)
    wu = [[jnp.concatenate([x[:, (2 * hh + 1) * hd:(2 * hh + 2) * hd], x[:, 2 * hh * hd:(2 * hh + 1) * hd]], axis=1)
           for hh in range(2)] for x in uw]
    qo = each(lambda x, w2: _dot_bf16(x, _block_diag2(w2[0], w2[1])), qkd, wu)
    pre = []
    for ci in chunks:
        for hh in range(2):
            rb = _dot_tn(kcs[ci] * jnp.exp(glast[ci][hh] - gcs[ci][hh]), wu[ci][hh])
            q_eff = qcs[ci] * egs[ci][hh] - qo[ci][:, 2 * hh * hd:(2 * hh + 1) * hd]
            rq = jnp.concatenate([rb[:, :hd], q_eff], axis=0).astype(BF16)
            pre.append((rq, rb[:, hd:], qo[ci][:, (2 * hh + 1) * hd:(2 * hh + 2) * hd], jnp.exp(glast[ci][hh])))

    states = [state[0], state[1]]
    for ci in range(n_chunks):
        r0 = ci * c
        for hh in range(2):
            rq, b_c, o_c, cd = pre[2 * ci + hh]
            s = states[hh]
            res = jnp.dot(rq, s.astype(BF16), preferred_element_type=F32)
            states[hh] = s * cd - res[:hd] + b_c
            out = res[hd:] + o_c
            out = out * lax.rsqrt(jnp.mean(out * out, axis=-1, keepdims=True) + RMS_EPS) * norm_w
            zc = z_ref[r0:r0 + c, hh * hd:(hh + 1) * hd].astype(F32)
            out = out * (zc * jax.nn.sigmoid(zc))
            o_ref[r0:r0 + c, hh * hd:(hh + 1) * hd] = out.astype(o_ref.dtype)
    state[0] = states[0]
    state[1] = states[1]


def gdn_delta(proj, conv_w, gcol, grow, norm_w, batch, seq, tb=GDN_TB):
    t = proj.shape[0]
    tb = min(tb, seq)
    nblk = seq // tb
    ncb = tb // GDN_CHUNK
    hd = HEAD_DIM
    kq = GDN_QK_HEADS
    kv2 = GDN_CONV_DIM // (2 * hd)
    v0 = (2 * GDN_QK_DIM) // (2 * hd)
    z0 = GDN_CONV_DIM // (2 * hd)
    del kv2
    return pl.pallas_call(
        _gdn_kernel,
        out_shape=jax.ShapeDtypeStruct((t, GDN_V_DIM), BF16),
        grid=(batch, GDN_QK_HEADS, nblk),
        in_specs=[
            pl.BlockSpec((tb, hd), lambda b, h, i: (b * nblk + i, h)),
            pl.BlockSpec((tb, hd), lambda b, h, i: (b * nblk + i, kq + h)),
            pl.BlockSpec((tb, 2 * hd), lambda b, h, i: (b * nblk + i, v0 + h)),
            pl.BlockSpec((tb, 2 * hd), lambda b, h, i: (b * nblk + i, z0 + h)),
            pl.BlockSpec((GDN_CONV, hd), lambda b, h, i: (0, h)),
            pl.BlockSpec((GDN_CONV, hd), lambda b, h, i: (0, kq + h)),
            pl.BlockSpec((GDN_CONV, 2 * hd), lambda b, h, i: (0, v0 + h)),
            pl.BlockSpec((1, 1, ncb, GDN_CHUNK, 4), lambda b, h, i: (b, h, i, 0, 0)),
            pl.BlockSpec((1, 1, ncb, 1, 2 * GDN_CHUNK), lambda b, h, i: (b, h, i, 0, 0)),
            pl.BlockSpec((1, hd), lambda b, h, i: (0, 0)),
        ],
        out_specs=pl.BlockSpec((tb, 2 * hd), lambda b, h, i: (b * nblk + i, h)),
        scratch_shapes=[
            pltpu.VMEM((tb + SUBLANES, hd), F32),
            pltpu.VMEM((tb + SUBLANES, hd), F32),
            pltpu.VMEM((tb + SUBLANES, 2 * hd), F32),
            pltpu.VMEM((tb, hd), F32),
            pltpu.VMEM((tb, hd), F32),
            pltpu.VMEM((tb, 2 * hd), F32),
            pltpu.VMEM((2, hd, hd), F32),
        ],
        compiler_params=_params("parallel", "parallel", "arbitrary"),
        name="gdn_delta",
    )(proj, proj, proj, proj, conv_w, conv_w, conv_w, gcol, grow, norm_w.reshape(1, hd))


def gated_deltanet_layer(h32, h16, w_in, conv_w, a_log, dt_bias, norm_w, w_out, gain, bias, batch, seq):
    t = h32.shape[0]
    main = GDN_CONV_DIM + GDN_V_DIM
    proj = matmul(h16, w_in[:, :main].astype(BF16))
    gates = gdn_gates(h32, w_in[:, main:], a_log, dt_bias)
    nc = seq // GDN_CHUNK
    beta = gates[:, :GDN_V_HEADS].reshape(batch, nc, GDN_CHUNK, GDN_QK_HEADS, 2)
    gc = gates[:, GDN_V_HEADS:2 * GDN_V_HEADS].reshape(batch, nc, GDN_CHUNK, GDN_QK_HEADS, 2)
    gcol = jnp.transpose(jnp.concatenate([gc, beta], axis=-1), (0, 3, 1, 2, 4))
    grow = jnp.transpose(gc, (0, 3, 1, 4, 2)).reshape(batch, GDN_QK_HEADS, nc, 1, 2 * GDN_CHUNK)
    o = gdn_delta(proj, conv_w, gcol, grow, norm_w, batch, seq)
    del t
    return proj_residual_ln(o, w_out.astype(BF16), h32, gain, bias)


def _dil_attn_kernel(q_ref, kp_ref, ko_ref, vp_ref, vo_ref, bias_ref, o_ref, lse_ref):
    lb = pl.program_id(2)
    blk = DIL_BLOCK
    qi = lax.broadcasted_iota(jnp.int32, (blk, blk), 0)
    kj = lax.broadcasted_iota(jnp.int32, (blk, blk), 1)
    mask_prev = (qi <= kj) & (lb > 0)
    mask_own = kj <= qi
    lane = lax.broadcasted_iota(jnp.int32, (blk, LANES), 1)
    lse_all = jnp.zeros((blk, LANES), F32)
    scale = HEAD_DIM ** -0.5
    for hd in range(DIL_HEADS):
        sl = slice(hd * HEAD_DIM, (hd + 1) * HEAD_DIM)
        q = q_ref[0, :, sl]
        bias = bias_ref[hd]
        s_prev = _dot_nt(q, kp_ref[0, :, sl]) * scale + bias[:, :blk]
        s_own = _dot_nt(q, ko_ref[0, :, sl]) * scale + bias[:, blk:]
        s_prev = jnp.where(mask_prev, s_prev, -jnp.inf)
        s_own = jnp.where(mask_own, s_own, -jnp.inf)
        m = jnp.maximum(jnp.max(s_prev, axis=-1, keepdims=True), jnp.max(s_own, axis=-1, keepdims=True))
        p_prev = jnp.exp(s_prev - m)
        p_own = jnp.exp(s_own - m)
        den = jnp.sum(p_prev, axis=-1, keepdims=True) + jnp.sum(p_own, axis=-1, keepdims=True)
        o = _dot_bf16(p_prev, vp_ref[0, :, sl]) + _dot_bf16(p_own, vo_ref[0, :, sl])
        o_ref[0, :, sl] = o / den
        lse_all = jnp.where(lane == hd, m + jnp.log(den), lse_all)
    lse_ref[0] = lse_all


def dilated_group_attention(q, k, v, bias, gi, dil, batch, seq):
    length = seq // dil
    nb = length // DIL_BLOCK
    kvd = DIL_KV_DIM
    qv = q.reshape(batch, length, dil * N_DIL * kvd)
    kview = k.reshape(batch, length, dil * kvd)
    vview = v.reshape(batch, length, dil * kvd)
    blk = DIL_BLOCK
    own = lambda b, r, i: (b, i, r)
    prev = lambda b, r, i: (b, jnp.maximum(i - 1, 0), r)
    o, lse = pl.pallas_call(
        _dil_attn_kernel,
        out_shape=(jax.ShapeDtypeStruct((batch, length, dil * kvd), F32),
                   jax.ShapeDtypeStruct((batch, length, dil * LANES), F32)),
        grid=(batch, dil, nb),
        in_specs=[
            pl.BlockSpec((1, blk, kvd), lambda b, r, i: (b, i, r * N_DIL + gi)),
            pl.BlockSpec((1, blk, kvd), prev),
            pl.BlockSpec((1, blk, kvd), own),
            pl.BlockSpec((1, blk, kvd), prev),
            pl.BlockSpec((1, blk, kvd), own),
            pl.BlockSpec((DIL_HEADS, blk, 2 * blk), lambda b, r, i: (0, 0, 0)),
        ],
        out_specs=(pl.BlockSpec((1, blk, kvd), own), pl.BlockSpec((1, blk, LANES), own)),
        compiler_params=_params("parallel", "parallel", "parallel"),
        name=f"dilated_attention_g{gi}",
    )(qv, kview, kview, vview, vview, bias)
    return o.reshape(batch * seq, kvd), lse.reshape(batch * seq, LANES)


def _dil_out_kernel(o0_ref, o1_ref, o2_ref, l0_ref, l1_ref, l2_ref, w_ref, h_ref, gain_ref, bias_ref,
                    o32_ref, o16_ref, mix_ref):
    l0, l1, l2 = l0_ref[...], l1_ref[...], l2_ref[...]
    m = jnp.maximum(jnp.maximum(l0, l1), l2)
    e0, e1, e2 = jnp.exp(l0 - m), jnp.exp(l1 - m), jnp.exp(l2 - m)
    inv = 1.0 / (e0 + e1 + e2)
    w0, w1, w2 = e0 * inv, e1 * inv, e2 * inv
    for hd in range(DIL_HEADS):
        sl = slice(hd * HEAD_DIM, (hd + 1) * HEAD_DIM)
        mix_ref[:, sl] = (w0[:, hd:hd + 1] * o0_ref[:, sl] + w1[:, hd:hd + 1] * o1_ref[:, sl]
                          + w2[:, hd:hd + 1] * o2_ref[:, sl])
    mix = jnp.dot(mix_ref[...].astype(BF16), w_ref[...], preferred_element_type=F32)
    y = _layer_norm_rows(DEEPNORM_ALPHA * h_ref[...] + mix, gain_ref[...], bias_ref[...])
    o32_ref[...] = y
    o16_ref[...] = y.astype(BF16)


def dilated_out(outs, lses, w_o, h, gain, bias, tm=LN_TM):
    m, kvd = outs[0].shape
    d = w_o.shape[1]
    tm = min(tm, m)
    row = lambda i: (i, 0)
    fixed = lambda i: (0, 0)
    return pl.pallas_call(
        _dil_out_kernel,
        out_shape=(jax.ShapeDtypeStruct((m, d), F32), jax.ShapeDtypeStruct((m, d), BF16)),
        grid=(m // tm,),
        in_specs=[pl.BlockSpec((tm, kvd), row)] * 3 + [pl.BlockSpec((tm, LANES), row)] * 3 + [
            pl.BlockSpec((kvd, d), fixed), pl.BlockSpec((tm, d), row),
            pl.BlockSpec((1, d), fixed), pl.BlockSpec((1, d), fixed)],
        out_specs=(pl.BlockSpec((tm, d), row), pl.BlockSpec((tm, d), row)),
        scratch_shapes=[pltpu.VMEM((tm, kvd), F32)],
        compiler_params=_params("parallel"),
        name="dilated_out_ln",
    )(*outs, *lses, w_o, h, gain.reshape(1, d), bias.reshape(1, d))


def _t5_bucket(dist):
    exact = REL_BUCKETS // 2
    distf = jnp.maximum(dist, exact).astype(F32)
    large = exact + (jnp.log(distf / exact) / math.log(REL_MAX_DIST / exact)
                     * (REL_BUCKETS - exact)).astype(jnp.int32)
    return jnp.where(dist < exact, dist, jnp.minimum(large, REL_BUCKETS - 1))


def _band_bias(rel_bias_g, dil):
    qi = jnp.arange(DIL_BLOCK)[:, None]
    kj = jnp.arange(2 * DIL_BLOCK)[None, :]
    steps = DIL_BLOCK + qi - kj
    bias = rel_bias_g[_t5_bucket(jnp.maximum(steps, 0) * dil)]
    return jnp.transpose(bias, (2, 0, 1)).astype(F32)


def _router_kernel(h_ref, w_ref, b_ref, o_ref):
    logits = jnp.dot(h_ref[...], w_ref[...], precision=HIGHEST, preferred_element_type=F32) + b_ref[...]
    lane = lax.broadcasted_iota(jnp.int32, logits.shape, 1)
    neg = -jnp.inf
    big = jnp.int32(LANES)
    is_group = lane < MOE_GROUPS
    gl = jnp.where(is_group, logits, neg)
    gmax = jnp.max(gl, axis=-1, keepdims=True)
    gidx = jnp.min(jnp.where(gl == gmax, lane, big), axis=-1, keepdims=True)
    gden = jnp.sum(jnp.exp(gl - gmax), axis=-1, keepdims=True)
    group_gate = 1.0 / gden
    lo = MOE_GROUPS + gidx * MOE_EXPERTS_PER_GROUP
    in_group = (lane >= lo) & (lane < lo + MOE_EXPERTS_PER_GROUP)
    ll = jnp.where(in_group, logits, neg)
    m1 = jnp.max(ll, axis=-1, keepdims=True)
    i1 = jnp.min(jnp.where(ll == m1, lane, big), axis=-1, keepdims=True)
    rest = jnp.where(lane == i1, neg, ll)
    m2 = jnp.max(rest, axis=-1, keepdims=True)
    i2 = jnp.min(jnp.where(rest == m2, lane, big), axis=-1, keepdims=True)
    den = jnp.sum(jnp.exp(ll - m1), axis=-1, keepdims=True)
    p1 = 1.0 / den
    p2 = jnp.exp(m2 - m1) / den
    g1 = group_gate * p1 / (p1 + p2)
    g2 = group_gate * p2 / (p1 + p2)
    id1 = (i1 - MOE_GROUPS).astype(F32)
    id2 = (i2 - MOE_GROUPS).astype(F32)
    o_ref[...] = jnp.where(lane == 0, id1, jnp.where(lane == 1, id2, jnp.where(lane == 2, g1, g2)))


def moe_router(h, w_group, b_group, w_expert, b_expert, tm=ROUTER_TM):
    m, d = h.shape
    tm = min(tm, m)
    pad = LANES - MOE_GROUPS - MOE_EXPERTS
    w = jnp.pad(jnp.concatenate([w_group, w_expert], axis=1), ((0, 0), (0, pad)))
    b = jnp.pad(jnp.concatenate([b_group, b_expert]), (0, pad)).reshape(1, LANES)
    out = pl.pallas_call(
        _router_kernel,
        out_shape=jax.ShapeDtypeStruct((m, LANES), F32),
        grid=(m // tm,),
        in_specs=[pl.BlockSpec((tm, d), lambda i: (i, 0)),
                  pl.BlockSpec((d, LANES), lambda i: (0, 0)),
                  pl.BlockSpec((1, LANES), lambda i: (0, 0))],
        out_specs=pl.BlockSpec((tm, LANES), lambda i: (i, 0)),
        compiler_params=_params("parallel"),
        name="moe_router",
    )(h, w, b)
    return out[:, :MOE_TOP_K].astype(jnp.int32), out[:, MOE_TOP_K:2 * MOE_TOP_K]


GATHER_UNROLL = 8


def _gather_rows_start(src_hbm, idx_ref, idx_base, buf, slot, sem):
    def body(r, carry):
        pltpu.make_async_copy(src_hbm.at[pl.ds(idx_ref[idx_base + r], 1)], buf.at[slot, pl.ds(r, 1)],
                              sem.at[slot]).start()
        return carry
    lax.fori_loop(0, buf.shape[1], body, 0, unroll=GATHER_UNROLL)


def _gather_rows_wait(src_hbm, buf, slot, sem):
    pltpu.make_async_copy(src_hbm.at[pl.ds(0, buf.shape[1])], buf.at[slot], sem.at[slot]).wait()


def _expert_kernel(bexp_ref, rtok_ref, nused_ref, h_hbm, gate_ref, w1_ref, w3_ref, w2_ref, y_ref, xbuf, sem):
    del bexp_ref
    i = pl.program_id(0)
    tm = xbuf.shape[1]
    n_used = nused_ref[0]
    slot = i % 2

    def issue(block, to_slot):
        _gather_rows_start(h_hbm, rtok_ref, block * tm, xbuf, to_slot, sem)

    @pl.when(i == 0)
    def _():
        issue(0, 0)

    @pl.when(i + 1 < n_used)
    def _():
        issue(i + 1, 1 - slot)

    @pl.when(i < n_used)
    def _():
        _gather_rows_wait(h_hbm, xbuf, slot, sem)
        x = xbuf[slot].astype(BF16)
        h1 = jnp.dot(x, w1_ref[0], preferred_element_type=F32)
        h3 = jnp.dot(x, w3_ref[0], preferred_element_type=F32)
        hmid = (h1 * jax.nn.sigmoid(h1) * h3).astype(BF16)
        y = jnp.dot(hmid, w2_ref[0], preferred_element_type=F32)
        y_ref[...] = y * gate_ref[...]

    @pl.when(i >= n_used)
    def _():
        y_ref[...] = jnp.zeros(y_ref.shape, y_ref.dtype)


def moe_experts(h, row_token, row_gate, block_expert, n_used, w1, w3, w2, tm=MOE_TM):
    n_rows = row_token.shape[0]
    n_blocks = n_rows // tm
    d = h.shape[1]
    ff = w1.shape[2]
    grid_spec = pltpu.PrefetchScalarGridSpec(
        num_scalar_prefetch=3,
        grid=(n_blocks,),
        in_specs=[
            pl.BlockSpec(memory_space=pl.ANY),
            pl.BlockSpec((tm, 1), lambda i, be, rt, nu: (i, 0)),
            pl.BlockSpec((1, d, ff), lambda i, be, rt, nu: (be[i], 0, 0)),
            pl.BlockSpec((1, d, ff), lambda i, be, rt, nu: (be[i], 0, 0)),
            pl.BlockSpec((1, ff, d), lambda i, be, rt, nu: (be[i], 0, 0)),
        ],
        out_specs=pl.BlockSpec((tm, d), lambda i, be, rt, nu: (i, 0)),
        scratch_shapes=[pltpu.VMEM((2, tm, d), F32), pltpu.SemaphoreType.DMA((2,))],
    )
    return pl.pallas_call(
        _expert_kernel,
        out_shape=jax.ShapeDtypeStruct((n_rows, d), F32),
        grid_spec=grid_spec,
        compiler_params=_params("arbitrary"),
        name="moe_experts",
    )(block_expert, row_token, n_used, h, row_gate.reshape(n_rows, 1), w1, w3, w2)


def _combine_kernel(dest_ref, y_hbm, h_ref, gain_ref, bias_ref, o32_ref, o16_ref, ybuf, sem):
    i = pl.program_id(0)
    n = pl.num_programs(0)
    tm = h_ref.shape[0]
    rows = MOE_TOP_K * tm
    slot = i % 2

    def issue(block, to_slot):
        _gather_rows_start(y_hbm, dest_ref, block * rows, ybuf, to_slot, sem)

    @pl.when(i == 0)
    def _():
        issue(0, 0)

    @pl.when(i + 1 < n)
    def _():
        issue(i + 1, 1 - slot)

    _gather_rows_wait(y_hbm, ybuf, slot, sem)
    ffn = ybuf[slot, 0:tm, :] + ybuf[slot, tm:rows, :]
    y = _layer_norm_rows(DEEPNORM_ALPHA * h_ref[...] + ffn, gain_ref[...], bias_ref[...])
    o32_ref[...] = y
    o16_ref[...] = y.astype(BF16)


def moe_combine_ln(y_rows, dest, h, gain, bias, tm=COMBINE_TM):
    m, d = h.shape
    tm = min(tm, m)
    grid_spec = pltpu.PrefetchScalarGridSpec(
        num_scalar_prefetch=1,
        grid=(m // tm,),
        in_specs=[
            pl.BlockSpec(memory_space=pl.ANY),
            pl.BlockSpec((tm, d), lambda i, de: (i, 0)),
            pl.BlockSpec((1, d), lambda i, de: (0, 0)),
            pl.BlockSpec((1, d), lambda i, de: (0, 0)),
        ],
        out_specs=(pl.BlockSpec((tm, d), lambda i, de: (i, 0)),
                   pl.BlockSpec((tm, d), lambda i, de: (i, 0))),
        scratch_shapes=[pltpu.VMEM((2, MOE_TOP_K * tm, d), F32), pltpu.SemaphoreType.DMA((2,))],
    )
    return pl.pallas_call(
        _combine_kernel,
        out_shape=(jax.ShapeDtypeStruct((m, d), F32), jax.ShapeDtypeStruct((m, d), BF16)),
        grid_spec=grid_spec,
        compiler_params=_params("arbitrary"),
        name="moe_combine_ln",
    )(dest.reshape(-1), y_rows, h, gain.reshape(1, d), bias.reshape(1, d))


def moe_layer(h32, w_group, b_group, w_expert, b_expert, w1, w3, w2, gain, bias, tm=MOE_TM):
    t = h32.shape[0]
    ids, gates = moe_router(h32, w_group, b_group, w_expert, b_expert)
    eid = ids.reshape(-1)
    n_pairs = eid.shape[0]
    onehot = (eid[:, None] == jnp.arange(MOE_EXPERTS, dtype=jnp.int32)[None, :]).astype(jnp.int32)
    counts = onehot.sum(0)
    rank = jnp.sum((jnp.cumsum(onehot, 0) - 1) * onehot, axis=1)
    padded = (counts + tm - 1) // tm * tm
    seg_end = jnp.cumsum(padded)
    dest = ((seg_end - padded)[eid] + rank).astype(jnp.int32)
    n_blocks = -(-n_pairs // tm) + MOE_EXPERTS
    n_rows = n_blocks * tm
    token_of_pair = (jnp.arange(n_pairs, dtype=jnp.int32) // MOE_TOP_K)
    row_token = jnp.zeros((n_rows,), jnp.int32).at[dest].set(token_of_pair)
    row_gate = jnp.zeros((n_rows,), F32).at[dest].set(gates.reshape(-1))
    block_expert = jnp.minimum(
        jnp.searchsorted(seg_end, jnp.arange(n_blocks, dtype=jnp.int32) * tm, side='right'),
        MOE_EXPERTS - 1).astype(jnp.int32)
    n_used = (seg_end[-1:] // tm).astype(jnp.int32)
    y_rows = moe_experts(h32, row_token, row_gate, block_expert, n_used, w1, w3, w2, tm=tm)
    ctm = min(COMBINE_TM, t)
    dest_tiles = jnp.transpose(dest.reshape(t // ctm, ctm, MOE_TOP_K), (0, 2, 1))
    return moe_combine_ln(y_rows, dest_tiles, h32, gain, bias, tm=ctm)


def kernel(x, gdn_w_in, gdn_conv, gdn_a_log, gdn_dt_bias, gdn_norm, gdn_w_out, kv_w_k, kv_w_v, dil_w_q, dil_w_o,
           rel_bias, ln_gain, ln_bias, moe_w_group, moe_b_group, moe_w_expert, moe_b_expert, moe_w1, moe_w3, moe_w2):
    batch, seq, d = x.shape
    t = batch * seq
    h32 = x.reshape(t, d)
    h16 = h32.astype(BF16)
    k = v = None
    biases = None
    for layer in range(DEPTH):
        if layer < N_A_LAYERS:
            h32, h16 = gated_deltanet_layer(
                h32, h16, gdn_w_in[layer], gdn_conv[layer], gdn_a_log[layer], gdn_dt_bias[layer],
                gdn_norm[layer], gdn_w_out[layer], ln_gain[layer, 0], ln_bias[layer, 0], batch, seq)
        else:
            if layer == N_A_LAYERS:
                kv = matmul(h16, jnp.concatenate([kv_w_k, kv_w_v], axis=1).astype(BF16), out_dtype=BF16)
                k, v = kv[:, :DIL_KV_DIM], kv[:, DIL_KV_DIM:]
                biases = [_band_bias(rel_bias[:, gi * DIL_HEADS:(gi + 1) * DIL_HEADS], dil)
                          for gi, (_, dil) in enumerate(DIL_CONFIGS)]
            j = layer - N_A_LAYERS
            q = matmul(h16, dil_w_q[j].astype(BF16), out_dtype=BF16)
            outs, lses = [], []
            for gi, (_, dil) in enumerate(DIL_CONFIGS):
                o, lse = dilated_group_attention(q, k, v, biases[gi], gi, dil, batch, seq)
                outs.append(o)
                lses.append(lse)
            h32, h16 = dilated_out(outs, lses, dil_w_o[j].astype(BF16), h32, ln_gain[layer, 0], ln_bias[layer, 0])
        h32, h16 = moe_layer(h32, moe_w_group[layer], moe_b_group[layer], moe_w_expert[layer], moe_b_expert[layer],
                             moe_w1[layer].astype(BF16), moe_w3[layer].astype(BF16), moe_w2[layer].astype(BF16),
                             ln_gain[layer, 1], ln_bias[layer, 1])
    return h32.reshape(batch, seq, d)
```

```python
import functools
import math

import jax
import jax.numpy as jnp
from jax import lax
from jax.experimental import pallas as pl
from jax.experimental.pallas import tpu as pltpu

F32 = jnp.float32
BF16 = jnp.bfloat16
HIGHEST = lax.Precision.HIGHEST

DEPTH = 4
N_A_LAYERS = DEPTH // 2
HEAD_DIM = 128
GDN_QK_HEADS = 16
GDN_V_HEADS = 32
GDN_CONV = 4
GDN_CHUNK = 64
GDN_QK_DIM = GDN_QK_HEADS * HEAD_DIM
GDN_V_DIM = GDN_V_HEADS * HEAD_DIM
GDN_CONV_DIM = 2 * GDN_QK_DIM + GDN_V_DIM
DIL_CONFIGS = ((128, 1), (512, 4), (2048, 16))
N_DIL = len(DIL_CONFIGS)
DIL_HEADS = 8
DIL_BLOCK = 128
DIL_KV_DIM = DIL_HEADS * HEAD_DIM
REL_BUCKETS = 32
REL_MAX_DIST = 2048
MOE_GROUPS = 4
MOE_EXPERTS_PER_GROUP = 8
MOE_EXPERTS = MOE_GROUPS * MOE_EXPERTS_PER_GROUP
MOE_TOP_K = 2
DEEPNORM_ALPHA = (2 * DEPTH) ** 0.25
LN_EPS = 1e-5
RMS_EPS = 1e-6
L2_EPS = 1e-6

LANES = 128
SUBLANES = 8
VMEM_LIMIT_BYTES = 56 * 1024 * 1024

MM_TM = 512
MM_TN = 1024
LN_TM = 256
GATE_TM = 512
GDN_TB = 512
GDN_QK_PER_STEP = 4
MOE_TM = 256
ROUTER_TM = 512
COMBINE_TM = 256


def _params(*semantics):
    return pltpu.CompilerParams(dimension_semantics=semantics, vmem_limit_bytes=VMEM_LIMIT_BYTES)


def _layer_norm_rows(y, gain, bias):
    mu = jnp.mean(y, axis=-1, keepdims=True)
    yc = y - mu
    var = jnp.mean(yc * yc, axis=-1, keepdims=True)
    return yc * lax.rsqrt(var + LN_EPS) * gain + bias


def _matmul_kernel(a_ref, b_ref, o_ref):
    o_ref[...] = jnp.dot(a_ref[...].astype(BF16), b_ref[...],
                         preferred_element_type=F32).astype(o_ref.dtype)


def matmul(a, b, out_dtype=F32, tm=MM_TM, tn=MM_TN):
    m, k = a.shape
    n = b.shape[1]
    tm, tn = min(tm, m), min(tn, n)
    assert m % tm == 0 and n % tn == 0
    return pl.pallas_call(
        _matmul_kernel,
        out_shape=jax.ShapeDtypeStruct((m, n), out_dtype),
        grid=(n // tn, m // tm),
        in_specs=[pl.BlockSpec((tm, k), lambda j, i: (i, 0)),
                  pl.BlockSpec((k, tn), lambda j, i: (0, j))],
        out_specs=pl.BlockSpec((tm, tn), lambda j, i: (i, j)),
        compiler_params=_params("parallel", "parallel"),
        name="matmul",
    )(a, b)


def _proj_ln_kernel(a_ref, w_ref, h_ref, gain_ref, bias_ref, o32_ref, o16_ref):
    mix = jnp.dot(a_ref[...].astype(BF16), w_ref[...], preferred_element_type=F32)
    y = _layer_norm_rows(DEEPNORM_ALPHA * h_ref[...] + mix, gain_ref[...], bias_ref[...])
    o32_ref[...] = y
    o16_ref[...] = y.astype(BF16)


def proj_residual_ln(a, w, h, gain, bias, tm=LN_TM):
    m, k = a.shape
    d = w.shape[1]
    tm = min(tm, m)
    assert m % tm == 0
    return pl.pallas_call(
        _proj_ln_kernel,
        out_shape=(jax.ShapeDtypeStruct((m, d), F32), jax.ShapeDtypeStruct((m, d), BF16)),
        grid=(m // tm,),
        in_specs=[pl.BlockSpec((tm, k), lambda i: (i, 0)),
                  pl.BlockSpec((k, d), lambda i: (0, 0)),
                  pl.BlockSpec((tm, d), lambda i: (i, 0)),
                  pl.BlockSpec((1, d), lambda i: (0, 0)),
                  pl.BlockSpec((1, d), lambda i: (0, 0))],
        out_specs=(pl.BlockSpec((tm, d), lambda i: (i, 0)),
                   pl.BlockSpec((tm, d), lambda i: (i, 0))),
        compiler_params=_params("parallel"),
        name="proj_residual_ln",
    )(a, w, h, gain.reshape(1, d), bias.reshape(1, d))


def _gdn_gate_kernel(h_ref, w_ref, alog_ref, dt_ref, o_ref):
    tm = h_ref.shape[0]
    ba = jnp.dot(h_ref[...], w_ref[...], precision=HIGHEST, preferred_element_type=F32)
    xa = ba + dt_ref[...]
    softplus = jnp.maximum(xa, 0.0) + jnp.log(1.0 + jnp.exp(-jnp.abs(xa)))
    g = -jnp.exp(alog_ref[...]) * softplus
    row = lax.broadcasted_iota(jnp.int32, (tm, tm), 0)
    col = lax.broadcasted_iota(jnp.int32, (tm, tm), 1)
    tri = ((row // GDN_CHUNK == col // GDN_CHUNK) & (col <= row)).astype(F32)
    gc = jnp.dot(tri, g, precision=HIGHEST, preferred_element_type=F32)
    lane = lax.broadcasted_iota(jnp.int32, ba.shape, 1)
    o_ref[...] = jnp.where(lane < GDN_V_HEADS, jax.nn.sigmoid(ba), gc)


def gdn_gates(h, w_ba, a_log, dt_bias, tm=GATE_TM):
    m, d = h.shape
    tm = min(tm, m)
    pad = LANES - 2 * GDN_V_HEADS
    w = jnp.pad(w_ba, ((0, 0), (0, pad)))
    zeros = jnp.zeros((GDN_V_HEADS,), F32)
    alog = jnp.concatenate([zeros, a_log, jnp.zeros((pad,), F32)]).reshape(1, LANES)
    dt = jnp.concatenate([zeros, dt_bias, jnp.zeros((pad,), F32)]).reshape(1, LANES)
    return pl.pallas_call(
        _gdn_gate_kernel,
        out_shape=jax.ShapeDtypeStruct((m, LANES), F32),
        grid=(m // tm,),
        in_specs=[pl.BlockSpec((tm, d), lambda i: (i, 0)),
                  pl.BlockSpec((d, LANES), lambda i: (0, 0)),
                  pl.BlockSpec((1, LANES), lambda i: (0, 0)),
                  pl.BlockSpec((1, LANES), lambda i: (0, 0))],
        out_specs=pl.BlockSpec((tm, LANES), lambda i: (i, 0)),
        compiler_params=_params("parallel"),
        name="gdn_gates",
    )(h, w, alog, dt)


def _dot_bf16(a, b):
    return jnp.dot(a.astype(BF16), b.astype(BF16), preferred_element_type=F32)


def _dot_nt(a, b):
    return lax.dot_general(a.astype(BF16), b.astype(BF16), (((1,), (1,)), ((), ())),
                           preferred_element_type=F32)


def _dot_tn(a, b):
    return lax.dot_general(a.astype(BF16), b.astype(BF16), (((0,), (0,)), ((), ())),
                           preferred_element_type=F32)


def _block_diag2(x0, x1):
    z = jnp.zeros_like(x0)
    return jnp.concatenate([jnp.concatenate([x0, z], axis=1), jnp.concatenate([z, x1], axis=1)], axis=0)


def _gdn_kernel(q_ref, k_ref, v_ref, z_ref, wq_ref, wk_ref, wv_ref, gcol_ref, grow_ref, nw_ref,
                o_ref, qx, kx, vx, qn, kn, vn, state):
    blk = pl.program_id(2)
    tb = q_ref.shape[0]
    n_chunks = tb // GDN_CHUNK
    n_qk = q_ref.shape[1] // HEAD_DIM
    c = GDN_CHUNK
    hd = HEAD_DIM
    tail = SUBLANES

    @pl.when(blk == 0)
    def _():
        qx[0:tail, :] = jnp.zeros((tail, qx.shape[1]), F32)
        kx[0:tail, :] = jnp.zeros((tail, kx.shape[1]), F32)
        vx[0:tail, :] = jnp.zeros((tail, vx.shape[1]), F32)
        state[...] = jnp.zeros(state.shape, F32)

    def conv_silu(x_ref, ext, w_ref):
        ext[tail:tail + tb, :] = x_ref[...].astype(F32)
        y = None
        for j in range(GDN_CONV):
            term = w_ref[j:j + 1, :] * ext[pl.ds(tail - (GDN_CONV - 1) + j, tb), :]
            y = term if y is None else y + term
        ext[0:tail, :] = ext[tb:tb + tail, :]
        return y * jax.nn.sigmoid(y)

    def l2n(x):
        heads = [x[:, g * hd:(g + 1) * hd] for g in range(n_qk)]
        return jnp.concatenate([xh * lax.rsqrt(jnp.sum(xh * xh, axis=-1, keepdims=True) + L2_EPS) for xh in heads], axis=1)

    qn[...] = l2n(conv_silu(q_ref, qx, wq_ref)) * (HEAD_DIM ** -0.5)
    kn[...] = l2n(conv_silu(k_ref, kx, wk_ref))
    vn[...] = conv_silu(v_ref, vx, wv_ref)

    row = lax.broadcasted_iota(jnp.int32, (c, 2 * c), 0)
    lane = lax.broadcasted_iota(jnp.int32, (c, 2 * c), 1)
    col = lane & (c - 1)
    head0 = lane < c
    causal = col <= row
    strict = col < row
    in_block = (row // 16) == (col // 16)
    eye = (row == col).astype(F32)
    norm_w = nw_ref[...]

    def mm(x, y):
        yb = jnp.concatenate([jnp.where(head0, y, 0.0), jnp.where(head0, 0.0, y)], axis=0)
        return _dot_bf16(x, yb)

    units = [(ci, g) for ci in range(n_chunks) for g in range(n_qk)]
    chunks = range(len(units))
    each = lambda f, *lists: [f(*xs) for xs in zip(*lists)]
    qcs = [qn[ci * c:(ci + 1) * c, g * hd:(g + 1) * hd] for ci, g in units]
    kcs = [kn[ci * c:(ci + 1) * c, g * hd:(g + 1) * hd] for ci, g in units]
    gcols = [gcol_ref[0, g, ci] for ci, g in units]
    grps = [grow_ref[0, g, ci] for ci, g in units]
    gcs = [[g[:, hh:hh + 1] for hh in range(2)] for g in gcols]
    betas = [[g[:, 2 + hh:3 + hh] for hh in range(2)] for g in gcols]
    egs = [[jnp.exp(g[hh]) for hh in range(2)] for g in gcs]
    glast = [[g[hh][c - 1:c, :] for hh in range(2)] for g in gcs]
    both = each(lambda q, k: _dot_nt(jnp.concatenate([q, k], axis=0), jnp.concatenate([k, k], axis=0)), qcs, kcs)
    decayp = each(lambda g, gr: jnp.exp(jnp.where(causal, jnp.where(head0, g[0], g[1]) - gr, -jnp.inf)), gcs, grps)
    ap = each(lambda b, bo, dc: jnp.where(strict, jnp.where(head0, b[0], b[1]) * bo[c:] * dc, 0.0), betas, both, decayp)
    qkd = each(lambda bo, dc: bo[:c] * dc, both, decayp)
    d = each(lambda a: jnp.where(in_block, a, 0.0), ap)
    e = each(lambda a, dd: a - dd, ap, d)
    n1 = each(lambda dd: -dd, d)
    p = each(lambda n: eye + n, n1)
    n2 = each(mm, n1, n1)
    p = each(lambda pp, n: pp + mm(pp, n), p, n2)
    n4 = each(mm, n2, n2)
    p = each(lambda pp, n: pp + mm(pp, n), p, n4)
    n8 = each(mm, n4, n4)
    dinv = each(lambda pp, n: pp + mm(pp, n), p, n8)
    nb = each(mm, dinv, e)
    nb2 = each(mm, nb, nb)
    qq = each(lambda n: eye - n, nb)
    qq = each(lambda x, n: x + mm(x, n), qq, nb2)
    tinvp = each(mm, qq, dinv)
    rhs = [[jnp.concatenate([vn[ci * c:(ci + 1) * c, (2 * g + hh) * hd:(2 * g + hh + 1) * hd] * betas[u][hh],
                             kcs[u] * (betas[u][hh] * egs[u][hh])], axis=1) for hh in range(2)]
           for u, (ci, g) in enumerate(units)]
    uw = each(lambda t, r: _dot_bf16(t, _block_diag2(r[0], r[1])), tinvp, rhs)
    wu = [[jnp.concatenate([x[:, (2 * hh + 1) * hd:(2 * hh + 2) * hd], x[:, 2 * hh * hd:(2 * hh + 1) * hd]], axis=1)
           for hh in range(2)] for x in uw]
    qo = each(lambda x, w2: _dot_bf16(x, _block_diag2(w2[0], w2[1])), qkd, wu)
    pre = {}
    for u, (ci, g) in enumerate(units):
        for hh in range(2):
            rb = _dot_tn(kcs[u] * jnp.exp(glast[u][hh] - gcs[u][hh]), wu[u][hh])
            q_eff = qcs[u] * egs[u][hh] - qo[u][:, 2 * hh * hd:(2 * hh + 1) * hd]
            rq = jnp.concatenate([rb[:, :hd], q_eff], axis=0).astype(BF16)
            pre[ci, 2 * g + hh] = (rq, rb[:, hd:], qo[u][:, (2 * hh + 1) * hd:(2 * hh + 2) * hd], jnp.exp(glast[u][hh]))

    n_v = 2 * n_qk
    states = [state[vh] for vh in range(n_v)]
    for ci in range(n_chunks):
        r0 = ci * c
        for vh in range(n_v):
            rq, b_c, o_c, cd = pre[ci, vh]
            s = states[vh]
            res = jnp.dot(rq, s.astype(BF16), preferred_element_type=F32)
            states[vh] = s * cd - res[:hd] + b_c
            out = res[hd:] + o_c
            out = out * lax.rsqrt(jnp.mean(out * out, axis=-1, keepdims=True) + RMS_EPS) * norm_w
            zc = z_ref[r0:r0 + c, vh * hd:(vh + 1) * hd].astype(F32)
            out = out * (zc * jax.nn.sigmoid(zc))
            o_ref[r0:r0 + c, vh * hd:(vh + 1) * hd] = out.astype(o_ref.dtype)
    for vh in range(n_v):
        state[vh] = states[vh]


def gdn_delta(proj, conv_w, gcol, grow, norm_w, batch, seq, tb=GDN_TB, n_qk=GDN_QK_PER_STEP):
    t = proj.shape[0]
    tb = min(tb, seq)
    nblk = seq // tb
    ncb = tb // GDN_CHUNK
    hd = HEAD_DIM
    qw = n_qk * hd
    vw = 2 * qw
    kq = GDN_QK_DIM // qw
    v0 = (2 * GDN_QK_DIM) // vw
    z0 = GDN_CONV_DIM // vw
    return pl.pallas_call(
        _gdn_kernel,
        out_shape=jax.ShapeDtypeStruct((t, GDN_V_DIM), BF16),
        grid=(batch, GDN_QK_HEADS // n_qk, nblk),
        in_specs=[
            pl.BlockSpec((tb, qw), lambda b, h, i: (b * nblk + i, h)),
            pl.BlockSpec((tb, qw), lambda b, h, i: (b * nblk + i, kq + h)),
            pl.BlockSpec((tb, vw), lambda b, h, i: (b * nblk + i, v0 + h)),
            pl.BlockSpec((tb, vw), lambda b, h, i: (b * nblk + i, z0 + h)),
            pl.BlockSpec((GDN_CONV, qw), lambda b, h, i: (0, h)),
            pl.BlockSpec((GDN_CONV, qw), lambda b, h, i: (0, kq + h)),
            pl.BlockSpec((GDN_CONV, vw), lambda b, h, i: (0, v0 + h)),
            pl.BlockSpec((1, n_qk, ncb, GDN_CHUNK, 4), lambda b, h, i: (b, h, i, 0, 0)),
            pl.BlockSpec((1, n_qk, ncb, 1, 2 * GDN_CHUNK), lambda b, h, i: (b, h, i, 0, 0)),
            pl.BlockSpec((1, hd), lambda b, h, i: (0, 0)),
        ],
        out_specs=pl.BlockSpec((tb, vw), lambda b, h, i: (b * nblk + i, h)),
        scratch_shapes=[
            pltpu.VMEM((tb + SUBLANES, qw), F32),
            pltpu.VMEM((tb + SUBLANES, qw), F32),
            pltpu.VMEM((tb + SUBLANES, vw), F32),
            pltpu.VMEM((tb, qw), F32),
            pltpu.VMEM((tb, qw), F32),
            pltpu.VMEM((tb, vw), F32),
            pltpu.VMEM((2 * n_qk, hd, hd), F32),
        ],
        compiler_params=_params("parallel", "parallel", "arbitrary"),
        name="gdn_delta",
    )(proj, proj, proj, proj, conv_w, conv_w, conv_w, gcol, grow, norm_w.reshape(1, hd))


def gated_deltanet_layer(h32, h16, w_in, conv_w, a_log, dt_bias, norm_w, w_out, gain, bias, batch, seq):
    t = h32.shape[0]
    main = GDN_CONV_DIM + GDN_V_DIM
    proj = matmul(h16, w_in[:, :main].astype(BF16))
    gates = gdn_gates(h32, w_in[:, main:], a_log, dt_bias)
    nc = seq // GDN_CHUNK
    beta = gates[:, :GDN_V_HEADS].reshape(batch, nc, GDN_CHUNK, GDN_QK_HEADS, 2)
    gc = gates[:, GDN_V_HEADS:2 * GDN_V_HEADS].reshape(batch, nc, GDN_CHUNK, GDN_QK_HEADS, 2)
    gcol = jnp.transpose(jnp.concatenate([gc, beta], axis=-1), (0, 3, 1, 2, 4))
    grow = jnp.transpose(gc, (0, 3, 1, 4, 2)).reshape(batch, GDN_QK_HEADS, nc, 1, 2 * GDN_CHUNK)
    o = gdn_delta(proj, conv_w, gcol, grow, norm_w, batch, seq)
    del t
    return proj_residual_ln(o, w_out.astype(BF16), h32, gain, bias)


def _dil_attn_kernel(q_ref, kp_ref, ko_ref, vp_ref, vo_ref, bias_ref, o_ref, lse_ref):
    lb = pl.program_id(2)
    blk = DIL_BLOCK
    qi = lax.broadcasted_iota(jnp.int32, (blk, blk), 0)
    kj = lax.broadcasted_iota(jnp.int32, (blk, blk), 1)
    mask_prev = (qi <= kj) & (lb > 0)
    mask_own = kj <= qi
    lane = lax.broadcasted_iota(jnp.int32, (blk, LANES), 1)
    lse_all = jnp.zeros((blk, LANES), F32)
    scale = HEAD_DIM ** -0.5
    heads = range(DIL_HEADS)
    sls = [slice(hd * HEAD_DIM, (hd + 1) * HEAD_DIM) for hd in heads]
    s_prev = [jnp.where(mask_prev, _dot_nt(q_ref[0, :, sl], kp_ref[0, :, sl]) * scale + bias_ref[hd][:, :blk], -jnp.inf)
              for hd, sl in zip(heads, sls)]
    s_own = [jnp.where(mask_own, _dot_nt(q_ref[0, :, sl], ko_ref[0, :, sl]) * scale + bias_ref[hd][:, blk:], -jnp.inf)
             for hd, sl in zip(heads, sls)]
    m = [jnp.maximum(jnp.max(sp, axis=-1, keepdims=True), jnp.max(so, axis=-1, keepdims=True))
         for sp, so in zip(s_prev, s_own)]
    p_prev = [jnp.exp(sp - mm) for sp, mm in zip(s_prev, m)]
    p_own = [jnp.exp(so - mm) for so, mm in zip(s_own, m)]
    den = [jnp.sum(pp, axis=-1, keepdims=True) + jnp.sum(po, axis=-1, keepdims=True) for pp, po in zip(p_prev, p_own)]
    o = [_dot_bf16(pp, vp_ref[0, :, sl]) + _dot_bf16(po, vo_ref[0, :, sl]) for pp, po, sl in zip(p_prev, p_own, sls)]
    for hd, sl in zip(heads, sls):
        o_ref[0, :, sl] = o[hd] / den[hd]
        lse_all = jnp.where(lane == hd, m[hd] + jnp.log(den[hd]), lse_all)
    lse_ref[0] = lse_all


def dilated_group_attention(q, k, v, bias, dil, batch, seq):
    length = seq // dil
    nb = length // DIL_BLOCK
    kvd = DIL_KV_DIM
    qv = q.reshape(batch, length, dil * kvd)
    kview = k.reshape(batch, length, dil * kvd)
    vview = v.reshape(batch, length, dil * kvd)
    blk = DIL_BLOCK
    own = lambda b, r, i: (b, i, r)
    prev = lambda b, r, i: (b, jnp.maximum(i - 1, 0), r)
    o, lse = pl.pallas_call(
        _dil_attn_kernel,
        out_shape=(jax.ShapeDtypeStruct((batch, length, dil * kvd), F32),
                   jax.ShapeDtypeStruct((batch, length, dil * LANES), F32)),
        grid=(batch, dil, nb),
        in_specs=[
            pl.BlockSpec((1, blk, kvd), own),
            pl.BlockSpec((1, blk, kvd), prev),
            pl.BlockSpec((1, blk, kvd), own),
            pl.BlockSpec((1, blk, kvd), prev),
            pl.BlockSpec((1, blk, kvd), own),
            pl.BlockSpec((DIL_HEADS, blk, 2 * blk), lambda b, r, i: (0, 0, 0)),
        ],
        out_specs=(pl.BlockSpec((1, blk, kvd), own), pl.BlockSpec((1, blk, LANES), own)),
        compiler_params=_params("parallel", "parallel", "parallel"),
        name=f"dilated_attention_d{dil}",
    )(qv, kview, kview, vview, vview, bias)
    return o.reshape(batch * seq, kvd), lse.reshape(batch * seq, LANES)


def _dil_out_kernel(o0_ref, o1_ref, o2_ref, l0_ref, l1_ref, l2_ref, w_ref, h_ref, gain_ref, bias_ref,
                    o32_ref, o16_ref, mix_ref):
    l0, l1, l2 = l0_ref[...], l1_ref[...], l2_ref[...]
    m = jnp.maximum(jnp.maximum(l0, l1), l2)
    e0, e1, e2 = jnp.exp(l0 - m), jnp.exp(l1 - m), jnp.exp(l2 - m)
    inv = 1.0 / (e0 + e1 + e2)
    w0, w1, w2 = e0 * inv, e1 * inv, e2 * inv
    for hd in range(DIL_HEADS):
        sl = slice(hd * HEAD_DIM, (hd + 1) * HEAD_DIM)
        mix_ref[:, sl] = (w0[:, hd:hd + 1] * o0_ref[:, sl] + w1[:, hd:hd + 1] * o1_ref[:, sl]
                          + w2[:, hd:hd + 1] * o2_ref[:, sl])
    mix = jnp.dot(mix_ref[...].astype(BF16), w_ref[...], preferred_element_type=F32)
    y = _layer_norm_rows(DEEPNORM_ALPHA * h_ref[...] + mix, gain_ref[...], bias_ref[...])
    o32_ref[...] = y
    o16_ref[...] = y.astype(BF16)


def dilated_out(outs, lses, w_o, h, gain, bias, tm=LN_TM):
    m, kvd = outs[0].shape
    d = w_o.shape[1]
    tm = min(tm, m)
    row = lambda i: (i, 0)
    fixed = lambda i: (0, 0)
    return pl.pallas_call(
        _dil_out_kernel,
        out_shape=(jax.ShapeDtypeStruct((m, d), F32), jax.ShapeDtypeStruct((m, d), BF16)),
        grid=(m // tm,),
        in_specs=[pl.BlockSpec((tm, kvd), row)] * 3 + [pl.BlockSpec((tm, LANES), row)] * 3 + [
            pl.BlockSpec((kvd, d), fixed), pl.BlockSpec((tm, d), row),
            pl.BlockSpec((1, d), fixed), pl.BlockSpec((1, d), fixed)],
        out_specs=(pl.BlockSpec((tm, d), row), pl.BlockSpec((tm, d), row)),
        scratch_shapes=[pltpu.VMEM((tm, kvd), F32)],
        compiler_params=_params("parallel"),
        name="dilated_out_ln",
    )(*outs, *lses, w_o, h, gain.reshape(1, d), bias.reshape(1, d))


def _t5_bucket(dist):
    exact = REL_BUCKETS // 2
    distf = jnp.maximum(dist, exact).astype(F32)
    large = exact + (jnp.log(distf / exact) / math.log(REL_MAX_DIST / exact)
                     * (REL_BUCKETS - exact)).astype(jnp.int32)
    return jnp.where(dist < exact, dist, jnp.minimum(large, REL_BUCKETS - 1))


def _band_bias(rel_bias_g, dil):
    qi = jnp.arange(DIL_BLOCK)[:, None]
    kj = jnp.arange(2 * DIL_BLOCK)[None, :]
    steps = DIL_BLOCK + qi - kj
    bias = rel_bias_g[_t5_bucket(jnp.maximum(steps, 0) * dil)]
    return jnp.transpose(bias, (2, 0, 1)).astype(F32)


def _router_kernel(h_ref, w_ref, b_ref, o_ref, cnt_ref, running):
    @pl.when(pl.program_id(0) == 0)
    def _():
        running[...] = jnp.zeros(running.shape, F32)

    logits = jnp.dot(h_ref[...], w_ref[...], precision=HIGHEST, preferred_element_type=F32) + b_ref[...]
    lane = lax.broadcasted_iota(jnp.int32, logits.shape, 1)
    neg = -jnp.inf
    big = jnp.int32(LANES)
    is_group = lane < MOE_GROUPS
    gl = jnp.where(is_group, logits, neg)
    gmax = jnp.max(gl, axis=-1, keepdims=True)
    gidx = jnp.min(jnp.where(gl == gmax, lane, big), axis=-1, keepdims=True)
    gden = jnp.sum(jnp.exp(gl - gmax), axis=-1, keepdims=True)
    group_gate = 1.0 / gden
    lo = MOE_GROUPS + gidx * MOE_EXPERTS_PER_GROUP
    in_group = (lane >= lo) & (lane < lo + MOE_EXPERTS_PER_GROUP)
    ll = jnp.where(in_group, logits, neg)
    m1 = jnp.max(ll, axis=-1, keepdims=True)
    i1 = jnp.min(jnp.where(ll == m1, lane, big), axis=-1, keepdims=True)
    rest = jnp.where(lane == i1, neg, ll)
    m2 = jnp.max(rest, axis=-1, keepdims=True)
    i2 = jnp.min(jnp.where(rest == m2, lane, big), axis=-1, keepdims=True)
    den = jnp.sum(jnp.exp(ll - m1), axis=-1, keepdims=True)
    p1 = 1.0 / den
    p2 = jnp.exp(m2 - m1) / den
    g1 = group_gate * p1 / (p1 + p2)
    g2 = group_gate * p2 / (p1 + p2)
    id1 = (i1 - MOE_GROUPS).astype(F32)
    id2 = (i2 - MOE_GROUPS).astype(F32)
    tm = logits.shape[0]
    oh1 = (lane == i1).astype(F32)
    oh2 = (lane == i2).astype(F32)
    both = oh1 + oh2
    r_i = lax.broadcasted_iota(jnp.int32, (tm, tm), 0)
    c_i = lax.broadcasted_iota(jnp.int32, (tm, tm), 1)
    before = jnp.dot((c_i < r_i).astype(BF16), both.astype(BF16), preferred_element_type=F32) + running[...]
    rank1 = jnp.sum(before * oh1, axis=-1, keepdims=True)
    rank2 = jnp.sum(before * oh2, axis=-1, keepdims=True)
    running[...] = running[...] + jnp.sum(both, axis=0, keepdims=True)
    cnt_ref[...] = running[...]
    o_ref[...] = jnp.where(lane == 0, id1, jnp.where(lane == 1, id2, jnp.where(lane == 2, g1, jnp.where(
        lane == 3, g2, jnp.where(lane == 4, rank1, rank2)))))


def moe_router(h, w_group, b_group, w_expert, b_expert, tm=ROUTER_TM):
    m, d = h.shape
    tm = min(tm, m)
    pad = LANES - MOE_GROUPS - MOE_EXPERTS
    w = jnp.pad(jnp.concatenate([w_group, w_expert], axis=1), ((0, 0), (0, pad)))
    b = jnp.pad(jnp.concatenate([b_group, b_expert]), (0, pad)).reshape(1, LANES)
    out, cnt = pl.pallas_call(
        _router_kernel,
        out_shape=(jax.ShapeDtypeStruct((m, LANES), F32), jax.ShapeDtypeStruct((1, LANES), F32)),
        grid=(m // tm,),
        in_specs=[pl.BlockSpec((tm, d), lambda i: (i, 0)),
                  pl.BlockSpec((d, LANES), lambda i: (0, 0)),
                  pl.BlockSpec((1, LANES), lambda i: (0, 0))],
        out_specs=(pl.BlockSpec((tm, LANES), lambda i: (i, 0)), pl.BlockSpec((1, LANES), lambda i: (0, 0))),
        scratch_shapes=[pltpu.VMEM((1, LANES), F32)],
        compiler_params=_params("arbitrary"),
        name="moe_router",
    )(h, w, b)
    k = MOE_TOP_K
    counts = cnt[0, MOE_GROUPS:MOE_GROUPS + MOE_EXPERTS].astype(jnp.int32)
    return out[:, :k].astype(jnp.int32), out[:, k:2 * k], out[:, 2 * k:3 * k].astype(jnp.int32), counts


GATHER_UNROLL = 8


def _gather_rows_start(src_hbm, idx_ref, idx_base, buf, slot, sem):
    def body(g, carry):
        r0 = pl.multiple_of(g * GATHER_UNROLL, GATHER_UNROLL)
        for j in range(GATHER_UNROLL):
            pltpu.make_async_copy(src_hbm.at[pl.ds(idx_ref[idx_base + r0 + j], 1)],
                                  buf.at[slot, pl.ds(r0 + j, 1)], sem.at[slot]).start()
        return carry
    lax.fori_loop(0, buf.shape[1] // GATHER_UNROLL, body, 0)


def _gather_rows_wait(src_hbm, buf, slot, sem):
    pltpu.make_async_copy(src_hbm.at[pl.ds(0, buf.shape[1])], buf.at[slot], sem.at[slot]).wait()


def _expert_kernel(bexp_ref, rtok_ref, nused_ref, h_hbm, w1_ref, w3_ref, w2_ref, y_ref, xbuf, sem, wb1, wb3, wb2):
    i = pl.program_id(0)
    tm = xbuf.shape[1]
    n_used = nused_ref[0]
    slot = i % 2

    def issue(block, to_slot):
        _gather_rows_start(h_hbm, rtok_ref, block * tm, xbuf, to_slot, sem)

    @pl.when(i == 0)
    def _():
        issue(0, 0)

    @pl.when(i + 1 < n_used)
    def _():
        issue(i + 1, 1 - slot)

    @pl.when(i < n_used)
    def _():
        @pl.when((i == 0) | (bexp_ref[i] != bexp_ref[jnp.maximum(i - 1, 0)]))
        def _():
            wb1[...] = w1_ref[0].astype(BF16)
            wb3[...] = w3_ref[0].astype(BF16)
            wb2[...] = w2_ref[0].astype(BF16)

        _gather_rows_wait(h_hbm, xbuf, slot, sem)
        x = xbuf[slot].astype(BF16)
        h1 = jnp.dot(x, wb1[...], preferred_element_type=F32)
        h3 = jnp.dot(x, wb3[...], preferred_element_type=F32)
        hmid = (h1 * jax.nn.sigmoid(h1) * h3).astype(BF16)
        y_ref[...] = jnp.dot(hmid, wb2[...], preferred_element_type=F32)

    @pl.when(i >= n_used)
    def _():
        y_ref[...] = jnp.zeros(y_ref.shape, y_ref.dtype)


def moe_experts(h, row_token, block_expert, n_used, w1, w3, w2, tm=MOE_TM):
    n_rows = row_token.shape[0]
    n_blocks = n_rows // tm
    d = h.shape[1]
    ff = w1.shape[2]
    grid_spec = pltpu.PrefetchScalarGridSpec(
        num_scalar_prefetch=3,
        grid=(n_blocks,),
        in_specs=[
            pl.BlockSpec(memory_space=pl.ANY),
            pl.BlockSpec((1, d, ff), lambda i, be, rt, nu: (be[i], 0, 0)),
            pl.BlockSpec((1, d, ff), lambda i, be, rt, nu: (be[i], 0, 0)),
            pl.BlockSpec((1, ff, d), lambda i, be, rt, nu: (be[i], 0, 0)),
        ],
        out_specs=pl.BlockSpec((tm, d), lambda i, be, rt, nu: (i, 0)),
        scratch_shapes=[pltpu.VMEM((2, tm, d), F32), pltpu.SemaphoreType.DMA((2,)),
                        pltpu.VMEM((d, ff), BF16), pltpu.VMEM((d, ff), BF16), pltpu.VMEM((ff, d), BF16)],
    )
    return pl.pallas_call(
        _expert_kernel,
        out_shape=jax.ShapeDtypeStruct((n_rows, d), F32),
        grid_spec=grid_spec,
        compiler_params=_params("arbitrary"),
        name="moe_experts",
    )(block_expert, row_token, n_used, h, w1, w3, w2)


def _combine_kernel(dest_ref, y_hbm, gate_ref, h_ref, gain_ref, bias_ref, o32_ref, o16_ref, ybuf, sem):
    i = pl.program_id(0)
    n = pl.num_programs(0)
    tm = h_ref.shape[0]
    rows = MOE_TOP_K * tm
    slot = i % 2

    def issue(block, to_slot):
        _gather_rows_start(y_hbm, dest_ref, block * rows, ybuf, to_slot, sem)

    @pl.when(i == 0)
    def _():
        issue(0, 0)

    @pl.when(i + 1 < n)
    def _():
        issue(i + 1, 1 - slot)

    _gather_rows_wait(y_hbm, ybuf, slot, sem)
    gates = gate_ref[...]
    ffn = ybuf[slot, 0:tm, :] * gates[:, 0:1] + ybuf[slot, tm:rows, :] * gates[:, 1:2]
    y = _layer_norm_rows(DEEPNORM_ALPHA * h_ref[...] + ffn, gain_ref[...], bias_ref[...])
    o32_ref[...] = y
    o16_ref[...] = y.astype(BF16)


def moe_combine_ln(y_rows, dest, gates, h, gain, bias, tm=COMBINE_TM):
    m, d = h.shape
    tm = min(tm, m)
    grid_spec = pltpu.PrefetchScalarGridSpec(
        num_scalar_prefetch=1,
        grid=(m // tm,),
        in_specs=[
            pl.BlockSpec(memory_space=pl.ANY),
            pl.BlockSpec((tm, MOE_TOP_K), lambda i, de: (i, 0)),
            pl.BlockSpec((tm, d), lambda i, de: (i, 0)),
            pl.BlockSpec((1, d), lambda i, de: (0, 0)),
            pl.BlockSpec((1, d), lambda i, de: (0, 0)),
        ],
        out_specs=(pl.BlockSpec((tm, d), lambda i, de: (i, 0)),
                   pl.BlockSpec((tm, d), lambda i, de: (i, 0))),
        scratch_shapes=[pltpu.VMEM((2, MOE_TOP_K * tm, d), F32), pltpu.SemaphoreType.DMA((2,))],
    )
    return pl.pallas_call(
        _combine_kernel,
        out_shape=(jax.ShapeDtypeStruct((m, d), F32), jax.ShapeDtypeStruct((m, d), BF16)),
        grid_spec=grid_spec,
        compiler_params=_params("arbitrary"),
        name="moe_combine_ln",
    )(dest.reshape(-1), y_rows, gates, h, gain.reshape(1, d), bias.reshape(1, d))


def moe_layer(h32, w_group, b_group, w_expert, b_expert, w1, w3, w2, gain, bias, tm=MOE_TM):
    t = h32.shape[0]
    ids, gates, ranks, counts = moe_router(h32, w_group, b_group, w_expert, b_expert)
    eid = ids.reshape(-1)
    n_pairs = eid.shape[0]
    padded = (counts + tm - 1) // tm * tm
    seg_end = jnp.cumsum(padded)
    seg_start = seg_end - padded
    onehot = eid[:, None] == jnp.arange(MOE_EXPERTS, dtype=jnp.int32)[None, :]
    dest = (jnp.sum(jnp.where(onehot, seg_start[None, :], 0), axis=1) + ranks.reshape(-1)).astype(jnp.int32)
    n_blocks = -(-n_pairs // tm) + MOE_EXPERTS
    n_rows = n_blocks * tm
    token_of_pair = (jnp.arange(n_pairs, dtype=jnp.int32) // MOE_TOP_K)
    row_token = jnp.zeros((n_rows,), jnp.int32).at[dest].set(token_of_pair)
    block_expert = jnp.minimum(
        jnp.searchsorted(seg_end, jnp.arange(n_blocks, dtype=jnp.int32) * tm, side='right'),
        MOE_EXPERTS - 1).astype(jnp.int32)
    n_used = (seg_end[-1:] // tm).astype(jnp.int32)
    y_rows = moe_experts(h32, row_token, block_expert, n_used, w1, w3, w2, tm=tm)
    ctm = min(COMBINE_TM, t)
    dest_tiles = jnp.transpose(dest.reshape(t // ctm, ctm, MOE_TOP_K), (0, 2, 1))
    return moe_combine_ln(y_rows, dest_tiles, gates, h32, gain, bias, tm=ctm)


def kernel(x, gdn_w_in, gdn_conv, gdn_a_log, gdn_dt_bias, gdn_norm, gdn_w_out, kv_w_k, kv_w_v, dil_w_q, dil_w_o,
           rel_bias, ln_gain, ln_bias, moe_w_group, moe_b_group, moe_w_expert, moe_b_expert, moe_w1, moe_w3, moe_w2):
    batch, seq, d = x.shape
    t = batch * seq
    h32 = x.reshape(t, d)
    h16 = h32.astype(BF16)
    k = v = None
    biases = None
    for layer in range(DEPTH):
        if layer < N_A_LAYERS:
            h32, h16 = gated_deltanet_layer(
                h32, h16, gdn_w_in[layer], gdn_conv[layer], gdn_a_log[layer], gdn_dt_bias[layer],
                gdn_norm[layer], gdn_w_out[layer], ln_gain[layer, 0], ln_bias[layer, 0], batch, seq)
        else:
            if layer == N_A_LAYERS:
                k = matmul(h16, kv_w_k.astype(BF16), out_dtype=BF16)
                v = matmul(h16, kv_w_v.astype(BF16), out_dtype=BF16)
                biases = [_band_bias(rel_bias[:, gi * DIL_HEADS:(gi + 1) * DIL_HEADS], dil)
                          for gi, (_, dil) in enumerate(DIL_CONFIGS)]
            j = layer - N_A_LAYERS
            outs, lses = [], []
            for gi, (_, dil) in enumerate(DIL_CONFIGS):
                q = matmul(h16, dil_w_q[j][:, gi * DIL_KV_DIM:(gi + 1) * DIL_KV_DIM].astype(BF16), out_dtype=BF16)
                o, lse = dilated_group_attention(q, k, v, biases[gi], dil, batch, seq)
                outs.append(o)
                lses.append(lse)
            h32, h16 = dilated_out(outs, lses, dil_w_o[j].astype(BF16), h32, ln_gain[layer, 0], ln_bias[layer, 0])
        h32, h16 = moe_layer(h32, moe_w_group[layer], moe_b_group[layer], moe_w_expert[layer], moe_b_expert[layer],
                             moe_w1[layer], moe_w3[layer], moe_w2[layer],
                             ln_gain[layer, 1], ln_bias[layer, 1])
    return h32.reshape(batch, seq, d)
```

```python
import math

import jax
import jax.numpy as jnp
from jax import lax
from jax.experimental import pallas as pl
from jax.experimental.pallas import tpu as pltpu

F32 = jnp.float32
BF16 = jnp.bfloat16

DEPTH = 4
N_A_LAYERS = DEPTH // 2
HEAD_DIM = 128
GDN_QK_HEADS = 16
GDN_V_HEADS = 32
GDN_CONV = 4
GDN_CHUNK = 64
GDN_QK_DIM = GDN_QK_HEADS * HEAD_DIM
GDN_V_DIM = GDN_V_HEADS * HEAD_DIM
GDN_CONV_DIM = 2 * GDN_QK_DIM + GDN_V_DIM
DIL_CONFIGS = ((128, 1), (512, 4), (2048, 16))
N_DIL = len(DIL_CONFIGS)
DIL_HEADS = 8
DIL_BLOCK = 128
DIL_KV_DIM = DIL_HEADS * HEAD_DIM
REL_BUCKETS = 32
REL_MAX_DIST = 2048
MOE_GROUPS = 4
MOE_EXPERTS_PER_GROUP = 8
MOE_EXPERTS = MOE_GROUPS * MOE_EXPERTS_PER_GROUP
MOE_TOP_K = 2
DEEPNORM_ALPHA = (2 * DEPTH) ** 0.25
LN_EPS = 1e-5
RMS_EPS = 1e-6
L2_EPS = 1e-6

LANES = 128
SUBLANES = 8
VMEM_LIMIT_BYTES = 56 * 1024 * 1024

MM_TM = 512
MM_TN = 1024
LN_TM = 256
GATE_TM = 512
GDN_TB = 512
GDN_QK_PER_STEP = 4
MOE_TM = 256
ROUTER_TM = 512
COMBINE_TM = 256


def _params(*semantics):
    return pltpu.CompilerParams(dimension_semantics=semantics, vmem_limit_bytes=VMEM_LIMIT_BYTES)


def _split_bf16(x):
    hi = x.astype(BF16)
    return hi, (x - hi.astype(F32)).astype(BF16)


def _dot_split(x, w2_ref):
    n = w2_ref.shape[1] // 2
    hi, lo = _split_bf16(x)
    a = jnp.dot(hi, w2_ref[...], preferred_element_type=F32)
    b = jnp.dot(lo, w2_ref[:, :n], preferred_element_type=F32)
    return a[:, :n] + a[:, n:] + b


def _hi_lo_columns(w):
    hi = w.astype(BF16)
    return jnp.concatenate([hi, (w - hi.astype(F32)).astype(BF16)], axis=1)


def _layer_norm_rows(y, gain, bias):
    mu = jnp.mean(y, axis=-1, keepdims=True)
    yc = y - mu
    var = jnp.mean(yc * yc, axis=-1, keepdims=True)
    return yc * lax.rsqrt(var + LN_EPS) * gain + bias


def _matmul_kernel(a_ref, b_ref, o_ref):
    o_ref[...] = jnp.dot(a_ref[...].astype(BF16), b_ref[...],
                         preferred_element_type=F32).astype(o_ref.dtype)


def matmul(a, b, out_dtype=F32, tm=MM_TM, tn=MM_TN):
    m, k = a.shape
    n = b.shape[1]
    tm, tn = min(tm, m), min(tn, n)
    assert m % tm == 0 and n % tn == 0
    return pl.pallas_call(
        _matmul_kernel,
        out_shape=jax.ShapeDtypeStruct((m, n), out_dtype),
        grid=(n // tn, m // tm),
        in_specs=[pl.BlockSpec((tm, k), lambda j, i: (i, 0)),
                  pl.BlockSpec((k, tn), lambda j, i: (0, j))],
        out_specs=pl.BlockSpec((tm, tn), lambda j, i: (i, j)),
        compiler_params=_params("parallel", "parallel"),
        name="matmul",
    )(a, b)


def _proj_ln_kernel(a_ref, w_ref, h_ref, gain_ref, bias_ref, o32_ref, o16_ref):
    mix = jnp.dot(a_ref[...].astype(BF16), w_ref[...], preferred_element_type=F32)
    y = _layer_norm_rows(DEEPNORM_ALPHA * h_ref[...] + mix, gain_ref[...], bias_ref[...])
    o32_ref[...] = y
    o16_ref[...] = y.astype(BF16)


def proj_residual_ln(a, w, h, gain, bias, tm=LN_TM):
    m, k = a.shape
    d = w.shape[1]
    tm = min(tm, m)
    assert m % tm == 0
    return pl.pallas_call(
        _proj_ln_kernel,
        out_shape=(jax.ShapeDtypeStruct((m, d), F32), jax.ShapeDtypeStruct((m, d), BF16)),
        grid=(m // tm,),
        in_specs=[pl.BlockSpec((tm, k), lambda i: (i, 0)),
                  pl.BlockSpec((k, d), lambda i: (0, 0)),
                  pl.BlockSpec((tm, d), lambda i: (i, 0)),
                  pl.BlockSpec((1, d), lambda i: (0, 0)),
                  pl.BlockSpec((1, d), lambda i: (0, 0))],
        out_specs=(pl.BlockSpec((tm, d), lambda i: (i, 0)),
                   pl.BlockSpec((tm, d), lambda i: (i, 0))),
        compiler_params=_params("parallel"),
        name="proj_residual_ln",
    )(a, w, h, gain.reshape(1, d), bias.reshape(1, d))


def _gdn_gate_kernel(h_ref, w_ref, alog_ref, dt_ref, o_ref):
    tm = h_ref.shape[0]
    n_groups = o_ref.shape[0]
    width = 2 * GDN_V_HEADS // n_groups
    ba = _dot_split(h_ref[...], w_ref)
    xa = ba + dt_ref[...]
    softplus = jnp.maximum(xa, 0.0) + jnp.log(1.0 + jnp.exp(-jnp.abs(xa)))
    g = -jnp.exp(alog_ref[...]) * softplus
    row = lax.broadcasted_iota(jnp.int32, (tm, tm), 0)
    col = lax.broadcasted_iota(jnp.int32, (tm, tm), 1)
    tri = ((row // GDN_CHUNK == col // GDN_CHUNK) & (col <= row)).astype(BF16)
    g_hi, g_lo = _split_bf16(g)
    gc2 = jnp.dot(tri, jnp.concatenate([g_hi, g_lo], axis=1), preferred_element_type=F32)
    gc = gc2[:, :LANES] + gc2[:, LANES:]
    lane = lax.broadcasted_iota(jnp.int32, ba.shape, 1)
    slab = jnp.where(lane % width < width // 2, jax.nn.sigmoid(ba), gc)
    for r in range(n_groups):
        o_ref[r] = slab if r == 0 else pltpu.roll(slab, LANES - width * r, axis=1)


def gdn_gates(h, w_ba, a_log, dt_bias, n_qk=GDN_QK_PER_STEP, tm=GATE_TM):
    m, d = h.shape
    tm = min(tm, m)
    v = 2 * n_qk
    n_groups = GDN_V_HEADS // v
    cols = [(v * (l // (2 * v)) + l % (2 * v)) if l % (2 * v) < v else (GDN_V_HEADS + v * (l // (2 * v)) + l % (2 * v) - v)
            for l in range(2 * GDN_V_HEADS)]
    is_decay = jnp.array([l % (2 * v) >= v for l in range(2 * GDN_V_HEADS)])
    head = jnp.array([c % GDN_V_HEADS for c in cols], jnp.int32)
    pad = LANES - 2 * GDN_V_HEADS
    w = jnp.pad(w_ba[:, jnp.array(cols, jnp.int32)], ((0, 0), (0, pad)))
    alog = jnp.pad(jnp.where(is_decay, a_log[head], 0.0), (0, pad)).reshape(1, LANES)
    dt = jnp.pad(jnp.where(is_decay, dt_bias[head], 0.0), (0, pad)).reshape(1, LANES)
    return pl.pallas_call(
        _gdn_gate_kernel,
        out_shape=jax.ShapeDtypeStruct((n_groups, m, LANES), F32),
        grid=(m // tm,),
        in_specs=[pl.BlockSpec((tm, d), lambda i: (i, 0)),
                  pl.BlockSpec((d, 2 * LANES), lambda i: (0, 0)),
                  pl.BlockSpec((1, LANES), lambda i: (0, 0)),
                  pl.BlockSpec((1, LANES), lambda i: (0, 0))],
        out_specs=pl.BlockSpec((n_groups, tm, LANES), lambda i: (0, i, 0)),
        compiler_params=_params("parallel"),
        name="gdn_gates",
    )(h, _hi_lo_columns(w), alog, dt)


def _dot_bf16(a, b):
    return jnp.dot(a.astype(BF16), b.astype(BF16), preferred_element_type=F32)


def _dot_nt(a, b):
    return lax.dot_general(a.astype(BF16), b.astype(BF16), (((1,), (1,)), ((), ())),
                           preferred_element_type=F32)


def _dot_tn(a, b):
    return lax.dot_general(a.astype(BF16), b.astype(BF16), (((0,), (0,)), ((), ())),
                           preferred_element_type=F32)


def _block_diag2(x0, x1):
    z = jnp.zeros_like(x0)
    return jnp.concatenate([jnp.concatenate([x0, z], axis=1), jnp.concatenate([z, x1], axis=1)], axis=0)


def _gdn_kernel(q_ref, k_ref, v_ref, z_ref, wq_ref, wk_ref, wv_ref, g_ref, nw_ref,
                o_ref, qx, kx, vx, qn, kn, vn, state):
    blk = pl.program_id(2)
    tb = q_ref.shape[0]
    n_chunks = tb // GDN_CHUNK
    n_qk = q_ref.shape[1] // HEAD_DIM
    c = GDN_CHUNK
    hd = HEAD_DIM
    tail = SUBLANES

    @pl.when(blk == 0)
    def _():
        qx[0:tail, :] = jnp.zeros((tail, qx.shape[1]), F32)
        kx[0:tail, :] = jnp.zeros((tail, kx.shape[1]), F32)
        vx[0:tail, :] = jnp.zeros((tail, vx.shape[1]), F32)
        state[...] = jnp.zeros(state.shape, F32)

    def conv_silu(x_ref, ext, w_ref):
        ext[tail:tail + tb, :] = x_ref[...].astype(F32)
        y = None
        for j in range(GDN_CONV):
            term = w_ref[j:j + 1, :] * ext[pl.ds(tail - (GDN_CONV - 1) + j, tb), :]
            y = term if y is None else y + term
        ext[0:tail, :] = ext[tb:tb + tail, :]
        return y * jax.nn.sigmoid(y)

    def l2n(x):
        heads = [x[:, g * hd:(g + 1) * hd] for g in range(n_qk)]
        return jnp.concatenate([xh * lax.rsqrt(jnp.sum(xh * xh, axis=-1, keepdims=True) + L2_EPS) for xh in heads], axis=1)

    qn[...] = l2n(conv_silu(q_ref, qx, wq_ref)) * (HEAD_DIM ** -0.5)
    kn[...] = l2n(conv_silu(k_ref, kx, wk_ref))
    vn[...] = conv_silu(v_ref, vx, wv_ref)

    row = lax.broadcasted_iota(jnp.int32, (c, 2 * c), 0)
    lane = lax.broadcasted_iota(jnp.int32, (c, 2 * c), 1)
    col = lane & (c - 1)
    head0 = lane < c
    causal = col <= row
    strict = col < row
    in_block = (row // 16) == (col // 16)
    eye = (row == col).astype(F32)
    norm_w = nw_ref[...]

    def mm(x, y):
        yb = jnp.concatenate([jnp.where(head0, y, 0.0), jnp.where(head0, 0.0, y)], axis=0)
        return _dot_bf16(x, yb)

    units = [(ci, g) for ci in range(n_chunks) for g in range(n_qk)]
    each = lambda f, *lists: [f(*xs) for xs in zip(*lists)]
    qcs = [qn[ci * c:(ci + 1) * c, g * hd:(g + 1) * hd] for ci, g in units]
    kcs = [kn[ci * c:(ci + 1) * c, g * hd:(g + 1) * hd] for ci, g in units]
    n_v = 2 * n_qk
    gates = g_ref[0]
    gates_t = gates.T
    betas = [[gates[ci * c:(ci + 1) * c, 2 * g + hh:2 * g + hh + 1] for hh in range(2)] for ci, g in units]
    gcs = [[gates[ci * c:(ci + 1) * c, n_v + 2 * g + hh:n_v + 2 * g + hh + 1] for hh in range(2)] for ci, g in units]

    def packed_row(ci, g):
        w0 = (ci // 2) * 2 * c
        r0, r1 = (gates_t[n_v + 2 * g + hh:n_v + 2 * g + hh + 1, w0:w0 + 2 * c] for hh in range(2))
        if ci % 2 == 0:
            return jnp.where(head0[0:1], r0, pltpu.roll(r1, c, axis=1))
        return jnp.where(head0[0:1], pltpu.roll(r0, c, axis=1), r1)

    grps = [packed_row(ci, g) for ci, g in units]
    egs = [[jnp.exp(g[hh]) for hh in range(2)] for g in gcs]
    glast = [[g[hh][c - 1:c, :] for hh in range(2)] for g in gcs]
    both = each(lambda q, k: _dot_nt(jnp.concatenate([q, k], axis=0), jnp.concatenate([k, k], axis=0)), qcs, kcs)
    decayp = each(lambda g, gr: jnp.exp(jnp.where(causal, jnp.where(head0, g[0], g[1]) - gr, -jnp.inf)), gcs, grps)
    ap = each(lambda b, bo, dc: jnp.where(strict, jnp.where(head0, b[0], b[1]) * bo[c:] * dc, 0.0), betas, both, decayp)
    qkd = each(lambda bo, dc: bo[:c] * dc, both, decayp)
    d = each(lambda a: jnp.where(in_block, a, 0.0), ap)
    e = each(lambda a, dd: a - dd, ap, d)
    n1 = each(lambda dd: -dd, d)
    p = each(lambda n: eye + n, n1)
    n2 = each(mm, n1, n1)
    p = each(lambda pp, n: pp + mm(pp, n), p, n2)
    n4 = each(mm, n2, n2)
    p = each(lambda pp, n: pp + mm(pp, n), p, n4)
    n8 = each(mm, n4, n4)
    dinv = each(lambda pp, n: pp + mm(pp, n), p, n8)
    nb = each(mm, dinv, e)
    nb2 = each(mm, nb, nb)
    qq = each(lambda n: eye - n, nb)
    qq = each(lambda x, n: x + mm(x, n), qq, nb2)
    tinvp = each(mm, qq, dinv)
    rhs = [[jnp.concatenate([vn[ci * c:(ci + 1) * c, (2 * g + hh) * hd:(2 * g + hh + 1) * hd] * betas[u][hh],
                             kcs[u] * (betas[u][hh] * egs[u][hh])], axis=1) for hh in range(2)]
           for u, (ci, g) in enumerate(units)]
    uw = each(lambda t, r: _dot_bf16(t, _block_diag2(r[0], r[1])), tinvp, rhs)
    wu = [[jnp.concatenate([x[:, (2 * hh + 1) * hd:(2 * hh + 2) * hd], x[:, 2 * hh * hd:(2 * hh + 1) * hd]], axis=1)
           for hh in range(2)] for x in uw]
    qo = each(lambda x, w2: _dot_bf16(x, _block_diag2(w2[0], w2[1])), qkd, wu)
    pre = {}
    for u, (ci, g) in enumerate(units):
        for hh in range(2):
            rb = _dot_tn(kcs[u] * jnp.exp(glast[u][hh] - gcs[u][hh]), wu[u][hh])
            q_eff = qcs[u] * egs[u][hh] - qo[u][:, 2 * hh * hd:(2 * hh + 1) * hd]
            rq = jnp.concatenate([rb[:, :hd], q_eff], axis=0).astype(BF16)
            pre[ci, 2 * g + hh] = (rq, rb[:, hd:], qo[u][:, (2 * hh + 1) * hd:(2 * hh + 2) * hd], jnp.exp(glast[u][hh]))

    states = [state[vh] for vh in range(n_v)]
    for ci in range(n_chunks):
        r0 = ci * c
        for vh in range(n_v):
            rq, b_c, o_c, cd = pre[ci, vh]
            s = states[vh]
            res = jnp.dot(rq, s.astype(BF16), preferred_element_type=F32)
            states[vh] = s * cd - res[:hd] + b_c
            out = res[hd:] + o_c
            out = out * lax.rsqrt(jnp.mean(out * out, axis=-1, keepdims=True) + RMS_EPS) * norm_w
            zc = z_ref[r0:r0 + c, vh * hd:(vh + 1) * hd].astype(F32)
            out = out * (zc * jax.nn.sigmoid(zc))
            o_ref[r0:r0 + c, vh * hd:(vh + 1) * hd] = out.astype(o_ref.dtype)
    for vh in range(n_v):
        state[vh] = states[vh]


def gdn_delta(proj, conv_w, gates, norm_w, batch, seq, tb=GDN_TB, n_qk=GDN_QK_PER_STEP):
    t = proj.shape[0]
    tb = min(tb, seq)
    nblk = seq // tb
    hd = HEAD_DIM
    qw = n_qk * hd
    vw = 2 * qw
    kq = GDN_QK_DIM // qw
    v0 = (2 * GDN_QK_DIM) // vw
    z0 = GDN_CONV_DIM // vw
    return pl.pallas_call(
        _gdn_kernel,
        out_shape=jax.ShapeDtypeStruct((t, GDN_V_DIM), BF16),
        grid=(batch, GDN_QK_HEADS // n_qk, nblk),
        in_specs=[
            pl.BlockSpec((tb, qw), lambda b, h, i: (b * nblk + i, h)),
            pl.BlockSpec((tb, qw), lambda b, h, i: (b * nblk + i, kq + h)),
            pl.BlockSpec((tb, vw), lambda b, h, i: (b * nblk + i, v0 + h)),
            pl.BlockSpec((tb, vw), lambda b, h, i: (b * nblk + i, z0 + h)),
            pl.BlockSpec((GDN_CONV, qw), lambda b, h, i: (0, h)),
            pl.BlockSpec((GDN_CONV, qw), lambda b, h, i: (0, kq + h)),
            pl.BlockSpec((GDN_CONV, vw), lambda b, h, i: (0, v0 + h)),
            pl.BlockSpec((1, tb, LANES), lambda b, h, i: (h, b * nblk + i, 0)),
            pl.BlockSpec((1, hd), lambda b, h, i: (0, 0)),
        ],
        out_specs=pl.BlockSpec((tb, vw), lambda b, h, i: (b * nblk + i, h)),
        scratch_shapes=[
            pltpu.VMEM((tb + SUBLANES, qw), F32),
            pltpu.VMEM((tb + SUBLANES, qw), F32),
            pltpu.VMEM((tb + SUBLANES, vw), F32),
            pltpu.VMEM((tb, qw), F32),
            pltpu.VMEM((tb, qw), F32),
            pltpu.VMEM((tb, vw), F32),
            pltpu.VMEM((2 * n_qk, hd, hd), F32),
        ],
        compiler_params=_params("parallel", "parallel", "arbitrary"),
        name="gdn_delta",
    )(proj, proj, proj, proj, conv_w, conv_w, conv_w, gates, norm_w.reshape(1, hd))


def gated_deltanet_layer(h32, h16, w_in, conv_w, a_log, dt_bias, norm_w, w_out, gain, bias, batch, seq):
    main = GDN_CONV_DIM + GDN_V_DIM
    proj = matmul(h16, w_in[:, :main].astype(BF16))
    gates = gdn_gates(h32, w_in[:, main:], a_log, dt_bias)
    o = gdn_delta(proj, conv_w, gates, norm_w, batch, seq)
    return proj_residual_ln(o, w_out.astype(BF16), h32, gain, bias)


def _dil_attn_kernel(q_ref, kp_ref, ko_ref, vp_ref, vo_ref, bias_ref, o_ref, lse_ref):
    lb = pl.program_id(2)
    blk = DIL_BLOCK
    qi = lax.broadcasted_iota(jnp.int32, (blk, blk), 0)
    kj = lax.broadcasted_iota(jnp.int32, (blk, blk), 1)
    mask_prev = (qi <= kj) & (lb > 0)
    mask_own = kj <= qi
    lane = lax.broadcasted_iota(jnp.int32, (blk, LANES), 1)
    lse_all = jnp.zeros((blk, LANES), F32)
    scale = HEAD_DIM ** -0.5
    heads = range(DIL_HEADS)
    sls = [slice(hd * HEAD_DIM, (hd + 1) * HEAD_DIM) for hd in heads]
    s_prev = [jnp.where(mask_prev, _dot_nt(q_ref[0, :, sl], kp_ref[0, :, sl]) * scale + bias_ref[hd][:, :blk], -jnp.inf)
              for hd, sl in zip(heads, sls)]
    s_own = [jnp.where(mask_own, _dot_nt(q_ref[0, :, sl], ko_ref[0, :, sl]) * scale + bias_ref[hd][:, blk:], -jnp.inf)
             for hd, sl in zip(heads, sls)]
    m = [jnp.maximum(jnp.max(sp, axis=-1, keepdims=True), jnp.max(so, axis=-1, keepdims=True))
         for sp, so in zip(s_prev, s_own)]
    p_prev = [jnp.exp(sp - mm) for sp, mm in zip(s_prev, m)]
    p_own = [jnp.exp(so - mm) for so, mm in zip(s_own, m)]
    den = [jnp.sum(pp, axis=-1, keepdims=True) + jnp.sum(po, axis=-1, keepdims=True) for pp, po in zip(p_prev, p_own)]
    o = [_dot_bf16(pp, vp_ref[0, :, sl]) + _dot_bf16(po, vo_ref[0, :, sl]) for pp, po, sl in zip(p_prev, p_own, sls)]
    for hd, sl in zip(heads, sls):
        o_ref[0, :, sl] = o[hd] / den[hd]
        lse_all = jnp.where(lane == hd, m[hd] + jnp.log(den[hd]), lse_all)
    lse_ref[0] = lse_all


def dilated_group_attention(q, k, v, bias, dil, batch, seq):
    length = seq // dil
    nb = length // DIL_BLOCK
    kvd = DIL_KV_DIM
    qv = q.reshape(batch, length, dil * kvd)
    kview = k.reshape(batch, length, dil * kvd)
    vview = v.reshape(batch, length, dil * kvd)
    blk = DIL_BLOCK
    own = lambda b, r, i: (b, i, r)
    prev = lambda b, r, i: (b, jnp.maximum(i - 1, 0), r)
    o, lse = pl.pallas_call(
        _dil_attn_kernel,
        out_shape=(jax.ShapeDtypeStruct((batch, length, dil * kvd), F32),
                   jax.ShapeDtypeStruct((batch, length, dil * LANES), F32)),
        grid=(batch, dil, nb),
        in_specs=[
            pl.BlockSpec((1, blk, kvd), own),
            pl.BlockSpec((1, blk, kvd), prev),
            pl.BlockSpec((1, blk, kvd), own),
            pl.BlockSpec((1, blk, kvd), prev),
            pl.BlockSpec((1, blk, kvd), own),
            pl.BlockSpec((DIL_HEADS, blk, 2 * blk), lambda b, r, i: (0, 0, 0)),
        ],
        out_specs=(pl.BlockSpec((1, blk, kvd), own), pl.BlockSpec((1, blk, LANES), own)),
        compiler_params=_params("parallel", "parallel", "parallel"),
        name=f"dilated_attention_d{dil}",
    )(qv, kview, kview, vview, vview, bias)
    return o.reshape(batch * seq, kvd), lse.reshape(batch * seq, LANES)


def _dil_out_kernel(o0_ref, o1_ref, o2_ref, l0_ref, l1_ref, l2_ref, w_ref, h_ref, gain_ref, bias_ref,
                    o32_ref, o16_ref, mix_ref):
    l0, l1, l2 = l0_ref[...], l1_ref[...], l2_ref[...]
    m = jnp.maximum(jnp.maximum(l0, l1), l2)
    e0, e1, e2 = jnp.exp(l0 - m), jnp.exp(l1 - m), jnp.exp(l2 - m)
    inv = 1.0 / (e0 + e1 + e2)
    w0, w1, w2 = e0 * inv, e1 * inv, e2 * inv
    for hd in range(DIL_HEADS):
        sl = slice(hd * HEAD_DIM, (hd + 1) * HEAD_DIM)
        mix_ref[:, sl] = (w0[:, hd:hd + 1] * o0_ref[:, sl] + w1[:, hd:hd + 1] * o1_ref[:, sl]
                          + w2[:, hd:hd + 1] * o2_ref[:, sl])
    mix = jnp.dot(mix_ref[...].astype(BF16), w_ref[...], preferred_element_type=F32)
    y = _layer_norm_rows(DEEPNORM_ALPHA * h_ref[...] + mix, gain_ref[...], bias_ref[...])
    o32_ref[...] = y
    o16_ref[...] = y.astype(BF16)


def dilated_out(outs, lses, w_o, h, gain, bias, tm=LN_TM):
    m, kvd = outs[0].shape
    d = w_o.shape[1]
    tm = min(tm, m)
    row = lambda i: (i, 0)
    fixed = lambda i: (0, 0)
    return pl.pallas_call(
        _dil_out_kernel,
        out_shape=(jax.ShapeDtypeStruct((m, d), F32), jax.ShapeDtypeStruct((m, d), BF16)),
        grid=(m // tm,),
        in_specs=[pl.BlockSpec((tm, kvd), row)] * 3 + [pl.BlockSpec((tm, LANES), row)] * 3 + [
            pl.BlockSpec((kvd, d), fixed), pl.BlockSpec((tm, d), row),
            pl.BlockSpec((1, d), fixed), pl.BlockSpec((1, d), fixed)],
        out_specs=(pl.BlockSpec((tm, d), row), pl.BlockSpec((tm, d), row)),
        scratch_shapes=[pltpu.VMEM((tm, kvd), F32)],
        compiler_params=_params("parallel"),
        name="dilated_out_ln",
    )(*outs, *lses, w_o, h, gain.reshape(1, d), bias.reshape(1, d))


def _t5_bucket(dist):
    exact = REL_BUCKETS // 2
    distf = jnp.maximum(dist, exact).astype(F32)
    large = exact + (jnp.log(distf / exact) / math.log(REL_MAX_DIST / exact)
                     * (REL_BUCKETS - exact)).astype(jnp.int32)
    return jnp.where(dist < exact, dist, jnp.minimum(large, REL_BUCKETS - 1))


def _band_bias(rel_bias_g, dil):
    qi = jnp.arange(DIL_BLOCK)[:, None]
    kj = jnp.arange(2 * DIL_BLOCK)[None, :]
    steps = DIL_BLOCK + qi - kj
    bias = rel_bias_g[_t5_bucket(jnp.maximum(steps, 0) * dil)]
    return jnp.transpose(bias, (2, 0, 1)).astype(F32)


def _router_kernel(h_ref, w_ref, b_ref, o_ref, cnt_ref, running):
    @pl.when(pl.program_id(0) == 0)
    def _():
        running[...] = jnp.zeros(running.shape, F32)

    logits = _dot_split(h_ref[...], w_ref) + b_ref[...]
    lane = lax.broadcasted_iota(jnp.int32, logits.shape, 1)
    neg = -jnp.inf
    big = jnp.int32(LANES)
    is_group = lane < MOE_GROUPS
    gl = jnp.where(is_group, logits, neg)
    gmax = jnp.max(gl, axis=-1, keepdims=True)
    gidx = jnp.min(jnp.where(gl == gmax, lane, big), axis=-1, keepdims=True)
    gden = jnp.sum(jnp.exp(gl - gmax), axis=-1, keepdims=True)
    group_gate = 1.0 / gden
    lo = MOE_GROUPS + gidx * MOE_EXPERTS_PER_GROUP
    in_group = (lane >= lo) & (lane < lo + MOE_EXPERTS_PER_GROUP)
    ll = jnp.where(in_group, logits, neg)
    m1 = jnp.max(ll, axis=-1, keepdims=True)
    i1 = jnp.min(jnp.where(ll == m1, lane, big), axis=-1, keepdims=True)
    rest = jnp.where(lane == i1, neg, ll)
    m2 = jnp.max(rest, axis=-1, keepdims=True)
    i2 = jnp.min(jnp.where(rest == m2, lane, big), axis=-1, keepdims=True)
    den = jnp.sum(jnp.exp(ll - m1), axis=-1, keepdims=True)
    p1 = 1.0 / den
    p2 = jnp.exp(m2 - m1) / den
    g1 = group_gate * p1 / (p1 + p2)
    g2 = group_gate * p2 / (p1 + p2)
    id1 = (i1 - MOE_GROUPS).astype(F32)
    id2 = (i2 - MOE_GROUPS).astype(F32)
    tm = logits.shape[0]
    oh1 = (lane == i1).astype(F32)
    oh2 = (lane == i2).astype(F32)
    both = oh1 + oh2
    r_i = lax.broadcasted_iota(jnp.int32, (tm, tm), 0)
    c_i = lax.broadcasted_iota(jnp.int32, (tm, tm), 1)
    before = jnp.dot((c_i < r_i).astype(BF16), both.astype(BF16), preferred_element_type=F32) + running[...]
    rank1 = jnp.sum(before * oh1, axis=-1, keepdims=True)
    rank2 = jnp.sum(before * oh2, axis=-1, keepdims=True)
    running[...] = running[...] + jnp.sum(both, axis=0, keepdims=True)
    cnt_ref[...] = running[...]
    o_ref[...] = jnp.where(lane == 0, id1, jnp.where(lane == 1, id2, jnp.where(lane == 2, g1, jnp.where(
        lane == 3, g2, jnp.where(lane == 4, rank1, rank2)))))


def moe_router(h, w_group, b_group, w_expert, b_expert, tm=ROUTER_TM):
    m, d = h.shape
    tm = min(tm, m)
    pad = LANES - MOE_GROUPS - MOE_EXPERTS
    w = jnp.pad(jnp.concatenate([w_group, w_expert], axis=1), ((0, 0), (0, pad)))
    b = jnp.pad(jnp.concatenate([b_group, b_expert]), (0, pad)).reshape(1, LANES)
    out, cnt = pl.pallas_call(
        _router_kernel,
        out_shape=(jax.ShapeDtypeStruct((m, LANES), F32), jax.ShapeDtypeStruct((1, LANES), F32)),
        grid=(m // tm,),
        in_specs=[pl.BlockSpec((tm, d), lambda i: (i, 0)),
                  pl.BlockSpec((d, 2 * LANES), lambda i: (0, 0)),
                  pl.BlockSpec((1, LANES), lambda i: (0, 0))],
        out_specs=(pl.BlockSpec((tm, LANES), lambda i: (i, 0)), pl.BlockSpec((1, LANES), lambda i: (0, 0))),
        scratch_shapes=[pltpu.VMEM((1, LANES), F32)],
        compiler_params=_params("arbitrary"),
        name="moe_router",
    )(h, _hi_lo_columns(w), b)
    k = MOE_TOP_K
    counts = cnt[0, MOE_GROUPS:MOE_GROUPS + MOE_EXPERTS].astype(jnp.int32)
    return out[:, :k].astype(jnp.int32), out[:, k:2 * k], out[:, 2 * k:3 * k].astype(jnp.int32), counts


def _gather_rows_start(src_hbm, idx_ref, idx_base, buf, slot, sem):
    def body(g, carry):
        for j in range(SUBLANES):
            pltpu.make_async_copy(src_hbm.at[pl.ds(idx_ref[idx_base + g * SUBLANES + j], 1)],
                                  buf.at[slot, g, pl.ds(j, 1)], sem.at[slot]).start()
        return carry
    lax.fori_loop(0, buf.shape[1], body, 0)


def _gather_rows_wait(buf, slot, sem):
    pltpu.make_async_copy(buf.at[slot], buf.at[slot], sem.at[slot]).wait()


def _expert_kernel(bexp_ref, rtok_ref, nused_ref, h_hbm, w1_ref, w3_ref, w2_ref, y_ref, xbuf, sem, wb1, wb3, wb2):
    i = pl.program_id(0)
    tm = xbuf.shape[1] * xbuf.shape[2]
    n_used = nused_ref[0]
    slot = i % 2

    def issue(block, to_slot):
        _gather_rows_start(h_hbm, rtok_ref, block * tm, xbuf, to_slot, sem)

    @pl.when(i == 0)
    def _():
        issue(0, 0)

    @pl.when(i + 1 < n_used)
    def _():
        issue(i + 1, 1 - slot)

    @pl.when(i < n_used)
    def _():
        @pl.when((i == 0) | (bexp_ref[i] != bexp_ref[jnp.maximum(i - 1, 0)]))
        def _():
            wb1[...] = w1_ref[0, 0].astype(BF16)
            wb3[...] = w3_ref[0, 0].astype(BF16)
            wb2[...] = w2_ref[0, 0].astype(BF16)

        _gather_rows_wait(xbuf, slot, sem)
        x = xbuf[slot].reshape(tm, xbuf.shape[3]).astype(BF16)
        h1 = jnp.dot(x, wb1[...], preferred_element_type=F32)
        h3 = jnp.dot(x, wb3[...], preferred_element_type=F32)
        hmid = (h1 * jax.nn.sigmoid(h1) * h3).astype(BF16)
        y_ref[...] = jnp.dot(hmid, wb2[...], preferred_element_type=F32)

    @pl.when(i >= n_used)
    def _():
        y_ref[...] = jnp.zeros(y_ref.shape, y_ref.dtype)


def moe_experts(h, row_token, block_expert, n_used, w1, w3, w2, layer, tm=MOE_TM):
    n_rows = row_token.shape[0]
    n_blocks = n_rows // tm
    d = h.shape[1]
    ff = w1.shape[3]
    grid_spec = pltpu.PrefetchScalarGridSpec(
        num_scalar_prefetch=3,
        grid=(n_blocks,),
        in_specs=[
            pl.BlockSpec(memory_space=pl.ANY),
            pl.BlockSpec((1, 1, d, ff), lambda i, be, rt, nu: (layer, be[i], 0, 0)),
            pl.BlockSpec((1, 1, d, ff), lambda i, be, rt, nu: (layer, be[i], 0, 0)),
            pl.BlockSpec((1, 1, ff, d), lambda i, be, rt, nu: (layer, be[i], 0, 0)),
        ],
        out_specs=pl.BlockSpec((tm, d), lambda i, be, rt, nu: (i, 0)),
        scratch_shapes=[pltpu.VMEM((2, tm // SUBLANES, SUBLANES, d), F32), pltpu.SemaphoreType.DMA((2,)),
                        pltpu.VMEM((d, ff), BF16), pltpu.VMEM((d, ff), BF16), pltpu.VMEM((ff, d), BF16)],
    )
    return pl.pallas_call(
        _expert_kernel,
        out_shape=jax.ShapeDtypeStruct((n_rows, d), F32),
        grid_spec=grid_spec,
        compiler_params=_params("arbitrary"),
        name="moe_experts",
    )(block_expert, row_token, n_used, h, w1, w3, w2)


def _combine_kernel(dest_ref, y_hbm, gate_ref, h_ref, gain_ref, bias_ref, o32_ref, o16_ref, ybuf, sem):
    i = pl.program_id(0)
    n = pl.num_programs(0)
    tm = h_ref.shape[0]
    rows = MOE_TOP_K * tm
    slot = i % 2

    def issue(block, to_slot):
        _gather_rows_start(y_hbm, dest_ref, block * rows, ybuf, to_slot, sem)

    @pl.when(i == 0)
    def _():
        issue(0, 0)

    @pl.when(i + 1 < n)
    def _():
        issue(i + 1, 1 - slot)

    _gather_rows_wait(ybuf, slot, sem)
    gates = gate_ref[...]
    yy = ybuf[slot].reshape(rows, ybuf.shape[3])
    ffn = yy[0:tm, :] * gates[:, 0:1] + yy[tm:rows, :] * gates[:, 1:2]
    y = _layer_norm_rows(DEEPNORM_ALPHA * h_ref[...] + ffn, gain_ref[...], bias_ref[...])
    o32_ref[...] = y
    o16_ref[...] = y.astype(BF16)


def moe_combine_ln(y_rows, dest, gates, h, gain, bias, tm=COMBINE_TM):
    m, d = h.shape
    tm = min(tm, m)
    grid_spec = pltpu.PrefetchScalarGridSpec(
        num_scalar_prefetch=1,
        grid=(m // tm,),
        in_specs=[
            pl.BlockSpec(memory_space=pl.ANY),
            pl.BlockSpec((tm, MOE_TOP_K), lambda i, de: (i, 0)),
            pl.BlockSpec((tm, d), lambda i, de: (i, 0)),
            pl.BlockSpec((1, d), lambda i, de: (0, 0)),
            pl.BlockSpec((1, d), lambda i, de: (0, 0)),
        ],
        out_specs=(pl.BlockSpec((tm, d), lambda i, de: (i, 0)),
                   pl.BlockSpec((tm, d), lambda i, de: (i, 0))),
        scratch_shapes=[pltpu.VMEM((2, MOE_TOP_K * tm // SUBLANES, SUBLANES, d), F32), pltpu.SemaphoreType.DMA((2,))],
    )
    return pl.pallas_call(
        _combine_kernel,
        out_shape=(jax.ShapeDtypeStruct((m, d), F32), jax.ShapeDtypeStruct((m, d), BF16)),
        grid_spec=grid_spec,
        compiler_params=_params("arbitrary"),
        name="moe_combine_ln",
    )(dest.reshape(-1), y_rows, gates, h, gain.reshape(1, d), bias.reshape(1, d))


def moe_layer(h32, w_group, b_group, w_expert, b_expert, w1, w3, w2, layer, gain, bias, tm=MOE_TM):
    t = h32.shape[0]
    ids, gates, ranks, counts = moe_router(h32, w_group, b_group, w_expert, b_expert)
    eid = ids.reshape(-1)
    n_pairs = eid.shape[0]
    padded = (counts + tm - 1) // tm * tm
    seg_end = jnp.cumsum(padded)
    seg_start = seg_end - padded
    onehot = eid[:, None] == jnp.arange(MOE_EXPERTS, dtype=jnp.int32)[None, :]
    dest = (jnp.sum(jnp.where(onehot, seg_start[None, :], 0), axis=1) + ranks.reshape(-1)).astype(jnp.int32)
    n_blocks = -(-n_pairs // tm) + MOE_EXPERTS
    n_rows = n_blocks * tm
    token_of_pair = (jnp.arange(n_pairs, dtype=jnp.int32) // MOE_TOP_K)
    row_token = jnp.zeros((n_rows,), jnp.int32).at[dest].set(token_of_pair)
    block_start = jnp.arange(n_blocks, dtype=jnp.int32) * tm
    block_expert = jnp.minimum(jnp.sum((seg_end[None, :] <= block_start[:, None]).astype(jnp.int32), axis=1),
                               MOE_EXPERTS - 1)
    n_used = (seg_end[-1:] // tm).astype(jnp.int32)
    y_rows = moe_experts(h32, row_token, block_expert, n_used, w1, w3, w2, layer, tm=tm)
    ctm = min(COMBINE_TM, t)
    dest_tiles = jnp.transpose(dest.reshape(t // ctm, ctm, MOE_TOP_K), (0, 2, 1))
    return moe_combine_ln(y_rows, dest_tiles, gates, h32, gain, bias, tm=ctm)


def kernel(x, gdn_w_in, gdn_conv, gdn_a_log, gdn_dt_bias, gdn_norm, gdn_w_out, kv_w_k, kv_w_v, dil_w_q, dil_w_o,
           rel_bias, ln_gain, ln_bias, moe_w_group, moe_b_group, moe_w_expert, moe_b_expert, moe_w1, moe_w3, moe_w2):
    batch, seq, d = x.shape
    t = batch * seq
    h32 = x.reshape(t, d)
    h16 = h32.astype(BF16)
    k = v = None
    biases = None
    for layer in range(DEPTH):
        if layer < N_A_LAYERS:
            h32, h16 = gated_deltanet_layer(
                h32, h16, gdn_w_in[layer], gdn_conv[layer], gdn_a_log[layer], gdn_dt_bias[layer],
                gdn_norm[layer], gdn_w_out[layer], ln_gain[layer, 0], ln_bias[layer, 0], batch, seq)
        else:
            if layer == N_A_LAYERS:
                k = matmul(h16, kv_w_k.astype(BF16), out_dtype=BF16)
                v = matmul(h16, kv_w_v.astype(BF16), out_dtype=BF16)
                biases = [_band_bias(rel_bias[:, gi * DIL_HEADS:(gi + 1) * DIL_HEADS], dil)
                          for gi, (_, dil) in enumerate(DIL_CONFIGS)]
            j = layer - N_A_LAYERS
            outs, lses = [], []
            for gi, (_, dil) in enumerate(DIL_CONFIGS):
                q = matmul(h16, dil_w_q[j][:, gi * DIL_KV_DIM:(gi + 1) * DIL_KV_DIM].astype(BF16), out_dtype=BF16)
                o, lse = dilated_group_attention(q, k, v, biases[gi], dil, batch, seq)
                outs.append(o)
                lses.append(lse)
            h32, h16 = dilated_out(outs, lses, dil_w_o[j].astype(BF16), h32, ln_gain[layer, 0], ln_bias[layer, 0])
        h32, h16 = moe_layer(h32, moe_w_group[layer], moe_b_group[layer], moe_w_expert[layer], moe_b_expert[layer],
                             moe_w1, moe_w3, moe_w2, layer, ln_gain[layer, 1], ln_bias[layer, 1])
    return h32.reshape(batch, seq, d)
```

```python
import functools
import math

import jax
import jax.numpy as jnp
from jax import lax
from jax.experimental import pallas as pl
from jax.experimental.pallas import tpu as pltpu

F32 = jnp.float32
BF16 = jnp.bfloat16

DEPTH = 4
N_A_LAYERS = DEPTH // 2
HEAD_DIM = 128
GDN_QK_HEADS = 16
GDN_V_HEADS = 32
GDN_CONV = 4
GDN_CHUNK = 64
GDN_QK_DIM = GDN_QK_HEADS * HEAD_DIM
GDN_V_DIM = GDN_V_HEADS * HEAD_DIM
GDN_CONV_DIM = 2 * GDN_QK_DIM + GDN_V_DIM
DIL_CONFIGS = ((128, 1), (512, 4), (2048, 16))
N_DIL = len(DIL_CONFIGS)
DIL_HEADS = 8
DIL_BLOCK = 128
DIL_KV_DIM = DIL_HEADS * HEAD_DIM
REL_BUCKETS = 32
REL_MAX_DIST = 2048
MOE_GROUPS = 4
MOE_EXPERTS_PER_GROUP = 8
MOE_EXPERTS = MOE_GROUPS * MOE_EXPERTS_PER_GROUP
MOE_TOP_K = 2
DEEPNORM_ALPHA = (2 * DEPTH) ** 0.25
LN_EPS = 1e-5
RMS_EPS = 1e-6
L2_EPS = 1e-6

LANES = 128
SUBLANES = 8
VMEM_LIMIT_BYTES = 56 * 1024 * 1024

MM_TM = 512
MM_TN = 1024
LN_TM = 256
GATE_TM = 512
GDN_TB = 512
GDN_QK_PER_STEP = 4
MOE_TM = 256
ROUTER_TM = 512
COMBINE_TM = 256


def _params(*semantics):
    return pltpu.CompilerParams(dimension_semantics=semantics, vmem_limit_bytes=VMEM_LIMIT_BYTES)


def _split_bf16(x):
    hi = x.astype(BF16)
    return hi, (x - hi.astype(F32)).astype(BF16)


def _dot_split(x, w2_ref):
    n = w2_ref.shape[1] // 2
    hi, lo = _split_bf16(x)
    a = jnp.dot(hi, w2_ref[...], preferred_element_type=F32)
    b = jnp.dot(lo, w2_ref[:, :n], preferred_element_type=F32)
    return a[:, :n] + a[:, n:] + b


def _hi_lo_columns(w):
    hi = w.astype(BF16)
    return jnp.concatenate([hi, (w - hi.astype(F32)).astype(BF16)], axis=1)


def _layer_norm_rows(y, gain, bias):
    mu = jnp.mean(y, axis=-1, keepdims=True)
    yc = y - mu
    var = jnp.mean(yc * yc, axis=-1, keepdims=True)
    return yc * lax.rsqrt(var + LN_EPS) * gain + bias


def _matmul_kernel(a_ref, b_ref, o_ref):
    o_ref[...] = jnp.dot(a_ref[...].astype(BF16), b_ref[...],
                         preferred_element_type=F32).astype(o_ref.dtype)


def matmul(a, b, out_dtype=F32, tm=MM_TM, tn=MM_TN):
    m, k = a.shape
    n = b.shape[1]
    tm, tn = min(tm, m), min(tn, n)
    assert m % tm == 0 and n % tn == 0
    return pl.pallas_call(
        _matmul_kernel,
        out_shape=jax.ShapeDtypeStruct((m, n), out_dtype),
        grid=(n // tn, m // tm),
        in_specs=[pl.BlockSpec((tm, k), lambda j, i: (i, 0)),
                  pl.BlockSpec((k, tn), lambda j, i: (0, j))],
        out_specs=pl.BlockSpec((tm, tn), lambda j, i: (i, j)),
        compiler_params=_params("parallel", "parallel"),
        name="matmul",
    )(a, b)


def _proj_ln_kernel(a_ref, w_ref, h_ref, gain_ref, bias_ref, o32_ref, o16_ref):
    mix = jnp.dot(a_ref[...].astype(BF16), w_ref[...], preferred_element_type=F32)
    y = _layer_norm_rows(DEEPNORM_ALPHA * h_ref[...] + mix, gain_ref[...], bias_ref[...])
    o32_ref[...] = y
    o16_ref[...] = y.astype(BF16)


def proj_residual_ln(a, w, h, gain, bias, tm=LN_TM):
    m, k = a.shape
    d = w.shape[1]
    tm = min(tm, m)
    assert m % tm == 0
    return pl.pallas_call(
        _proj_ln_kernel,
        out_shape=(jax.ShapeDtypeStruct((m, d), F32), jax.ShapeDtypeStruct((m, d), BF16)),
        grid=(m // tm,),
        in_specs=[pl.BlockSpec((tm, k), lambda i: (i, 0)),
                  pl.BlockSpec((k, d), lambda i: (0, 0)),
                  pl.BlockSpec((tm, d), lambda i: (i, 0)),
                  pl.BlockSpec((1, d), lambda i: (0, 0)),
                  pl.BlockSpec((1, d), lambda i: (0, 0))],
        out_specs=(pl.BlockSpec((tm, d), lambda i: (i, 0)),
                   pl.BlockSpec((tm, d), lambda i: (i, 0))),
        compiler_params=_params("parallel"),
        name="proj_residual_ln",
    )(a, w, h, gain.reshape(1, d), bias.reshape(1, d))


def _gdn_gate_kernel(h_ref, w_ref, alog_ref, dt_ref, o_ref):
    tm = h_ref.shape[0]
    n_groups = o_ref.shape[0]
    width = 2 * GDN_V_HEADS // n_groups
    ba = _dot_split(h_ref[...], w_ref)
    xa = ba + dt_ref[...]
    softplus = jnp.maximum(xa, 0.0) + jnp.log(1.0 + jnp.exp(-jnp.abs(xa)))
    g = -jnp.exp(alog_ref[...]) * softplus
    row = lax.broadcasted_iota(jnp.int32, (tm, tm), 0)
    col = lax.broadcasted_iota(jnp.int32, (tm, tm), 1)
    tri = ((row // GDN_CHUNK == col // GDN_CHUNK) & (col <= row)).astype(BF16)
    g_hi, g_lo = _split_bf16(g)
    gc2 = jnp.dot(tri, jnp.concatenate([g_hi, g_lo], axis=1), preferred_element_type=F32)
    gc = gc2[:, :LANES] + gc2[:, LANES:]
    lane = lax.broadcasted_iota(jnp.int32, ba.shape, 1)
    slab = jnp.where(lane % width < width // 2, jax.nn.sigmoid(ba), gc)
    for r in range(n_groups):
        o_ref[r] = slab if r == 0 else pltpu.roll(slab, LANES - width * r, axis=1)


def gdn_gates(h, w_ba, a_log, dt_bias, n_qk=GDN_QK_PER_STEP, tm=GATE_TM):
    m, d = h.shape
    tm = min(tm, m)
    v = 2 * n_qk
    n_groups = GDN_V_HEADS // v
    cols = [(v * (l // (2 * v)) + l % (2 * v)) if l % (2 * v) < v else (GDN_V_HEADS + v * (l // (2 * v)) + l % (2 * v) - v)
            for l in range(2 * GDN_V_HEADS)]
    is_decay = jnp.array([l % (2 * v) >= v for l in range(2 * GDN_V_HEADS)])
    head = jnp.array([c % GDN_V_HEADS for c in cols], jnp.int32)
    pad = LANES - 2 * GDN_V_HEADS
    w = jnp.pad(w_ba[:, jnp.array(cols, jnp.int32)], ((0, 0), (0, pad)))
    alog = jnp.pad(jnp.where(is_decay, a_log[head], 0.0), (0, pad)).reshape(1, LANES)
    dt = jnp.pad(jnp.where(is_decay, dt_bias[head], 0.0), (0, pad)).reshape(1, LANES)
    return pl.pallas_call(
        _gdn_gate_kernel,
        out_shape=jax.ShapeDtypeStruct((n_groups, m, LANES), F32),
        grid=(m // tm,),
        in_specs=[pl.BlockSpec((tm, d), lambda i: (i, 0)),
                  pl.BlockSpec((d, 2 * LANES), lambda i: (0, 0)),
                  pl.BlockSpec((1, LANES), lambda i: (0, 0)),
                  pl.BlockSpec((1, LANES), lambda i: (0, 0))],
        out_specs=pl.BlockSpec((n_groups, tm, LANES), lambda i: (0, i, 0)),
        compiler_params=_params("parallel"),
        name="gdn_gates",
    )(h, _hi_lo_columns(w), alog, dt)


def _dot_bf16(a, b):
    return jnp.dot(a.astype(BF16), b.astype(BF16), preferred_element_type=F32)


def _dot_nt(a, b):
    return lax.dot_general(a.astype(BF16), b.astype(BF16), (((1,), (1,)), ((), ())),
                           preferred_element_type=F32)


def _dot_tn(a, b):
    return lax.dot_general(a.astype(BF16), b.astype(BF16), (((0,), (0,)), ((), ())),
                           preferred_element_type=F32)


def _block_diag2(x0, x1):
    z = jnp.zeros_like(x0)
    return jnp.concatenate([jnp.concatenate([x0, z], axis=1), jnp.concatenate([z, x1], axis=1)], axis=0)


def _gdn_kernel(q_ref, k_ref, v_ref, z_ref, wq_ref, wk_ref, wv_ref, g_ref, nw_ref,
                o_ref, qx, kx, vx, qn, kn, vn, state):
    blk = pl.program_id(2)
    tb = q_ref.shape[0]
    n_chunks = tb // GDN_CHUNK
    n_qk = q_ref.shape[1] // HEAD_DIM
    c = GDN_CHUNK
    hd = HEAD_DIM
    tail = SUBLANES

    @pl.when(blk == 0)
    def _():
        qx[0:tail, :] = jnp.zeros((tail, qx.shape[1]), F32)
        kx[0:tail, :] = jnp.zeros((tail, kx.shape[1]), F32)
        vx[0:tail, :] = jnp.zeros((tail, vx.shape[1]), F32)
        state[...] = jnp.zeros(state.shape, F32)

    def conv_silu(x_ref, ext, w_ref):
        ext[tail:tail + tb, :] = x_ref[...].astype(F32)
        y = None
        for j in range(GDN_CONV):
            term = w_ref[j:j + 1, :] * ext[pl.ds(tail - (GDN_CONV - 1) + j, tb), :]
            y = term if y is None else y + term
        ext[0:tail, :] = ext[tb:tb + tail, :]
        return y * jax.nn.sigmoid(y)

    def l2n(x):
        heads = [x[:, g * hd:(g + 1) * hd] for g in range(n_qk)]
        return jnp.concatenate([xh * lax.rsqrt(jnp.sum(xh * xh, axis=-1, keepdims=True) + L2_EPS) for xh in heads], axis=1)

    qn[...] = l2n(conv_silu(q_ref, qx, wq_ref)) * (HEAD_DIM ** -0.5)
    kn[...] = l2n(conv_silu(k_ref, kx, wk_ref))
    vn[...] = conv_silu(v_ref, vx, wv_ref)

    row = lax.broadcasted_iota(jnp.int32, (c, 2 * c), 0)
    lane = lax.broadcasted_iota(jnp.int32, (c, 2 * c), 1)
    col = lane & (c - 1)
    head0 = lane < c
    causal = col <= row
    strict = col < row
    in_block = (row // 16) == (col // 16)
    eye = (row == col).astype(F32)
    norm_w = nw_ref[...]

    def mm(x, y):
        yb = jnp.concatenate([jnp.where(head0, y, 0.0), jnp.where(head0, 0.0, y)], axis=0)
        return _dot_bf16(x, yb)

    units = [(ci, g) for ci in range(n_chunks) for g in range(n_qk)]
    each = lambda f, *lists: [f(*xs) for xs in zip(*lists)]
    qcs = [qn[ci * c:(ci + 1) * c, g * hd:(g + 1) * hd] for ci, g in units]
    kcs = [kn[ci * c:(ci + 1) * c, g * hd:(g + 1) * hd] for ci, g in units]
    n_v = 2 * n_qk
    gates = g_ref[0]
    gates_t = gates.T
    betas = [[gates[ci * c:(ci + 1) * c, 2 * g + hh:2 * g + hh + 1] for hh in range(2)] for ci, g in units]
    gcs = [[gates[ci * c:(ci + 1) * c, n_v + 2 * g + hh:n_v + 2 * g + hh + 1] for hh in range(2)] for ci, g in units]

    def packed_row(ci, g):
        w0 = (ci // 2) * 2 * c
        r0, r1 = (gates_t[n_v + 2 * g + hh:n_v + 2 * g + hh + 1, w0:w0 + 2 * c] for hh in range(2))
        if ci % 2 == 0:
            return jnp.where(head0[0:1], r0, pltpu.roll(r1, c, axis=1))
        return jnp.where(head0[0:1], pltpu.roll(r0, c, axis=1), r1)

    grps = [packed_row(ci, g) for ci, g in units]
    egs = [[jnp.exp(g[hh]) for hh in range(2)] for g in gcs]
    glast = [[g[hh][c - 1:c, :] for hh in range(2)] for g in gcs]
    both = each(lambda q, k: _dot_nt(jnp.concatenate([q, k], axis=0), jnp.concatenate([k, k], axis=0)), qcs, kcs)
    decayp = each(lambda g, gr: jnp.exp(jnp.where(causal, jnp.where(head0, g[0], g[1]) - gr, -jnp.inf)), gcs, grps)
    ap = each(lambda b, bo, dc: jnp.where(strict, jnp.where(head0, b[0], b[1]) * bo[c:] * dc, 0.0), betas, both, decayp)
    qkd = each(lambda bo, dc: bo[:c] * dc, both, decayp)
    d = each(lambda a: jnp.where(in_block, a, 0.0), ap)
    e = each(lambda a, dd: a - dd, ap, d)
    n1 = each(lambda dd: -dd, d)
    p = each(lambda n: eye + n, n1)
    n2 = each(mm, n1, n1)
    p = each(lambda pp, n: pp + mm(pp, n), p, n2)
    n4 = each(mm, n2, n2)
    p = each(lambda pp, n: pp + mm(pp, n), p, n4)
    n8 = each(mm, n4, n4)
    dinv = each(lambda pp, n: pp + mm(pp, n), p, n8)
    nb = each(mm, dinv, e)
    nb2 = each(mm, nb, nb)
    qq = each(lambda n: eye - n, nb)
    qq = each(lambda x, n: x + mm(x, n), qq, nb2)
    tinvp = each(mm, qq, dinv)
    rhs = [[jnp.concatenate([vn[ci * c:(ci + 1) * c, (2 * g + hh) * hd:(2 * g + hh + 1) * hd] * betas[u][hh],
                             kcs[u] * (betas[u][hh] * egs[u][hh])], axis=1) for hh in range(2)]
           for u, (ci, g) in enumerate(units)]
    uw = each(lambda t, r: _dot_bf16(t, _block_diag2(r[0], r[1])), tinvp, rhs)
    wu = [[jnp.concatenate([x[:, (2 * hh + 1) * hd:(2 * hh + 2) * hd], x[:, 2 * hh * hd:(2 * hh + 1) * hd]], axis=1)
           for hh in range(2)] for x in uw]
    qo = each(lambda x, w2: _dot_bf16(x, _block_diag2(w2[0], w2[1])), qkd, wu)
    pre = {}
    for u, (ci, g) in enumerate(units):
        for hh in range(2):
            rb = _dot_tn(kcs[u] * jnp.exp(glast[u][hh] - gcs[u][hh]), wu[u][hh])
            q_eff = qcs[u] * egs[u][hh] - qo[u][:, 2 * hh * hd:(2 * hh + 1) * hd]
            rq = jnp.concatenate([rb[:, :hd], q_eff], axis=0).astype(BF16)
            pre[ci, 2 * g + hh] = (rq, rb[:, hd:], qo[u][:, (2 * hh + 1) * hd:(2 * hh + 2) * hd], jnp.exp(glast[u][hh]))

    states = [state[vh] for vh in range(n_v)]
    for ci in range(n_chunks):
        r0 = ci * c
        for vh in range(n_v):
            rq, b_c, o_c, cd = pre[ci, vh]
            s = states[vh]
            res = jnp.dot(rq, s.astype(BF16), preferred_element_type=F32)
            states[vh] = s * cd - res[:hd] + b_c
            out = res[hd:] + o_c
            out = out * lax.rsqrt(jnp.mean(out * out, axis=-1, keepdims=True) + RMS_EPS) * norm_w
            zc = z_ref[r0:r0 + c, vh * hd:(vh + 1) * hd].astype(F32)
            out = out * (zc * jax.nn.sigmoid(zc))
            o_ref[r0:r0 + c, vh * hd:(vh + 1) * hd] = out.astype(o_ref.dtype)
    for vh in range(n_v):
        state[vh] = states[vh]


def gdn_delta(proj, conv_w, gates, norm_w, batch, seq, tb=GDN_TB, n_qk=GDN_QK_PER_STEP):
    t = proj.shape[0]
    tb = min(tb, seq)
    nblk = seq // tb
    hd = HEAD_DIM
    qw = n_qk * hd
    vw = 2 * qw
    kq = GDN_QK_DIM // qw
    v0 = (2 * GDN_QK_DIM) // vw
    z0 = GDN_CONV_DIM // vw
    return pl.pallas_call(
        _gdn_kernel,
        out_shape=jax.ShapeDtypeStruct((t, GDN_V_DIM), BF16),
        grid=(batch, GDN_QK_HEADS // n_qk, nblk),
        in_specs=[
            pl.BlockSpec((tb, qw), lambda b, h, i: (b * nblk + i, h)),
            pl.BlockSpec((tb, qw), lambda b, h, i: (b * nblk + i, kq + h)),
            pl.BlockSpec((tb, vw), lambda b, h, i: (b * nblk + i, v0 + h)),
            pl.BlockSpec((tb, vw), lambda b, h, i: (b * nblk + i, z0 + h)),
            pl.BlockSpec((GDN_CONV, qw), lambda b, h, i: (0, h)),
            pl.BlockSpec((GDN_CONV, qw), lambda b, h, i: (0, kq + h)),
            pl.BlockSpec((GDN_CONV, vw), lambda b, h, i: (0, v0 + h)),
            pl.BlockSpec((1, tb, LANES), lambda b, h, i: (h, b * nblk + i, 0)),
            pl.BlockSpec((1, hd), lambda b, h, i: (0, 0)),
        ],
        out_specs=pl.BlockSpec((tb, vw), lambda b, h, i: (b * nblk + i, h)),
        scratch_shapes=[
            pltpu.VMEM((tb + SUBLANES, qw), F32),
            pltpu.VMEM((tb + SUBLANES, qw), F32),
            pltpu.VMEM((tb + SUBLANES, vw), F32),
            pltpu.VMEM((tb, qw), F32),
            pltpu.VMEM((tb, qw), F32),
            pltpu.VMEM((tb, vw), F32),
            pltpu.VMEM((2 * n_qk, hd, hd), F32),
        ],
        compiler_params=_params("parallel", "parallel", "arbitrary"),
        name="gdn_delta",
    )(proj, proj, proj, proj, conv_w, conv_w, conv_w, gates, norm_w.reshape(1, hd))


def gated_deltanet_layer(h32, h16, w_in, conv_w, a_log, dt_bias, norm_w, w_out, gain, bias, batch, seq):
    main = GDN_CONV_DIM + GDN_V_DIM
    proj = matmul(h16, w_in[:, :main].astype(BF16))
    gates = gdn_gates(h32, w_in[:, main:], a_log, dt_bias)
    o = gdn_delta(proj, conv_w, gates, norm_w, batch, seq)
    return proj_residual_ln(o, w_out.astype(BF16), h32, gain, bias)


def _dil_attn_kernel(q_ref, kp_ref, ko_ref, vp_ref, vo_ref, bias_ref, o_ref, lse_ref):
    lb = pl.program_id(2)
    blk = DIL_BLOCK
    qi = lax.broadcasted_iota(jnp.int32, (blk, blk), 0)
    kj = lax.broadcasted_iota(jnp.int32, (blk, blk), 1)
    mask_prev = (qi <= kj) & (lb > 0)
    mask_own = kj <= qi
    lane = lax.broadcasted_iota(jnp.int32, (blk, LANES), 1)
    lse_all = jnp.zeros((blk, LANES), F32)
    scale = HEAD_DIM ** -0.5
    heads = range(DIL_HEADS)
    sls = [slice(hd * HEAD_DIM, (hd + 1) * HEAD_DIM) for hd in heads]
    s_prev = [jnp.where(mask_prev, _dot_nt(q_ref[0, :, sl], kp_ref[0, :, sl]) * scale + bias_ref[hd][:, :blk], -jnp.inf)
              for hd, sl in zip(heads, sls)]
    s_own = [jnp.where(mask_own, _dot_nt(q_ref[0, :, sl], ko_ref[0, :, sl]) * scale + bias_ref[hd][:, blk:], -jnp.inf)
             for hd, sl in zip(heads, sls)]
    m = [jnp.maximum(jnp.max(sp, axis=-1, keepdims=True), jnp.max(so, axis=-1, keepdims=True))
         for sp, so in zip(s_prev, s_own)]
    p_prev = [jnp.exp(sp - mm) for sp, mm in zip(s_prev, m)]
    p_own = [jnp.exp(so - mm) for so, mm in zip(s_own, m)]
    den = [jnp.sum(pp, axis=-1, keepdims=True) + jnp.sum(po, axis=-1, keepdims=True) for pp, po in zip(p_prev, p_own)]
    o = [_dot_bf16(pp, vp_ref[0, :, sl]) + _dot_bf16(po, vo_ref[0, :, sl]) for pp, po, sl in zip(p_prev, p_own, sls)]
    for hd, sl in zip(heads, sls):
        o_ref[0, :, sl] = o[hd] / den[hd]
        lse_all = jnp.where(lane == hd, m[hd] + jnp.log(den[hd]), lse_all)
    lse_ref[0] = lse_all


def dilated_group_attention(q, k, v, bias, dil, batch, seq):
    length = seq // dil
    nb = length // DIL_BLOCK
    kvd = DIL_KV_DIM
    qv = q.reshape(batch, length, dil * kvd)
    kview = k.reshape(batch, length, dil * kvd)
    vview = v.reshape(batch, length, dil * kvd)
    blk = DIL_BLOCK
    own = lambda b, r, i: (b, i, r)
    prev = lambda b, r, i: (b, jnp.maximum(i - 1, 0), r)
    o, lse = pl.pallas_call(
        _dil_attn_kernel,
        out_shape=(jax.ShapeDtypeStruct((batch, length, dil * kvd), F32),
                   jax.ShapeDtypeStruct((batch, length, dil * LANES), F32)),
        grid=(batch, dil, nb),
        in_specs=[
            pl.BlockSpec((1, blk, kvd), own),
            pl.BlockSpec((1, blk, kvd), prev),
            pl.BlockSpec((1, blk, kvd), own),
            pl.BlockSpec((1, blk, kvd), prev),
            pl.BlockSpec((1, blk, kvd), own),
            pl.BlockSpec((DIL_HEADS, blk, 2 * blk), lambda b, r, i: (0, 0, 0)),
        ],
        out_specs=(pl.BlockSpec((1, blk, kvd), own), pl.BlockSpec((1, blk, LANES), own)),
        compiler_params=_params("parallel", "parallel", "parallel"),
        name=f"dilated_attention_d{dil}",
    )(qv, kview, kview, vview, vview, bias)
    return o.reshape(batch * seq, kvd), lse.reshape(batch * seq, LANES)


def _dil_out_kernel(o0_ref, o1_ref, o2_ref, l0_ref, l1_ref, l2_ref, w_ref, h_ref, gain_ref, bias_ref,
                    o32_ref, o16_ref, mix_ref):
    l0, l1, l2 = l0_ref[...], l1_ref[...], l2_ref[...]
    m = jnp.maximum(jnp.maximum(l0, l1), l2)
    e0, e1, e2 = jnp.exp(l0 - m), jnp.exp(l1 - m), jnp.exp(l2 - m)
    inv = 1.0 / (e0 + e1 + e2)
    w0, w1, w2 = e0 * inv, e1 * inv, e2 * inv
    for hd in range(DIL_HEADS):
        sl = slice(hd * HEAD_DIM, (hd + 1) * HEAD_DIM)
        mix_ref[:, sl] = (w0[:, hd:hd + 1] * o0_ref[:, sl] + w1[:, hd:hd + 1] * o1_ref[:, sl]
                          + w2[:, hd:hd + 1] * o2_ref[:, sl])
    mix = jnp.dot(mix_ref[...].astype(BF16), w_ref[...], preferred_element_type=F32)
    y = _layer_norm_rows(DEEPNORM_ALPHA * h_ref[...] + mix, gain_ref[...], bias_ref[...])
    o32_ref[...] = y
    o16_ref[...] = y.astype(BF16)


def dilated_out(outs, lses, w_o, h, gain, bias, tm=LN_TM):
    m, kvd = outs[0].shape
    d = w_o.shape[1]
    tm = min(tm, m)
    row = lambda i: (i, 0)
    fixed = lambda i: (0, 0)
    return pl.pallas_call(
        _dil_out_kernel,
        out_shape=(jax.ShapeDtypeStruct((m, d), F32), jax.ShapeDtypeStruct((m, d), BF16)),
        grid=(m // tm,),
        in_specs=[pl.BlockSpec((tm, kvd), row)] * 3 + [pl.BlockSpec((tm, LANES), row)] * 3 + [
            pl.BlockSpec((kvd, d), fixed), pl.BlockSpec((tm, d), row),
            pl.BlockSpec((1, d), fixed), pl.BlockSpec((1, d), fixed)],
        out_specs=(pl.BlockSpec((tm, d), row), pl.BlockSpec((tm, d), row)),
        scratch_shapes=[pltpu.VMEM((tm, kvd), F32)],
        compiler_params=_params("parallel"),
        name="dilated_out_ln",
    )(*outs, *lses, w_o, h, gain.reshape(1, d), bias.reshape(1, d))


def _t5_bucket(dist):
    exact = REL_BUCKETS // 2
    distf = jnp.maximum(dist, exact).astype(F32)
    large = exact + (jnp.log(distf / exact) / math.log(REL_MAX_DIST / exact)
                     * (REL_BUCKETS - exact)).astype(jnp.int32)
    return jnp.where(dist < exact, dist, jnp.minimum(large, REL_BUCKETS - 1))


def _band_bias(rel_bias_g, dil):
    qi = jnp.arange(DIL_BLOCK)[:, None]
    kj = jnp.arange(2 * DIL_BLOCK)[None, :]
    steps = DIL_BLOCK + qi - kj
    bias = rel_bias_g[_t5_bucket(jnp.maximum(steps, 0) * dil)]
    return jnp.transpose(bias, (2, 0, 1)).astype(F32)


def _router_kernel(h_ref, w_ref, b_ref, o_ref, cnt_ref, running):
    @pl.when(pl.program_id(0) == 0)
    def _():
        running[...] = jnp.zeros(running.shape, F32)

    logits = _dot_split(h_ref[...], w_ref) + b_ref[...]
    lane = lax.broadcasted_iota(jnp.int32, logits.shape, 1)
    neg = -jnp.inf
    big = jnp.int32(LANES)
    is_group = lane < MOE_GROUPS
    gl = jnp.where(is_group, logits, neg)
    gmax = jnp.max(gl, axis=-1, keepdims=True)
    gidx = jnp.min(jnp.where(gl == gmax, lane, big), axis=-1, keepdims=True)
    gden = jnp.sum(jnp.exp(gl - gmax), axis=-1, keepdims=True)
    group_gate = 1.0 / gden
    lo = MOE_GROUPS + gidx * MOE_EXPERTS_PER_GROUP
    in_group = (lane >= lo) & (lane < lo + MOE_EXPERTS_PER_GROUP)
    ll = jnp.where(in_group, logits, neg)
    m1 = jnp.max(ll, axis=-1, keepdims=True)
    i1 = jnp.min(jnp.where(ll == m1, lane, big), axis=-1, keepdims=True)
    rest = jnp.where(lane == i1, neg, ll)
    m2 = jnp.max(rest, axis=-1, keepdims=True)
    i2 = jnp.min(jnp.where(rest == m2, lane, big), axis=-1, keepdims=True)
    den = jnp.sum(jnp.exp(ll - m1), axis=-1, keepdims=True)
    p1 = 1.0 / den
    p2 = jnp.exp(m2 - m1) / den
    g1 = group_gate * p1 / (p1 + p2)
    g2 = group_gate * p2 / (p1 + p2)
    id1 = (i1 - MOE_GROUPS).astype(F32)
    id2 = (i2 - MOE_GROUPS).astype(F32)
    tm = logits.shape[0]
    oh1 = (lane == i1).astype(F32)
    oh2 = (lane == i2).astype(F32)
    both = oh1 + oh2
    r_i = lax.broadcasted_iota(jnp.int32, (tm, tm), 0)
    c_i = lax.broadcasted_iota(jnp.int32, (tm, tm), 1)
    before = jnp.dot((c_i < r_i).astype(BF16), both.astype(BF16), preferred_element_type=F32) + running[...]
    rank1 = jnp.sum(before * oh1, axis=-1, keepdims=True)
    rank2 = jnp.sum(before * oh2, axis=-1, keepdims=True)
    running[...] = running[...] + jnp.sum(both, axis=0, keepdims=True)
    cnt_ref[...] = running[...]
    o_ref[...] = jnp.where(lane == 0, id1, jnp.where(lane == 1, id2, jnp.where(lane == 2, g1, jnp.where(
        lane == 3, g2, jnp.where(lane == 4, rank1, rank2)))))


def moe_router(h, w_group, b_group, w_expert, b_expert, tm=ROUTER_TM):
    m, d = h.shape
    tm = min(tm, m)
    pad = LANES - MOE_GROUPS - MOE_EXPERTS
    w = jnp.pad(jnp.concatenate([w_group, w_expert], axis=1), ((0, 0), (0, pad)))
    b = jnp.pad(jnp.concatenate([b_group, b_expert]), (0, pad)).reshape(1, LANES)
    out, cnt = pl.pallas_call(
        _router_kernel,
        out_shape=(jax.ShapeDtypeStruct((m, LANES), F32), jax.ShapeDtypeStruct((1, LANES), F32)),
        grid=(m // tm,),
        in_specs=[pl.BlockSpec((tm, d), lambda i: (i, 0)),
                  pl.BlockSpec((d, 2 * LANES), lambda i: (0, 0)),
                  pl.BlockSpec((1, LANES), lambda i: (0, 0))],
        out_specs=(pl.BlockSpec((tm, LANES), lambda i: (i, 0)), pl.BlockSpec((1, LANES), lambda i: (0, 0))),
        scratch_shapes=[pltpu.VMEM((1, LANES), F32)],
        compiler_params=_params("arbitrary"),
        name="moe_router",
    )(h, _hi_lo_columns(w), b)
    k = MOE_TOP_K
    counts = cnt[0, MOE_GROUPS:MOE_GROUPS + MOE_EXPERTS].astype(jnp.int32)
    return out[:, :k].astype(jnp.int32), out[:, k:2 * k], out[:, 2 * k:3 * k].astype(jnp.int32), counts


def _gather_rows_start(src_hbm, idx_ref, idx_base, buf, sem, unrolled):
    def copy(g, j):
        pltpu.make_async_copy(src_hbm.at[pl.ds(idx_ref[idx_base + g * SUBLANES + j], 1)],
                              buf.at[g, pl.ds(j, 1)], sem).start()

    if unrolled:
        for g in range(buf.shape[0]):
            for j in range(SUBLANES):
                copy(g, j)
    else:
        def body(g, carry):
            for j in range(SUBLANES):
                copy(g, j)
            return carry
        lax.fori_loop(0, buf.shape[0], body, 0)


def _gather_rows_wait(buf, sem):
    pltpu.make_async_copy(buf, buf, sem).wait()


def _gathered_blocks(src_hbm, idx_ref, bufs, sems, compute):
    i = pl.program_id(0)
    n = pl.num_programs(0)
    rows = bufs[0].shape[0] * bufs[0].shape[1]
    width = bufs[0].shape[2]

    @pl.when(i == 0)
    def _():
        _gather_rows_start(src_hbm, idx_ref, 0, bufs[0], sems.at[0], unrolled=False)

    def step(s):
        _gather_rows_wait(bufs[s], sems.at[s])
        _gather_rows_start(src_hbm, idx_ref, jnp.minimum(i + 1, n - 1) * rows, bufs[1 - s], sems.at[1 - s],
                           unrolled=True)
        compute(bufs[s][...].reshape(rows, width))

        @pl.when(i == n - 1)
        def _():
            _gather_rows_wait(bufs[1 - s], sems.at[1 - s])

    for s in range(2):
        pl.when(i % 2 == s)(functools.partial(step, s))


def _expert_kernel(bexp_ref, rtok_ref, nused_ref, h_hbm, w1_ref, w3_ref, w2_ref, y_ref,
                   xbuf_a, xbuf_b, sems, wb1, wb3, wb2):
    i = pl.program_id(0)

    @pl.when((i < nused_ref[0]) & ((i == 0) | (bexp_ref[i] != bexp_ref[jnp.maximum(i - 1, 0)])))
    def _():
        wb1[...] = w1_ref[0, 0].astype(BF16)
        wb3[...] = w3_ref[0, 0].astype(BF16)
        wb2[...] = w2_ref[0, 0].astype(BF16)

    def compute(x):
        x = x.astype(BF16)
        h1 = jnp.dot(x, wb1[...], preferred_element_type=F32)
        h3 = jnp.dot(x, wb3[...], preferred_element_type=F32)
        hmid = (h1 * jax.nn.sigmoid(h1) * h3).astype(BF16)
        y_ref[...] = jnp.dot(hmid, wb2[...], preferred_element_type=F32)

    _gathered_blocks(h_hbm, rtok_ref, (xbuf_a, xbuf_b), sems, compute)


def moe_experts(h, row_token, block_expert, n_used, w1, w3, w2, layer, tm=MOE_TM):
    n_rows = row_token.shape[0]
    n_blocks = n_rows // tm
    d = h.shape[1]
    ff = w1.shape[3]
    grid_spec = pltpu.PrefetchScalarGridSpec(
        num_scalar_prefetch=3,
        grid=(n_blocks,),
        in_specs=[
            pl.BlockSpec(memory_space=pl.ANY),
            pl.BlockSpec((1, 1, d, ff), lambda i, be, rt, nu: (layer, be[i], 0, 0)),
            pl.BlockSpec((1, 1, d, ff), lambda i, be, rt, nu: (layer, be[i], 0, 0)),
            pl.BlockSpec((1, 1, ff, d), lambda i, be, rt, nu: (layer, be[i], 0, 0)),
        ],
        out_specs=pl.BlockSpec((tm, d), lambda i, be, rt, nu: (i, 0)),
        scratch_shapes=[pltpu.VMEM((tm // SUBLANES, SUBLANES, d), F32), pltpu.VMEM((tm // SUBLANES, SUBLANES, d), F32),
                        pltpu.SemaphoreType.DMA((2,)),
                        pltpu.VMEM((d, ff), BF16), pltpu.VMEM((d, ff), BF16), pltpu.VMEM((ff, d), BF16)],
    )
    return pl.pallas_call(
        _expert_kernel,
        out_shape=jax.ShapeDtypeStruct((n_rows, d), F32),
        grid_spec=grid_spec,
        compiler_params=_params("arbitrary"),
        name="moe_experts",
    )(block_expert, row_token, n_used, h, w1, w3, w2)


def _combine_kernel(dest_ref, y_hbm, gate_ref, h_ref, gain_ref, bias_ref, o32_ref, o16_ref, ybuf_a, ybuf_b, sems):
    tm = h_ref.shape[0]

    def compute(yy):
        gates = gate_ref[...]
        ffn = yy[0:tm, :] * gates[:, 0:1] + yy[tm:, :] * gates[:, 1:2]
        y = _layer_norm_rows(DEEPNORM_ALPHA * h_ref[...] + ffn, gain_ref[...], bias_ref[...])
        o32_ref[...] = y
        o16_ref[...] = y.astype(BF16)

    _gathered_blocks(y_hbm, dest_ref, (ybuf_a, ybuf_b), sems, compute)


def moe_combine_ln(y_rows, dest, gates, h, gain, bias, tm=COMBINE_TM):
    m, d = h.shape
    tm = min(tm, m)
    grid_spec = pltpu.PrefetchScalarGridSpec(
        num_scalar_prefetch=1,
        grid=(m // tm,),
        in_specs=[
            pl.BlockSpec(memory_space=pl.ANY),
            pl.BlockSpec((tm, MOE_TOP_K), lambda i, de: (i, 0)),
            pl.BlockSpec((tm, d), lambda i, de: (i, 0)),
            pl.BlockSpec((1, d), lambda i, de: (0, 0)),
            pl.BlockSpec((1, d), lambda i, de: (0, 0)),
        ],
        out_specs=(pl.BlockSpec((tm, d), lambda i, de: (i, 0)),
                   pl.BlockSpec((tm, d), lambda i, de: (i, 0))),
        scratch_shapes=[pltpu.VMEM((MOE_TOP_K * tm // SUBLANES, SUBLANES, d), F32),
                        pltpu.VMEM((MOE_TOP_K * tm // SUBLANES, SUBLANES, d), F32), pltpu.SemaphoreType.DMA((2,))],
    )
    return pl.pallas_call(
        _combine_kernel,
        out_shape=(jax.ShapeDtypeStruct((m, d), F32), jax.ShapeDtypeStruct((m, d), BF16)),
        grid_spec=grid_spec,
        compiler_params=_params("arbitrary"),
        name="moe_combine_ln",
    )(dest.reshape(-1), y_rows, gates, h, gain.reshape(1, d), bias.reshape(1, d))


def moe_layer(h32, w_group, b_group, w_expert, b_expert, w1, w3, w2, layer, gain, bias, tm=MOE_TM):
    t = h32.shape[0]
    ids, gates, ranks, counts = moe_router(h32, w_group, b_group, w_expert, b_expert)
    eid = ids.reshape(-1)
    n_pairs = eid.shape[0]
    padded = (counts + tm - 1) // tm * tm
    seg_end = jnp.cumsum(padded)
    seg_start = seg_end - padded
    onehot = eid[:, None] == jnp.arange(MOE_EXPERTS, dtype=jnp.int32)[None, :]
    dest = (jnp.sum(jnp.where(onehot, seg_start[None, :], 0), axis=1) + ranks.reshape(-1)).astype(jnp.int32)
    n_blocks = -(-n_pairs // tm) + MOE_EXPERTS
    n_rows = n_blocks * tm
    token_of_pair = (jnp.arange(n_pairs, dtype=jnp.int32) // MOE_TOP_K)
    row_token = jnp.zeros((n_rows,), jnp.int32).at[dest].set(token_of_pair)
    block_start = jnp.arange(n_blocks, dtype=jnp.int32) * tm
    block_expert = jnp.minimum(jnp.sum((seg_end[None, :] <= block_start[:, None]).astype(jnp.int32), axis=1),
                               MOE_EXPERTS - 1)
    n_used = (seg_end[-1:] // tm).astype(jnp.int32)
    y_rows = moe_experts(h32, row_token, block_expert, n_used, w1, w3, w2, layer, tm=tm)
    ctm = min(COMBINE_TM, t)
    dest_tiles = jnp.transpose(dest.reshape(t // ctm, ctm, MOE_TOP_K), (0, 2, 1))
    return moe_combine_ln(y_rows, dest_tiles, gates, h32, gain, bias, tm=ctm)


def kernel(x, gdn_w_in, gdn_conv, gdn_a_log, gdn_dt_bias, gdn_norm, gdn_w_out, kv_w_k, kv_w_v, dil_w_q, dil_w_o,
           rel_bias, ln_gain, ln_bias, moe_w_group, moe_b_group, moe_w_expert, moe_b_expert, moe_w1, moe_w3, moe_w2):
    batch, seq, d = x.shape
    t = batch * seq
    h32 = x.reshape(t, d)
    h16 = h32.astype(BF16)
    k = v = None
    biases = None
    for layer in range(DEPTH):
        if layer < N_A_LAYERS:
            h32, h16 = gated_deltanet_layer(
                h32, h16, gdn_w_in[layer], gdn_conv[layer], gdn_a_log[layer], gdn_dt_bias[layer],
                gdn_norm[layer], gdn_w_out[layer], ln_gain[layer, 0], ln_bias[layer, 0], batch, seq)
        else:
            if layer == N_A_LAYERS:
                k = matmul(h16, kv_w_k.astype(BF16), out_dtype=BF16)
                v = matmul(h16, kv_w_v.astype(BF16), out_dtype=BF16)
                biases = [_band_bias(rel_bias[:, gi * DIL_HEADS:(gi + 1) * DIL_HEADS], dil)
                          for gi, (_, dil) in enumerate(DIL_CONFIGS)]
            j = layer - N_A_LAYERS
            outs, lses = [], []
            for gi, (_, dil) in enumerate(DIL_CONFIGS):
                q = matmul(h16, dil_w_q[j][:, gi * DIL_KV_DIM:(gi + 1) * DIL_KV_DIM].astype(BF16), out_dtype=BF16)
                o, lse = dilated_group_attention(q, k, v, biases[gi], dil, batch, seq)
                outs.append(o)
                lses.append(lse)
            h32, h16 = dilated_out(outs, lses, dil_w_o[j].astype(BF16), h32, ln_gain[layer, 0], ln_bias[layer, 0])
        h32, h16 = moe_layer(h32, moe_w_group[layer], moe_b_group[layer], moe_w_expert[layer], moe_b_expert[layer],
                             moe_w1, moe_w3, moe_w2, layer, ln_gain[layer, 1], ln_bias[layer, 1])
    return h32.reshape(batch, seq, d)
```

```python
import functools
import math

import jax
import jax.numpy as jnp
from jax import lax
from jax.experimental import pallas as pl
from jax.experimental.pallas import tpu as pltpu

F32 = jnp.float32
BF16 = jnp.bfloat16

DEPTH = 4
N_A_LAYERS = DEPTH // 2
HEAD_DIM = 128
GDN_QK_HEADS = 16
GDN_V_HEADS = 32
GDN_CONV = 4
GDN_CHUNK = 64
GDN_QK_DIM = GDN_QK_HEADS * HEAD_DIM
GDN_V_DIM = GDN_V_HEADS * HEAD_DIM
GDN_CONV_DIM = 2 * GDN_QK_DIM + GDN_V_DIM
DIL_CONFIGS = ((128, 1), (512, 4), (2048, 16))
N_DIL = len(DIL_CONFIGS)
DIL_HEADS = 8
DIL_BLOCK = 128
DIL_KV_DIM = DIL_HEADS * HEAD_DIM
REL_BUCKETS = 32
REL_MAX_DIST = 2048
MOE_GROUPS = 4
MOE_EXPERTS_PER_GROUP = 8
MOE_EXPERTS = MOE_GROUPS * MOE_EXPERTS_PER_GROUP
MOE_TOP_K = 2
DEEPNORM_ALPHA = (2 * DEPTH) ** 0.25
LN_EPS = 1e-5
RMS_EPS = 1e-6
L2_EPS = 1e-6

LANES = 128
SUBLANES = 8
VMEM_LIMIT_BYTES = 56 * 1024 * 1024

MM_TM = 512
MM_TN = 1024
LN_TM = 256
GATE_TM = 512
GDN_TB = 512
GDN_QK_PER_STEP = 4
MOE_TM = 256
ROUTER_TM = 512
COMBINE_TM = 256


def _params(*semantics):
    return pltpu.CompilerParams(dimension_semantics=semantics, vmem_limit_bytes=VMEM_LIMIT_BYTES)


def _split_bf16(x):
    hi = x.astype(BF16)
    return hi, (x - hi.astype(F32)).astype(BF16)


def _dot_split(x, w2_ref):
    n = w2_ref.shape[1] // 2
    hi, lo = _split_bf16(x)
    a = jnp.dot(hi, w2_ref[...], preferred_element_type=F32)
    b = jnp.dot(lo, w2_ref[:, :n], preferred_element_type=F32)
    return a[:, :n] + a[:, n:] + b


def _hi_lo_columns(w):
    hi = w.astype(BF16)
    return jnp.concatenate([hi, (w - hi.astype(F32)).astype(BF16)], axis=1)


def _store_row_contiguous(ref, y):
    rows, width = y.shape
    w = width // LANES
    for c in range(w):
        ref[pl.ds(c, rows, stride=w), :] = y[:, c * LANES:(c + 1) * LANES]


def _load_row_contiguous(ref, rows):
    w = ref.shape[0] // rows
    return jnp.concatenate([ref[pl.ds(c, rows, stride=w), :] for c in range(w)], axis=1)


def _layer_norm_rows(y, gain, bias):
    mu = jnp.mean(y, axis=-1, keepdims=True)
    yc = y - mu
    var = jnp.mean(yc * yc, axis=-1, keepdims=True)
    return yc * lax.rsqrt(var + LN_EPS) * gain + bias


def _matmul_kernel(a_ref, b_ref, o_ref):
    o_ref[...] = jnp.dot(a_ref[...].astype(BF16), b_ref[...],
                         preferred_element_type=F32).astype(o_ref.dtype)


def matmul(a, b, out_dtype=F32, tm=MM_TM, tn=MM_TN):
    m, k = a.shape
    n = b.shape[1]
    tm, tn = min(tm, m), min(tn, n)
    assert m % tm == 0 and n % tn == 0
    return pl.pallas_call(
        _matmul_kernel,
        out_shape=jax.ShapeDtypeStruct((m, n), out_dtype),
        grid=(n // tn, m // tm),
        in_specs=[pl.BlockSpec((tm, k), lambda j, i: (i, 0)),
                  pl.BlockSpec((k, tn), lambda j, i: (0, j))],
        out_specs=pl.BlockSpec((tm, tn), lambda j, i: (i, j)),
        compiler_params=_params("parallel", "parallel"),
        name="matmul",
    )(a, b)


def _proj_ln_kernel(a_ref, w_ref, h_ref, gain_ref, bias_ref, o32_ref, o16_ref, orow_ref):
    mix = jnp.dot(a_ref[...].astype(BF16), w_ref[...], preferred_element_type=F32)
    y = _layer_norm_rows(DEEPNORM_ALPHA * h_ref[...] + mix, gain_ref[...], bias_ref[...])
    o32_ref[...] = y
    o16_ref[...] = y.astype(BF16)
    _store_row_contiguous(orow_ref, y)


def proj_residual_ln(a, w, h, gain, bias, tm=LN_TM):
    m, k = a.shape
    d = w.shape[1]
    tm = min(tm, m)
    assert m % tm == 0
    return pl.pallas_call(
        _proj_ln_kernel,
        out_shape=(jax.ShapeDtypeStruct((m, d), F32), jax.ShapeDtypeStruct((m, d), BF16),
                   jax.ShapeDtypeStruct((m * (d // LANES), LANES), F32)),
        grid=(m // tm,),
        in_specs=[pl.BlockSpec((tm, k), lambda i: (i, 0)),
                  pl.BlockSpec((k, d), lambda i: (0, 0)),
                  pl.BlockSpec((tm, d), lambda i: (i, 0)),
                  pl.BlockSpec((1, d), lambda i: (0, 0)),
                  pl.BlockSpec((1, d), lambda i: (0, 0))],
        out_specs=(pl.BlockSpec((tm, d), lambda i: (i, 0)),
                   pl.BlockSpec((tm, d), lambda i: (i, 0)),
                   pl.BlockSpec((tm * (d // LANES), LANES), lambda i: (i, 0))),
        compiler_params=_params("parallel"),
        name="proj_residual_ln",
    )(a, w, h, gain.reshape(1, d), bias.reshape(1, d))


def _gdn_gate_kernel(h_ref, w_ref, alog_ref, dt_ref, o_ref):
    tm = h_ref.shape[0]
    n_groups = o_ref.shape[0]
    width = 2 * GDN_V_HEADS // n_groups
    ba = _dot_split(h_ref[...], w_ref)
    xa = ba + dt_ref[...]
    softplus = jnp.maximum(xa, 0.0) + jnp.log(1.0 + jnp.exp(-jnp.abs(xa)))
    g = -jnp.exp(alog_ref[...]) * softplus
    row = lax.broadcasted_iota(jnp.int32, (tm, tm), 0)
    col = lax.broadcasted_iota(jnp.int32, (tm, tm), 1)
    tri = ((row // GDN_CHUNK == col // GDN_CHUNK) & (col <= row)).astype(BF16)
    g_hi, g_lo = _split_bf16(g)
    gc2 = jnp.dot(tri, jnp.concatenate([g_hi, g_lo], axis=1), preferred_element_type=F32)
    gc = gc2[:, :LANES] + gc2[:, LANES:]
    lane = lax.broadcasted_iota(jnp.int32, ba.shape, 1)
    slab = jnp.where(lane % width < width // 2, jax.nn.sigmoid(ba), gc)
    for r in range(n_groups):
        o_ref[r] = slab if r == 0 else pltpu.roll(slab, LANES - width * r, axis=1)


def gdn_gates(h, w_ba, a_log, dt_bias, n_qk=GDN_QK_PER_STEP, tm=GATE_TM):
    m, d = h.shape
    tm = min(tm, m)
    v = 2 * n_qk
    n_groups = GDN_V_HEADS // v
    cols = [(v * (l // (2 * v)) + l % (2 * v)) if l % (2 * v) < v else (GDN_V_HEADS + v * (l // (2 * v)) + l % (2 * v) - v)
            for l in range(2 * GDN_V_HEADS)]
    is_decay = jnp.array([l % (2 * v) >= v for l in range(2 * GDN_V_HEADS)])
    head = jnp.array([c % GDN_V_HEADS for c in cols], jnp.int32)
    pad = LANES - 2 * GDN_V_HEADS
    w = jnp.pad(w_ba[:, jnp.array(cols, jnp.int32)], ((0, 0), (0, pad)))
    alog = jnp.pad(jnp.where(is_decay, a_log[head], 0.0), (0, pad)).reshape(1, LANES)
    dt = jnp.pad(jnp.where(is_decay, dt_bias[head], 0.0), (0, pad)).reshape(1, LANES)
    return pl.pallas_call(
        _gdn_gate_kernel,
        out_shape=jax.ShapeDtypeStruct((n_groups, m, LANES), F32),
        grid=(m // tm,),
        in_specs=[pl.BlockSpec((tm, d), lambda i: (i, 0)),
                  pl.BlockSpec((d, 2 * LANES), lambda i: (0, 0)),
                  pl.BlockSpec((1, LANES), lambda i: (0, 0)),
                  pl.BlockSpec((1, LANES), lambda i: (0, 0))],
        out_specs=pl.BlockSpec((n_groups, tm, LANES), lambda i: (0, i, 0)),
        compiler_params=_params("parallel"),
        name="gdn_gates",
    )(h, _hi_lo_columns(w), alog, dt)


def _dot_bf16(a, b):
    return jnp.dot(a.astype(BF16), b.astype(BF16), preferred_element_type=F32)


def _dot_nt(a, b):
    return lax.dot_general(a.astype(BF16), b.astype(BF16), (((1,), (1,)), ((), ())),
                           preferred_element_type=F32)


def _dot_tn(a, b):
    return lax.dot_general(a.astype(BF16), b.astype(BF16), (((0,), (0,)), ((), ())),
                           preferred_element_type=F32)


def _block_diag2(x0, x1):
    z = jnp.zeros_like(x0)
    return jnp.concatenate([jnp.concatenate([x0, z], axis=1), jnp.concatenate([z, x1], axis=1)], axis=0)


def _gdn_kernel(q_ref, k_ref, v_ref, z_ref, wq_ref, wk_ref, wv_ref, g_ref, nw_ref,
                o_ref, qx, kx, vx, qn, kn, vn, state):
    blk = pl.program_id(2)
    tb = q_ref.shape[0]
    n_chunks = tb // GDN_CHUNK
    n_qk = q_ref.shape[1] // HEAD_DIM
    c = GDN_CHUNK
    hd = HEAD_DIM
    tail = SUBLANES

    @pl.when(blk == 0)
    def _():
        qx[0:tail, :] = jnp.zeros((tail, qx.shape[1]), F32)
        kx[0:tail, :] = jnp.zeros((tail, kx.shape[1]), F32)
        vx[0:tail, :] = jnp.zeros((tail, vx.shape[1]), F32)
        state[...] = jnp.zeros(state.shape, F32)

    def conv_silu(x_ref, ext, w_ref):
        ext[tail:tail + tb, :] = x_ref[...].astype(F32)
        y = None
        for j in range(GDN_CONV):
            term = w_ref[j:j + 1, :] * ext[pl.ds(tail - (GDN_CONV - 1) + j, tb), :]
            y = term if y is None else y + term
        ext[0:tail, :] = ext[tb:tb + tail, :]
        return y * jax.nn.sigmoid(y)

    def l2n(x):
        heads = [x[:, g * hd:(g + 1) * hd] for g in range(n_qk)]
        return jnp.concatenate([xh * lax.rsqrt(jnp.sum(xh * xh, axis=-1, keepdims=True) + L2_EPS) for xh in heads], axis=1)

    qn[...] = l2n(conv_silu(q_ref, qx, wq_ref)) * (HEAD_DIM ** -0.5)
    kn[...] = l2n(conv_silu(k_ref, kx, wk_ref))
    vn[...] = conv_silu(v_ref, vx, wv_ref)

    row = lax.broadcasted_iota(jnp.int32, (c, 2 * c), 0)
    lane = lax.broadcasted_iota(jnp.int32, (c, 2 * c), 1)
    col = lane & (c - 1)
    head0 = lane < c
    causal = col <= row
    strict = col < row
    in_block = (row // 16) == (col // 16)
    eye = (row == col).astype(F32)
    norm_w = nw_ref[...]

    def mm(x, y):
        yb = jnp.concatenate([jnp.where(head0, y, 0.0), jnp.where(head0, 0.0, y)], axis=0)
        return _dot_bf16(x, yb)

    units = [(ci, g) for ci in range(n_chunks) for g in range(n_qk)]
    each = lambda f, *lists: [f(*xs) for xs in zip(*lists)]
    qcs = [qn[ci * c:(ci + 1) * c, g * hd:(g + 1) * hd] for ci, g in units]
    kcs = [kn[ci * c:(ci + 1) * c, g * hd:(g + 1) * hd] for ci, g in units]
    n_v = 2 * n_qk
    gates = g_ref[0]
    gates_t = gates.T
    betas = [[gates[ci * c:(ci + 1) * c, 2 * g + hh:2 * g + hh + 1] for hh in range(2)] for ci, g in units]
    gcs = [[gates[ci * c:(ci + 1) * c, n_v + 2 * g + hh:n_v + 2 * g + hh + 1] for hh in range(2)] for ci, g in units]

    def packed_row(ci, g):
        w0 = (ci // 2) * 2 * c
        r0, r1 = (gates_t[n_v + 2 * g + hh:n_v + 2 * g + hh + 1, w0:w0 + 2 * c] for hh in range(2))
        if ci % 2 == 0:
            return jnp.where(head0[0:1], r0, pltpu.roll(r1, c, axis=1))
        return jnp.where(head0[0:1], pltpu.roll(r0, c, axis=1), r1)

    grps = [packed_row(ci, g) for ci, g in units]
    egs = [[jnp.exp(g[hh]) for hh in range(2)] for g in gcs]
    glast = [[g[hh][c - 1:c, :] for hh in range(2)] for g in gcs]
    both = each(lambda q, k: _dot_nt(jnp.concatenate([q, k], axis=0), jnp.concatenate([k, k], axis=0)), qcs, kcs)
    decayp = each(lambda g, gr: jnp.exp(jnp.where(causal, jnp.where(head0, g[0], g[1]) - gr, -jnp.inf)), gcs, grps)
    ap = each(lambda b, bo, dc: jnp.where(strict, jnp.where(head0, b[0], b[1]) * bo[c:] * dc, 0.0), betas, both, decayp)
    qkd = each(lambda bo, dc: bo[:c] * dc, both, decayp)
    d = each(lambda a: jnp.where(in_block, a, 0.0), ap)
    e = each(lambda a, dd: a - dd, ap, d)
    n1 = each(lambda dd: -dd, d)
    p = each(lambda n: eye + n, n1)
    n2 = each(mm, n1, n1)
    p = each(lambda pp, n: pp + mm(pp, n), p, n2)
    n4 = each(mm, n2, n2)
    p = each(lambda pp, n: pp + mm(pp, n), p, n4)
    n8 = each(mm, n4, n4)
    dinv = each(lambda pp, n: pp + mm(pp, n), p, n8)
    nb = each(mm, dinv, e)
    nb2 = each(mm, nb, nb)
    qq = each(lambda n: eye - n, nb)
    qq = each(lambda x, n: x + mm(x, n), qq, nb2)
    tinvp = each(mm, qq, dinv)
    rhs = [[jnp.concatenate([vn[ci * c:(ci + 1) * c, (2 * g + hh) * hd:(2 * g + hh + 1) * hd] * betas[u][hh],
                             kcs[u] * (betas[u][hh] * egs[u][hh])], axis=1) for hh in range(2)]
           for u, (ci, g) in enumerate(units)]
    uw = each(lambda t, r: _dot_bf16(t, _block_diag2(r[0], r[1])), tinvp, rhs)
    wu = [[jnp.concatenate([x[:, (2 * hh + 1) * hd:(2 * hh + 2) * hd], x[:, 2 * hh * hd:(2 * hh + 1) * hd]], axis=1)
           for hh in range(2)] for x in uw]
    qo = each(lambda x, w2: _dot_bf16(x, _block_diag2(w2[0], w2[1])), qkd, wu)
    pre = {}
    for u, (ci, g) in enumerate(units):
        for hh in range(2):
            rb = _dot_tn(kcs[u] * jnp.exp(glast[u][hh] - gcs[u][hh]), wu[u][hh])
            q_eff = qcs[u] * egs[u][hh] - qo[u][:, 2 * hh * hd:(2 * hh + 1) * hd]
            rq = jnp.concatenate([rb[:, :hd], q_eff], axis=0).astype(BF16)
            pre[ci, 2 * g + hh] = (rq, rb[:, hd:], qo[u][:, (2 * hh + 1) * hd:(2 * hh + 2) * hd], jnp.exp(glast[u][hh]))

    states = [state[vh] for vh in range(n_v)]
    for ci in range(n_chunks):
        r0 = ci * c
        for vh in range(n_v):
            rq, b_c, o_c, cd = pre[ci, vh]
            s = states[vh]
            res = jnp.dot(rq, s.astype(BF16), preferred_element_type=F32)
            states[vh] = s * cd - res[:hd] + b_c
            out = res[hd:] + o_c
            out = out * lax.rsqrt(jnp.mean(out * out, axis=-1, keepdims=True) + RMS_EPS) * norm_w
            zc = z_ref[r0:r0 + c, vh * hd:(vh + 1) * hd].astype(F32)
            out = out * (zc * jax.nn.sigmoid(zc))
            o_ref[r0:r0 + c, vh * hd:(vh + 1) * hd] = out.astype(o_ref.dtype)
    for vh in range(n_v):
        state[vh] = states[vh]


def gdn_delta(proj, conv_w, gates, norm_w, batch, seq, tb=GDN_TB, n_qk=GDN_QK_PER_STEP):
    t = proj.shape[0]
    tb = min(tb, seq)
    nblk = seq // tb
    hd = HEAD_DIM
    qw = n_qk * hd
    vw = 2 * qw
    kq = GDN_QK_DIM // qw
    v0 = (2 * GDN_QK_DIM) // vw
    z0 = GDN_CONV_DIM // vw
    return pl.pallas_call(
        _gdn_kernel,
        out_shape=jax.ShapeDtypeStruct((t, GDN_V_DIM), BF16),
        grid=(batch, GDN_QK_HEADS // n_qk, nblk),
        in_specs=[
            pl.BlockSpec((tb, qw), lambda b, h, i: (b * nblk + i, h)),
            pl.BlockSpec((tb, qw), lambda b, h, i: (b * nblk + i, kq + h)),
            pl.BlockSpec((tb, vw), lambda b, h, i: (b * nblk + i, v0 + h)),
            pl.BlockSpec((tb, vw), lambda b, h, i: (b * nblk + i, z0 + h)),
            pl.BlockSpec((GDN_CONV, qw), lambda b, h, i: (0, h)),
            pl.BlockSpec((GDN_CONV, qw), lambda b, h, i: (0, kq + h)),
            pl.BlockSpec((GDN_CONV, vw), lambda b, h, i: (0, v0 + h)),
            pl.BlockSpec((1, tb, LANES), lambda b, h, i: (h, b * nblk + i, 0)),
            pl.BlockSpec((1, hd), lambda b, h, i: (0, 0)),
        ],
        out_specs=pl.BlockSpec((tb, vw), lambda b, h, i: (b * nblk + i, h)),
        scratch_shapes=[
            pltpu.VMEM((tb + SUBLANES, qw), F32),
            pltpu.VMEM((tb + SUBLANES, qw), F32),
            pltpu.VMEM((tb + SUBLANES, vw), F32),
            pltpu.VMEM((tb, qw), F32),
            pltpu.VMEM((tb, qw), F32),
            pltpu.VMEM((tb, vw), F32),
            pltpu.VMEM((2 * n_qk, hd, hd), F32),
        ],
        compiler_params=_params("parallel", "parallel", "arbitrary"),
        name="gdn_delta",
    )(proj, proj, proj, proj, conv_w, conv_w, conv_w, gates, norm_w.reshape(1, hd))


def gated_deltanet_layer(h32, h16, w_in, conv_w, a_log, dt_bias, norm_w, w_out, gain, bias, batch, seq):
    main = GDN_CONV_DIM + GDN_V_DIM
    proj = matmul(h16, w_in[:, :main].astype(BF16))
    gates = gdn_gates(h32, w_in[:, main:], a_log, dt_bias)
    o = gdn_delta(proj, conv_w, gates, norm_w, batch, seq)
    return proj_residual_ln(o, w_out.astype(BF16), h32, gain, bias)


def _dil_attn_kernel(q_ref, kp_ref, ko_ref, vp_ref, vo_ref, bias_ref, o_ref, lse_ref):
    lb = pl.program_id(2)
    blk = DIL_BLOCK
    qi = lax.broadcasted_iota(jnp.int32, (blk, blk), 0)
    kj = lax.broadcasted_iota(jnp.int32, (blk, blk), 1)
    mask_prev = (qi <= kj) & (lb > 0)
    mask_own = kj <= qi
    lane = lax.broadcasted_iota(jnp.int32, (blk, LANES), 1)
    lse_all = jnp.zeros((blk, LANES), F32)
    scale = HEAD_DIM ** -0.5
    heads = range(DIL_HEADS)
    sls = [slice(hd * HEAD_DIM, (hd + 1) * HEAD_DIM) for hd in heads]
    s_prev = [jnp.where(mask_prev, _dot_nt(q_ref[0, :, sl], kp_ref[0, :, sl]) * scale + bias_ref[hd][:, :blk], -jnp.inf)
              for hd, sl in zip(heads, sls)]
    s_own = [jnp.where(mask_own, _dot_nt(q_ref[0, :, sl], ko_ref[0, :, sl]) * scale + bias_ref[hd][:, blk:], -jnp.inf)
             for hd, sl in zip(heads, sls)]
    m = [jnp.maximum(jnp.max(sp, axis=-1, keepdims=True), jnp.max(so, axis=-1, keepdims=True))
         for sp, so in zip(s_prev, s_own)]
    p_prev = [jnp.exp(sp - mm) for sp, mm in zip(s_prev, m)]
    p_own = [jnp.exp(so - mm) for so, mm in zip(s_own, m)]
    den = [jnp.sum(pp, axis=-1, keepdims=True) + jnp.sum(po, axis=-1, keepdims=True) for pp, po in zip(p_prev, p_own)]
    o = [_dot_bf16(pp, vp_ref[0, :, sl]) + _dot_bf16(po, vo_ref[0, :, sl]) for pp, po, sl in zip(p_prev, p_own, sls)]
    for hd, sl in zip(heads, sls):
        o_ref[0, :, sl] = o[hd] / den[hd]
        lse_all = jnp.where(lane == hd, m[hd] + jnp.log(den[hd]), lse_all)
    lse_ref[0] = lse_all


def dilated_group_attention(q, k, v, bias, dil, batch, seq):
    length = seq // dil
    nb = length // DIL_BLOCK
    kvd = DIL_KV_DIM
    qv = q.reshape(batch, length, dil * kvd)
    kview = k.reshape(batch, length, dil * kvd)
    vview = v.reshape(batch, length, dil * kvd)
    blk = DIL_BLOCK
    own = lambda b, r, i: (b, i, r)
    prev = lambda b, r, i: (b, jnp.maximum(i - 1, 0), r)
    o, lse = pl.pallas_call(
        _dil_attn_kernel,
        out_shape=(jax.ShapeDtypeStruct((batch, length, dil * kvd), F32),
                   jax.ShapeDtypeStruct((batch, length, dil * LANES), F32)),
        grid=(batch, dil, nb),
        in_specs=[
            pl.BlockSpec((1, blk, kvd), own),
            pl.BlockSpec((1, blk, kvd), prev),
            pl.BlockSpec((1, blk, kvd), own),
            pl.BlockSpec((1, blk, kvd), prev),
            pl.BlockSpec((1, blk, kvd), own),
            pl.BlockSpec((DIL_HEADS, blk, 2 * blk), lambda b, r, i: (0, 0, 0)),
        ],
        out_specs=(pl.BlockSpec((1, blk, kvd), own), pl.BlockSpec((1, blk, LANES), own)),
        compiler_params=_params("parallel", "parallel", "parallel"),
        name=f"dilated_attention_d{dil}",
    )(qv, kview, kview, vview, vview, bias)
    return o.reshape(batch * seq, kvd), lse.reshape(batch * seq, LANES)


def _dil_out_kernel(o0_ref, o1_ref, o2_ref, l0_ref, l1_ref, l2_ref, w_ref, h_ref, gain_ref, bias_ref,
                    o32_ref, o16_ref, orow_ref, mix_ref):
    l0, l1, l2 = l0_ref[...], l1_ref[...], l2_ref[...]
    m = jnp.maximum(jnp.maximum(l0, l1), l2)
    e0, e1, e2 = jnp.exp(l0 - m), jnp.exp(l1 - m), jnp.exp(l2 - m)
    inv = 1.0 / (e0 + e1 + e2)
    w0, w1, w2 = e0 * inv, e1 * inv, e2 * inv
    for hd in range(DIL_HEADS):
        sl = slice(hd * HEAD_DIM, (hd + 1) * HEAD_DIM)
        mix_ref[:, sl] = (w0[:, hd:hd + 1] * o0_ref[:, sl] + w1[:, hd:hd + 1] * o1_ref[:, sl]
                          + w2[:, hd:hd + 1] * o2_ref[:, sl])
    mix = jnp.dot(mix_ref[...].astype(BF16), w_ref[...], preferred_element_type=F32)
    y = _layer_norm_rows(DEEPNORM_ALPHA * h_ref[...] + mix, gain_ref[...], bias_ref[...])
    o32_ref[...] = y
    o16_ref[...] = y.astype(BF16)
    _store_row_contiguous(orow_ref, y)


def dilated_out(outs, lses, w_o, h, gain, bias, tm=LN_TM):
    m, kvd = outs[0].shape
    d = w_o.shape[1]
    tm = min(tm, m)
    row = lambda i: (i, 0)
    fixed = lambda i: (0, 0)
    return pl.pallas_call(
        _dil_out_kernel,
        out_shape=(jax.ShapeDtypeStruct((m, d), F32), jax.ShapeDtypeStruct((m, d), BF16),
                   jax.ShapeDtypeStruct((m * (d // LANES), LANES), F32)),
        grid=(m // tm,),
        in_specs=[pl.BlockSpec((tm, kvd), row)] * 3 + [pl.BlockSpec((tm, LANES), row)] * 3 + [
            pl.BlockSpec((kvd, d), fixed), pl.BlockSpec((tm, d), row),
            pl.BlockSpec((1, d), fixed), pl.BlockSpec((1, d), fixed)],
        out_specs=(pl.BlockSpec((tm, d), row), pl.BlockSpec((tm, d), row), pl.BlockSpec((tm * (d // LANES), LANES), row)),
        scratch_shapes=[pltpu.VMEM((tm, kvd), F32)],
        compiler_params=_params("parallel"),
        name="dilated_out_ln",
    )(*outs, *lses, w_o, h, gain.reshape(1, d), bias.reshape(1, d))


def _t5_bucket(dist):
    exact = REL_BUCKETS // 2
    distf = jnp.maximum(dist, exact).astype(F32)
    large = exact + (jnp.log(distf / exact) / math.log(REL_MAX_DIST / exact)
                     * (REL_BUCKETS - exact)).astype(jnp.int32)
    return jnp.where(dist < exact, dist, jnp.minimum(large, REL_BUCKETS - 1))


def _band_bias(rel_bias_g, dil):
    qi = jnp.arange(DIL_BLOCK)[:, None]
    kj = jnp.arange(2 * DIL_BLOCK)[None, :]
    steps = DIL_BLOCK + qi - kj
    bias = rel_bias_g[_t5_bucket(jnp.maximum(steps, 0) * dil)]
    return jnp.transpose(bias, (2, 0, 1)).astype(F32)


def _router_kernel(h_ref, w_ref, b_ref, o_ref, cnt_ref, running):
    @pl.when(pl.program_id(0) == 0)
    def _():
        running[...] = jnp.zeros(running.shape, F32)

    logits = _dot_split(h_ref[...], w_ref) + b_ref[...]
    lane = lax.broadcasted_iota(jnp.int32, logits.shape, 1)
    neg = -jnp.inf
    big = jnp.int32(LANES)
    is_group = lane < MOE_GROUPS
    gl = jnp.where(is_group, logits, neg)
    gmax = jnp.max(gl, axis=-1, keepdims=True)
    gidx = jnp.min(jnp.where(gl == gmax, lane, big), axis=-1, keepdims=True)
    gden = jnp.sum(jnp.exp(gl - gmax), axis=-1, keepdims=True)
    group_gate = 1.0 / gden
    lo = MOE_GROUPS + gidx * MOE_EXPERTS_PER_GROUP
    in_group = (lane >= lo) & (lane < lo + MOE_EXPERTS_PER_GROUP)
    ll = jnp.where(in_group, logits, neg)
    m1 = jnp.max(ll, axis=-1, keepdims=True)
    i1 = jnp.min(jnp.where(ll == m1, lane, big), axis=-1, keepdims=True)
    rest = jnp.where(lane == i1, neg, ll)
    m2 = jnp.max(rest, axis=-1, keepdims=True)
    i2 = jnp.min(jnp.where(rest == m2, lane, big), axis=-1, keepdims=True)
    den = jnp.sum(jnp.exp(ll - m1), axis=-1, keepdims=True)
    p1 = 1.0 / den
    p2 = jnp.exp(m2 - m1) / den
    g1 = group_gate * p1 / (p1 + p2)
    g2 = group_gate * p2 / (p1 + p2)
    id1 = (i1 - MOE_GROUPS).astype(F32)
    id2 = (i2 - MOE_GROUPS).astype(F32)
    tm = logits.shape[0]
    oh1 = (lane == i1).astype(F32)
    oh2 = (lane == i2).astype(F32)
    both = oh1 + oh2
    r_i = lax.broadcasted_iota(jnp.int32, (tm, tm), 0)
    c_i = lax.broadcasted_iota(jnp.int32, (tm, tm), 1)
    before = jnp.dot((c_i < r_i).astype(BF16), both.astype(BF16), preferred_element_type=F32) + running[...]
    rank1 = jnp.sum(before * oh1, axis=-1, keepdims=True)
    rank2 = jnp.sum(before * oh2, axis=-1, keepdims=True)
    running[...] = running[...] + jnp.sum(both, axis=0, keepdims=True)
    cnt_ref[...] = running[...]
    o_ref[...] = jnp.where(lane == 0, id1, jnp.where(lane == 1, id2, jnp.where(lane == 2, g1, jnp.where(
        lane == 3, g2, jnp.where(lane == 4, rank1, rank2)))))


def moe_router(h, w_group, b_group, w_expert, b_expert, tm=ROUTER_TM):
    m, d = h.shape
    tm = min(tm, m)
    pad = LANES - MOE_GROUPS - MOE_EXPERTS
    w = jnp.pad(jnp.concatenate([w_group, w_expert], axis=1), ((0, 0), (0, pad)))
    b = jnp.pad(jnp.concatenate([b_group, b_expert]), (0, pad)).reshape(1, LANES)
    out, cnt = pl.pallas_call(
        _router_kernel,
        out_shape=(jax.ShapeDtypeStruct((m, LANES), F32), jax.ShapeDtypeStruct((1, LANES), F32)),
        grid=(m // tm,),
        in_specs=[pl.BlockSpec((tm, d), lambda i: (i, 0)),
                  pl.BlockSpec((d, 2 * LANES), lambda i: (0, 0)),
                  pl.BlockSpec((1, LANES), lambda i: (0, 0))],
        out_specs=(pl.BlockSpec((tm, LANES), lambda i: (i, 0)), pl.BlockSpec((1, LANES), lambda i: (0, 0))),
        scratch_shapes=[pltpu.VMEM((1, LANES), F32)],
        compiler_params=_params("arbitrary"),
        name="moe_router",
    )(h, _hi_lo_columns(w), b)
    k = MOE_TOP_K
    counts = cnt[0, MOE_GROUPS:MOE_GROUPS + MOE_EXPERTS].astype(jnp.int32)
    return out[:, :k].astype(jnp.int32), out[:, k:2 * k], out[:, 2 * k:3 * k].astype(jnp.int32), counts


def _gather_rows_start(src_hbm, idx_ref, idx_base, buf, sem, unrolled, row_tiles=None):
    if row_tiles is None:
        groups = buf.shape[0]

        def copy(g, j):
            pltpu.make_async_copy(src_hbm.at[pl.ds(idx_ref[idx_base + g * SUBLANES + j], 1)],
                                  buf.at[g, pl.ds(j, 1)], sem).start()
    else:
        w = row_tiles
        groups = buf.shape[0] // (w * SUBLANES)

        def copy(g, j):
            r = g * SUBLANES + j
            pltpu.make_async_copy(src_hbm.at[pl.ds(pl.multiple_of(idx_ref[idx_base + r], w), w)],
                                  buf.at[pl.ds(pl.multiple_of(r * w, w), w)], sem).start()

    if unrolled:
        for g in range(groups):
            for j in range(SUBLANES):
                copy(g, j)
    else:
        def body(g, carry):
            for j in range(SUBLANES):
                copy(g, j)
            return carry
        lax.fori_loop(0, groups, body, 0)


def _gather_rows_wait(buf, sem):
    pltpu.make_async_copy(buf, buf, sem).wait()


def _gathered_blocks(src_hbm, idx_ref, bufs, sems, compute, row_tiles=None):
    i = pl.program_id(0)
    n = pl.num_programs(0)
    if row_tiles is None:
        rows = bufs[0].shape[0] * bufs[0].shape[1]
        load = lambda buf: buf[...].reshape(rows, buf.shape[2])
    else:
        rows = bufs[0].shape[0] // row_tiles
        load = lambda buf: _load_row_contiguous(buf, rows)

    @pl.when(i == 0)
    def _():
        _gather_rows_start(src_hbm, idx_ref, 0, bufs[0], sems.at[0], False, row_tiles)

    def step(s):
        _gather_rows_wait(bufs[s], sems.at[s])
        _gather_rows_start(src_hbm, idx_ref, jnp.minimum(i + 1, n - 1) * rows, bufs[1 - s], sems.at[1 - s],
                           True, row_tiles)
        compute(load(bufs[s]))

        @pl.when(i == n - 1)
        def _():
            _gather_rows_wait(bufs[1 - s], sems.at[1 - s])

    for s in range(2):
        pl.when(i % 2 == s)(functools.partial(step, s))


def _expert_kernel(bexp_ref, rtok_ref, nused_ref, h_hbm, w1_ref, w3_ref, w2_ref, y_ref,
                   xbuf_a, xbuf_b, sems, wb1, wb3, wb2):
    i = pl.program_id(0)

    @pl.when((i < nused_ref[0]) & ((i == 0) | (bexp_ref[i] != bexp_ref[jnp.maximum(i - 1, 0)])))
    def _():
        wb1[...] = w1_ref[0, 0].astype(BF16)
        wb3[...] = w3_ref[0, 0].astype(BF16)
        wb2[...] = w2_ref[0, 0].astype(BF16)

    def compute(x):
        x = x.astype(BF16)
        h1 = jnp.dot(x, wb1[...], preferred_element_type=F32)
        h3 = jnp.dot(x, wb3[...], preferred_element_type=F32)
        hmid = (h1 * jax.nn.sigmoid(h1) * h3).astype(BF16)
        y_ref[...] = jnp.dot(hmid, wb2[...], preferred_element_type=F32)

    _gathered_blocks(h_hbm, rtok_ref, (xbuf_a, xbuf_b), sems, compute, row_tiles=w1_ref.shape[2] // LANES)


def moe_experts(h_rows, row_token, block_expert, n_used, w1, w3, w2, layer, tm=MOE_TM):
    n_rows = row_token.shape[0]
    n_blocks = n_rows // tm
    d = w1.shape[2]
    row_tiles = d // LANES
    ff = w1.shape[3]
    grid_spec = pltpu.PrefetchScalarGridSpec(
        num_scalar_prefetch=3,
        grid=(n_blocks,),
        in_specs=[
            pl.BlockSpec(memory_space=pl.ANY),
            pl.BlockSpec((1, 1, d, ff), lambda i, be, rt, nu: (layer, be[i], 0, 0)),
            pl.BlockSpec((1, 1, d, ff), lambda i, be, rt, nu: (layer, be[i], 0, 0)),
            pl.BlockSpec((1, 1, ff, d), lambda i, be, rt, nu: (layer, be[i], 0, 0)),
        ],
        out_specs=pl.BlockSpec((tm, d), lambda i, be, rt, nu: (i, 0)),
        scratch_shapes=[pltpu.VMEM((tm * row_tiles, LANES), F32), pltpu.VMEM((tm * row_tiles, LANES), F32),
                        pltpu.SemaphoreType.DMA((2,)),
                        pltpu.VMEM((d, ff), BF16), pltpu.VMEM((d, ff), BF16), pltpu.VMEM((ff, d), BF16)],
    )
    return pl.pallas_call(
        _expert_kernel,
        out_shape=jax.ShapeDtypeStruct((n_rows, d), F32),
        grid_spec=grid_spec,
        compiler_params=_params("arbitrary"),
        name="moe_experts",
    )(block_expert, row_token, n_used, h_rows, w1, w3, w2)


def _combine_kernel(dest_ref, y_hbm, gate_ref, h_ref, gain_ref, bias_ref, o32_ref, o16_ref, ybuf_a, ybuf_b, sems):
    tm = h_ref.shape[0]

    def compute(yy):
        gates = gate_ref[...]
        ffn = yy[0:tm, :] * gates[:, 0:1] + yy[tm:, :] * gates[:, 1:2]
        y = _layer_norm_rows(DEEPNORM_ALPHA * h_ref[...] + ffn, gain_ref[...], bias_ref[...])
        o32_ref[...] = y
        o16_ref[...] = y.astype(BF16)

    _gathered_blocks(y_hbm, dest_ref, (ybuf_a, ybuf_b), sems, compute)


def moe_combine_ln(y_rows, dest, gates, h, gain, bias, tm=COMBINE_TM):
    m, d = h.shape
    tm = min(tm, m)
    grid_spec = pltpu.PrefetchScalarGridSpec(
        num_scalar_prefetch=1,
        grid=(m // tm,),
        in_specs=[
            pl.BlockSpec(memory_space=pl.ANY),
            pl.BlockSpec((tm, MOE_TOP_K), lambda i, de: (i, 0)),
            pl.BlockSpec((tm, d), lambda i, de: (i, 0)),
            pl.BlockSpec((1, d), lambda i, de: (0, 0)),
            pl.BlockSpec((1, d), lambda i, de: (0, 0)),
        ],
        out_specs=(pl.BlockSpec((tm, d), lambda i, de: (i, 0)),
                   pl.BlockSpec((tm, d), lambda i, de: (i, 0))),
        scratch_shapes=[pltpu.VMEM((MOE_TOP_K * tm // SUBLANES, SUBLANES, d), F32),
                        pltpu.VMEM((MOE_TOP_K * tm // SUBLANES, SUBLANES, d), F32), pltpu.SemaphoreType.DMA((2,))],
    )
    return pl.pallas_call(
        _combine_kernel,
        out_shape=(jax.ShapeDtypeStruct((m, d), F32), jax.ShapeDtypeStruct((m, d), BF16)),
        grid_spec=grid_spec,
        compiler_params=_params("arbitrary"),
        name="moe_combine_ln",
    )(dest.reshape(-1), y_rows, gates, h, gain.reshape(1, d), bias.reshape(1, d))


def moe_layer(h32, h_rows, w_group, b_group, w_expert, b_expert, w1, w3, w2, layer, gain, bias, tm=MOE_TM):
    t = h32.shape[0]
    ids, gates, ranks, counts = moe_router(h32, w_group, b_group, w_expert, b_expert)
    eid = ids.reshape(-1)
    n_pairs = eid.shape[0]
    padded = (counts + tm - 1) // tm * tm
    seg_end = jnp.cumsum(padded)
    seg_start = seg_end - padded
    onehot = eid[:, None] == jnp.arange(MOE_EXPERTS, dtype=jnp.int32)[None, :]
    dest = (jnp.sum(jnp.where(onehot, seg_start[None, :], 0), axis=1) + ranks.reshape(-1)).astype(jnp.int32)
    n_blocks = -(-n_pairs // tm) + MOE_EXPERTS
    n_rows = n_blocks * tm
    token_of_pair = (jnp.arange(n_pairs, dtype=jnp.int32) // MOE_TOP_K)
    row_token = jnp.zeros((n_rows,), jnp.int32).at[dest].set(token_of_pair * (h32.shape[1] // LANES))
    block_start = jnp.arange(n_blocks, dtype=jnp.int32) * tm
    block_expert = jnp.minimum(jnp.sum((seg_end[None, :] <= block_start[:, None]).astype(jnp.int32), axis=1),
                               MOE_EXPERTS - 1)
    n_used = (seg_end[-1:] // tm).astype(jnp.int32)
    y_rows = moe_experts(h_rows, row_token, block_expert, n_used, w1, w3, w2, layer, tm=tm)
    ctm = min(COMBINE_TM, t)
    dest_tiles = jnp.transpose(dest.reshape(t // ctm, ctm, MOE_TOP_K), (0, 2, 1))
    return moe_combine_ln(y_rows, dest_tiles, gates, h32, gain, bias, tm=ctm)


def kernel(x, gdn_w_in, gdn_conv, gdn_a_log, gdn_dt_bias, gdn_norm, gdn_w_out, kv_w_k, kv_w_v, dil_w_q, dil_w_o,
           rel_bias, ln_gain, ln_bias, moe_w_group, moe_b_group, moe_w_expert, moe_b_expert, moe_w1, moe_w3, moe_w2):
    batch, seq, d = x.shape
    t = batch * seq
    h32 = x.reshape(t, d)
    h16 = h32.astype(BF16)
    k = v = None
    biases = None
    for layer in range(DEPTH):
        if layer < N_A_LAYERS:
            h32, h16, h_rows = gated_deltanet_layer(
                h32, h16, gdn_w_in[layer], gdn_conv[layer], gdn_a_log[layer], gdn_dt_bias[layer],
                gdn_norm[layer], gdn_w_out[layer], ln_gain[layer, 0], ln_bias[layer, 0], batch, seq)
        else:
            if layer == N_A_LAYERS:
                k = matmul(h16, kv_w_k.astype(BF16), out_dtype=BF16)
                v = matmul(h16, kv_w_v.astype(BF16), out_dtype=BF16)
                biases = [_band_bias(rel_bias[:, gi * DIL_HEADS:(gi + 1) * DIL_HEADS], dil)
                          for gi, (_, dil) in enumerate(DIL_CONFIGS)]
            j = layer - N_A_LAYERS
            outs, lses = [], []
            for gi, (_, dil) in enumerate(DIL_CONFIGS):
                q = matmul(h16, dil_w_q[j][:, gi * DIL_KV_DIM:(gi + 1) * DIL_KV_DIM].astype(BF16), out_dtype=BF16)
                o, lse = dilated_group_attention(q, k, v, biases[gi], dil, batch, seq)
                outs.append(o)
                lses.append(lse)
            h32, h16, h_rows = dilated_out(outs, lses, dil_w_o[j].astype(BF16), h32, ln_gain[layer, 0], ln_bias[layer, 0])
        h32, h16 = moe_layer(h32, h_rows, moe_w_group[layer], moe_b_group[layer], moe_w_expert[layer], moe_b_expert[layer],
                             moe_w1, moe_w3, moe_w2, layer, ln_gain[layer, 1], ln_bias[layer, 1])
    return h32.reshape(batch, seq, d)
```

```python
import functools
import math

import jax
import jax.numpy as jnp
from jax import lax
from jax.experimental import pallas as pl
from jax.experimental.pallas import tpu as pltpu

F32 = jnp.float32
BF16 = jnp.bfloat16

DEPTH = 4
N_A_LAYERS = DEPTH // 2
HEAD_DIM = 128
GDN_QK_HEADS = 16
GDN_V_HEADS = 32
GDN_CONV = 4
GDN_CHUNK = 64
GDN_QK_DIM = GDN_QK_HEADS * HEAD_DIM
GDN_V_DIM = GDN_V_HEADS * HEAD_DIM
GDN_CONV_DIM = 2 * GDN_QK_DIM + GDN_V_DIM
DIL_CONFIGS = ((128, 1), (512, 4), (2048, 16))
N_DIL = len(DIL_CONFIGS)
DIL_HEADS = 8
DIL_BLOCK = 128
DIL_KV_DIM = DIL_HEADS * HEAD_DIM
REL_BUCKETS = 32
REL_MAX_DIST = 2048
MOE_GROUPS = 4
MOE_EXPERTS_PER_GROUP = 8
MOE_EXPERTS = MOE_GROUPS * MOE_EXPERTS_PER_GROUP
MOE_TOP_K = 2
DEEPNORM_ALPHA = (2 * DEPTH) ** 0.25
LN_EPS = 1e-5
RMS_EPS = 1e-6
L2_EPS = 1e-6

LANES = 128
SUBLANES = 8
VMEM_LIMIT_BYTES = 56 * 1024 * 1024

MM_TM = 512
MM_TN = 1024
LN_TM = 256
GATE_TM = 512
GDN_TB = 512
GDN_QK_PER_STEP = 4
MOE_TM = 256
ROUTER_TM = 512
COMBINE_TM = 256


def _params(*semantics):
    return pltpu.CompilerParams(dimension_semantics=semantics, vmem_limit_bytes=VMEM_LIMIT_BYTES)


def _split_bf16(x):
    hi = x.astype(BF16)
    return hi, (x - hi.astype(F32)).astype(BF16)


def _dot_split(x, w2_ref):
    n = w2_ref.shape[1] // 2
    hi, lo = _split_bf16(x)
    a = jnp.dot(hi, w2_ref[...], preferred_element_type=F32)
    b = jnp.dot(lo, w2_ref[:, :n], preferred_element_type=F32)
    return a[:, :n] + a[:, n:] + b


def _hi_lo_columns(w):
    hi = w.astype(BF16)
    return jnp.concatenate([hi, (w - hi.astype(F32)).astype(BF16)], axis=1)


def _store_row_contiguous(ref, y):
    rows, width = y.shape
    w = width // LANES
    for c in range(w):
        ref[pl.ds(c, rows, stride=w), :] = y[:, c * LANES:(c + 1) * LANES]


def _load_row_contiguous(ref, rows):
    w = ref.shape[0] // rows
    return jnp.concatenate([ref[pl.ds(c, rows, stride=w), :] for c in range(w)], axis=1)


def _layer_norm_rows(y, gain, bias):
    mu = jnp.mean(y, axis=-1, keepdims=True)
    yc = y - mu
    var = jnp.mean(yc * yc, axis=-1, keepdims=True)
    return yc * lax.rsqrt(var + LN_EPS) * gain + bias


def _matmul_kernel(a_ref, b_ref, o_ref):
    o_ref[...] = jnp.dot(a_ref[...].astype(BF16), b_ref[...],
                         preferred_element_type=F32).astype(o_ref.dtype)


def matmul(a, b, out_dtype=F32, tm=MM_TM, tn=MM_TN):
    m, k = a.shape
    n = b.shape[1]
    tm, tn = min(tm, m), min(tn, n)
    assert m % tm == 0 and n % tn == 0
    return pl.pallas_call(
        _matmul_kernel,
        out_shape=jax.ShapeDtypeStruct((m, n), out_dtype),
        grid=(n // tn, m // tm),
        in_specs=[pl.BlockSpec((tm, k), lambda j, i: (i, 0)),
                  pl.BlockSpec((k, tn), lambda j, i: (0, j))],
        out_specs=pl.BlockSpec((tm, tn), lambda j, i: (i, j)),
        compiler_params=_params("parallel", "parallel"),
        name="matmul",
    )(a, b)


def _proj_ln_kernel(a_ref, w_ref, h_ref, gain_ref, bias_ref, o32_ref, o16_ref, orow_ref):
    mix = jnp.dot(a_ref[...].astype(BF16), w_ref[...], preferred_element_type=F32)
    y = _layer_norm_rows(DEEPNORM_ALPHA * h_ref[...] + mix, gain_ref[...], bias_ref[...])
    o32_ref[...] = y
    o16_ref[...] = y.astype(BF16)
    _store_row_contiguous(orow_ref, y)


def proj_residual_ln(a, w, h, gain, bias, tm=LN_TM):
    m, k = a.shape
    d = w.shape[1]
    tm = min(tm, m)
    assert m % tm == 0
    return pl.pallas_call(
        _proj_ln_kernel,
        out_shape=(jax.ShapeDtypeStruct((m, d), F32), jax.ShapeDtypeStruct((m, d), BF16),
                   jax.ShapeDtypeStruct((m * (d // LANES), LANES), F32)),
        grid=(m // tm,),
        in_specs=[pl.BlockSpec((tm, k), lambda i: (i, 0)),
                  pl.BlockSpec((k, d), lambda i: (0, 0)),
                  pl.BlockSpec((tm, d), lambda i: (i, 0)),
                  pl.BlockSpec((1, d), lambda i: (0, 0)),
                  pl.BlockSpec((1, d), lambda i: (0, 0))],
        out_specs=(pl.BlockSpec((tm, d), lambda i: (i, 0)),
                   pl.BlockSpec((tm, d), lambda i: (i, 0)),
                   pl.BlockSpec((tm * (d // LANES), LANES), lambda i: (i, 0))),
        compiler_params=_params("parallel"),
        name="proj_residual_ln",
    )(a, w, h, gain.reshape(1, d), bias.reshape(1, d))


def _gdn_gate_kernel(h_ref, w_ref, alog_ref, dt_ref, o_ref):
    tm = h_ref.shape[0]
    n_groups = o_ref.shape[0]
    width = 2 * GDN_V_HEADS // n_groups
    ba = _dot_split(h_ref[...], w_ref)
    xa = ba + dt_ref[...]
    softplus = jnp.maximum(xa, 0.0) + jnp.log(1.0 + jnp.exp(-jnp.abs(xa)))
    g = -jnp.exp(alog_ref[...]) * softplus
    row = lax.broadcasted_iota(jnp.int32, (tm, tm), 0)
    col = lax.broadcasted_iota(jnp.int32, (tm, tm), 1)
    tri = ((row // GDN_CHUNK == col // GDN_CHUNK) & (col <= row)).astype(BF16)
    g_hi, g_lo = _split_bf16(g)
    gc2 = jnp.dot(tri, jnp.concatenate([g_hi, g_lo], axis=1), preferred_element_type=F32)
    gc = gc2[:, :LANES] + gc2[:, LANES:]
    lane = lax.broadcasted_iota(jnp.int32, ba.shape, 1)
    slab = jnp.where(lane % width < width // 2, jax.nn.sigmoid(ba), gc)
    for r in range(n_groups):
        o_ref[r] = slab if r == 0 else pltpu.roll(slab, LANES - width * r, axis=1)


def gdn_gates(h, w_ba, a_log, dt_bias, n_qk=GDN_QK_PER_STEP, tm=GATE_TM):
    m, d = h.shape
    tm = min(tm, m)
    v = 2 * n_qk
    n_groups = GDN_V_HEADS // v
    cols = [(v * (l // (2 * v)) + l % (2 * v)) if l % (2 * v) < v else (GDN_V_HEADS + v * (l // (2 * v)) + l % (2 * v) - v)
            for l in range(2 * GDN_V_HEADS)]
    is_decay = jnp.array([l % (2 * v) >= v for l in range(2 * GDN_V_HEADS)])
    head = jnp.array([c % GDN_V_HEADS for c in cols], jnp.int32)
    pad = LANES - 2 * GDN_V_HEADS
    w = jnp.pad(w_ba[:, jnp.array(cols, jnp.int32)], ((0, 0), (0, pad)))
    alog = jnp.pad(jnp.where(is_decay, a_log[head], 0.0), (0, pad)).reshape(1, LANES)
    dt = jnp.pad(jnp.where(is_decay, dt_bias[head], 0.0), (0, pad)).reshape(1, LANES)
    return pl.pallas_call(
        _gdn_gate_kernel,
        out_shape=jax.ShapeDtypeStruct((n_groups, m, LANES), F32),
        grid=(m // tm,),
        in_specs=[pl.BlockSpec((tm, d), lambda i: (i, 0)),
                  pl.BlockSpec((d, 2 * LANES), lambda i: (0, 0)),
                  pl.BlockSpec((1, LANES), lambda i: (0, 0)),
                  pl.BlockSpec((1, LANES), lambda i: (0, 0))],
        out_specs=pl.BlockSpec((n_groups, tm, LANES), lambda i: (0, i, 0)),
        compiler_params=_params("parallel"),
        name="gdn_gates",
    )(h, _hi_lo_columns(w), alog, dt)


def _dot_bf16(a, b):
    return jnp.dot(a.astype(BF16), b.astype(BF16), preferred_element_type=F32)


def _dot_nt(a, b):
    return lax.dot_general(a.astype(BF16), b.astype(BF16), (((1,), (1,)), ((), ())),
                           preferred_element_type=F32)


def _dot_tn(a, b):
    return lax.dot_general(a.astype(BF16), b.astype(BF16), (((0,), (0,)), ((), ())),
                           preferred_element_type=F32)


def _block_diag2(x0, x1):
    z = jnp.zeros_like(x0)
    return jnp.concatenate([jnp.concatenate([x0, z], axis=1), jnp.concatenate([z, x1], axis=1)], axis=0)


def _gdn_kernel(q_ref, k_ref, v_ref, z_ref, wq_ref, wk_ref, wv_ref, g_ref, nw_ref,
                o_ref, qx, kx, vx, qn, kn, vn, state):
    blk = pl.program_id(2)
    tb = q_ref.shape[0]
    n_chunks = tb // GDN_CHUNK
    n_qk = q_ref.shape[1] // HEAD_DIM
    c = GDN_CHUNK
    hd = HEAD_DIM
    tail = SUBLANES

    @pl.when(blk == 0)
    def _():
        qx[0:tail, :] = jnp.zeros((tail, qx.shape[1]), F32)
        kx[0:tail, :] = jnp.zeros((tail, kx.shape[1]), F32)
        vx[0:tail, :] = jnp.zeros((tail, vx.shape[1]), F32)
        state[...] = jnp.zeros(state.shape, F32)

    def conv_silu(x_ref, ext, w_ref):
        ext[tail:tail + tb, :] = x_ref[...].astype(F32)
        y = None
        for j in range(GDN_CONV):
            term = w_ref[j:j + 1, :] * ext[pl.ds(tail - (GDN_CONV - 1) + j, tb), :]
            y = term if y is None else y + term
        ext[0:tail, :] = ext[tb:tb + tail, :]
        return y * jax.nn.sigmoid(y)

    def l2n(x):
        heads = [x[:, g * hd:(g + 1) * hd] for g in range(n_qk)]
        return jnp.concatenate([xh * lax.rsqrt(jnp.sum(xh * xh, axis=-1, keepdims=True) + L2_EPS) for xh in heads], axis=1)

    qn[...] = l2n(conv_silu(q_ref, qx, wq_ref)) * (HEAD_DIM ** -0.5)
    kn[...] = l2n(conv_silu(k_ref, kx, wk_ref))
    vn[...] = conv_silu(v_ref, vx, wv_ref)

    row = lax.broadcasted_iota(jnp.int32, (c, 2 * c), 0)
    lane = lax.broadcasted_iota(jnp.int32, (c, 2 * c), 1)
    col = lane & (c - 1)
    head0 = lane < c
    causal = col <= row
    strict = col < row
    in_block = (row // 16) == (col // 16)
    eye = (row == col).astype(F32)
    norm_w = nw_ref[...]

    def mm(x, y):
        yb = jnp.concatenate([jnp.where(head0, y, 0.0), jnp.where(head0, 0.0, y)], axis=0)
        return _dot_bf16(x, yb)

    units = [(ci, g) for ci in range(n_chunks) for g in range(n_qk)]
    each = lambda f, *lists: [f(*xs) for xs in zip(*lists)]
    qcs = [qn[ci * c:(ci + 1) * c, g * hd:(g + 1) * hd] for ci, g in units]
    kcs = [kn[ci * c:(ci + 1) * c, g * hd:(g + 1) * hd] for ci, g in units]
    n_v = 2 * n_qk
    gates = g_ref[0]
    gates_t = gates.T
    betas = [[gates[ci * c:(ci + 1) * c, 2 * g + hh:2 * g + hh + 1] for hh in range(2)] for ci, g in units]
    gcs = [[gates[ci * c:(ci + 1) * c, n_v + 2 * g + hh:n_v + 2 * g + hh + 1] for hh in range(2)] for ci, g in units]

    def packed_row(ci, g):
        w0 = (ci // 2) * 2 * c
        r0, r1 = (gates_t[n_v + 2 * g + hh:n_v + 2 * g + hh + 1, w0:w0 + 2 * c] for hh in range(2))
        if ci % 2 == 0:
            return jnp.where(head0[0:1], r0, pltpu.roll(r1, c, axis=1))
        return jnp.where(head0[0:1], pltpu.roll(r0, c, axis=1), r1)

    grps = [packed_row(ci, g) for ci, g in units]
    egs = [[jnp.exp(g[hh]) for hh in range(2)] for g in gcs]
    glast = [[g[hh][c - 1:c, :] for hh in range(2)] for g in gcs]
    both = each(lambda q, k: _dot_nt(jnp.concatenate([q, k], axis=0), jnp.concatenate([k, k], axis=0)), qcs, kcs)
    decayp = each(lambda g, gr: jnp.exp(jnp.where(causal, jnp.where(head0, g[0], g[1]) - gr, -jnp.inf)), gcs, grps)
    ap = each(lambda b, bo, dc: jnp.where(strict, jnp.where(head0, b[0], b[1]) * bo[c:] * dc, 0.0), betas, both, decayp)
    qkd = each(lambda bo, dc: bo[:c] * dc, both, decayp)
    d = each(lambda a: jnp.where(in_block, a, 0.0), ap)
    e = each(lambda a, dd: a - dd, ap, d)
    n1 = each(lambda dd: -dd, d)
    p = each(lambda n: eye + n, n1)
    n2 = each(mm, n1, n1)
    p = each(lambda pp, n: pp + mm(pp, n), p, n2)
    n4 = each(mm, n2, n2)
    p = each(lambda pp, n: pp + mm(pp, n), p, n4)
    n8 = each(mm, n4, n4)
    dinv = each(lambda pp, n: pp + mm(pp, n), p, n8)
    nb = each(mm, dinv, e)
    nb2 = each(mm, nb, nb)
    qq = each(lambda n: eye - n, nb)
    qq = each(lambda x, n: x + mm(x, n), qq, nb2)
    tinvp = each(mm, qq, dinv)
    rhs = [[jnp.concatenate([vn[ci * c:(ci + 1) * c, (2 * g + hh) * hd:(2 * g + hh + 1) * hd] * betas[u][hh],
                             kcs[u] * (betas[u][hh] * egs[u][hh])], axis=1) for hh in range(2)]
           for u, (ci, g) in enumerate(units)]
    uw = each(lambda t, r: _dot_bf16(t, _block_diag2(r[0], r[1])), tinvp, rhs)
    wu = [[jnp.concatenate([x[:, (2 * hh + 1) * hd:(2 * hh + 2) * hd], x[:, 2 * hh * hd:(2 * hh + 1) * hd]], axis=1)
           for hh in range(2)] for x in uw]
    qo = each(lambda x, w2: _dot_bf16(x, _block_diag2(w2[0], w2[1])), qkd, wu)
    pre = {}
    for u, (ci, g) in enumerate(units):
        for hh in range(2):
            rb = _dot_tn(kcs[u] * jnp.exp(glast[u][hh] - gcs[u][hh]), wu[u][hh])
            q_eff = qcs[u] * egs[u][hh] - qo[u][:, 2 * hh * hd:(2 * hh + 1) * hd]
            rq = jnp.concatenate([rb[:, :hd], q_eff], axis=0).astype(BF16)
            pre[ci, 2 * g + hh] = (rq, rb[:, hd:], qo[u][:, (2 * hh + 1) * hd:(2 * hh + 2) * hd], jnp.exp(glast[u][hh]))

    states = [state[vh] for vh in range(n_v)]
    for ci in range(n_chunks):
        r0 = ci * c
        for vh in range(n_v):
            rq, b_c, o_c, cd = pre[ci, vh]
            s = states[vh]
            res = jnp.dot(rq, s.astype(BF16), preferred_element_type=F32)
            states[vh] = s * cd - res[:hd] + b_c
            out = res[hd:] + o_c
            out = out * lax.rsqrt(jnp.mean(out * out, axis=-1, keepdims=True) + RMS_EPS) * norm_w
            zc = z_ref[r0:r0 + c, vh * hd:(vh + 1) * hd].astype(F32)
            out = out * (zc * jax.nn.sigmoid(zc))
            o_ref[r0:r0 + c, vh * hd:(vh + 1) * hd] = out.astype(o_ref.dtype)
    for vh in range(n_v):
        state[vh] = states[vh]


def gdn_delta(proj, conv_w, gates, norm_w, batch, seq, tb=GDN_TB, n_qk=GDN_QK_PER_STEP):
    t = proj.shape[0]
    tb = min(tb, seq)
    nblk = seq // tb
    hd = HEAD_DIM
    qw = n_qk * hd
    vw = 2 * qw
    kq = GDN_QK_DIM // qw
    v0 = (2 * GDN_QK_DIM) // vw
    z0 = GDN_CONV_DIM // vw
    return pl.pallas_call(
        _gdn_kernel,
        out_shape=jax.ShapeDtypeStruct((t, GDN_V_DIM), BF16),
        grid=(batch, GDN_QK_HEADS // n_qk, nblk),
        in_specs=[
            pl.BlockSpec((tb, qw), lambda b, h, i: (b * nblk + i, h)),
            pl.BlockSpec((tb, qw), lambda b, h, i: (b * nblk + i, kq + h)),
            pl.BlockSpec((tb, vw), lambda b, h, i: (b * nblk + i, v0 + h)),
            pl.BlockSpec((tb, vw), lambda b, h, i: (b * nblk + i, z0 + h)),
            pl.BlockSpec((GDN_CONV, qw), lambda b, h, i: (0, h)),
            pl.BlockSpec((GDN_CONV, qw), lambda b, h, i: (0, kq + h)),
            pl.BlockSpec((GDN_CONV, vw), lambda b, h, i: (0, v0 + h)),
            pl.BlockSpec((1, tb, LANES), lambda b, h, i: (h, b * nblk + i, 0)),
            pl.BlockSpec((1, hd), lambda b, h, i: (0, 0)),
        ],
        out_specs=pl.BlockSpec((tb, vw), lambda b, h, i: (b * nblk + i, h)),
        scratch_shapes=[
            pltpu.VMEM((tb + SUBLANES, qw), F32),
            pltpu.VMEM((tb + SUBLANES, qw), F32),
            pltpu.VMEM((tb + SUBLANES, vw), F32),
            pltpu.VMEM((tb, qw), F32),
            pltpu.VMEM((tb, qw), F32),
            pltpu.VMEM((tb, vw), F32),
            pltpu.VMEM((2 * n_qk, hd, hd), F32),
        ],
        compiler_params=_params("parallel", "parallel", "arbitrary"),
        name="gdn_delta",
    )(proj, proj, proj, proj, conv_w, conv_w, conv_w, gates, norm_w.reshape(1, hd))


def gated_deltanet_layer(h32, h16, w_in, conv_w, a_log, dt_bias, norm_w, w_out, gain, bias, batch, seq):
    main = GDN_CONV_DIM + GDN_V_DIM
    proj = matmul(h16, w_in[:, :main].astype(BF16))
    gates = gdn_gates(h32, w_in[:, main:], a_log, dt_bias)
    o = gdn_delta(proj, conv_w, gates, norm_w, batch, seq)
    return proj_residual_ln(o, w_out.astype(BF16), h32, gain, bias)


DIL_STEP_TOKENS = {1: 512, 4: 512, 16: 2048}
DIL_STEP_HEADS = {1: 8, 4: 8, 16: 2}
DIL_CHAINS = 8


def _dil_attn_kernel(q_ref, kp_ref, ko_ref, vp_ref, vo_ref, bias_ref, o_ref, lse_ref, qs, ks, vs, os_, *, dil):
    blk = DIL_BLOCK
    span = q_ref.shape[0]
    hp = q_ref.shape[1] // HEAD_DIM
    halo = dil * blk
    m_blocks = span // halo
    first_span = pl.program_id(1) == 0
    hg = pl.program_id(2)
    for h in range(hp):
        s = slice(h * HEAD_DIM, (h + 1) * HEAD_DIM)
        qs[h] = q_ref[:, s].astype(F32)
        ks[h, 0:halo, :] = kp_ref[:, s].astype(F32)
        ks[h, halo:halo + span, :] = ko_ref[:, s].astype(F32)
        vs[h, 0:halo, :] = vp_ref[:, s].astype(F32)
        vs[h, halo:halo + span, :] = vo_ref[:, s].astype(F32)

    qi = lax.broadcasted_iota(jnp.int32, (blk, blk), 0)
    kj = lax.broadcasted_iota(jnp.int32, (blk, blk), 1)
    band_prev = qi <= kj
    mask_own = kj <= qi
    lane = lax.broadcasted_iota(jnp.int32, (blk, LANES), 1)
    scale = HEAD_DIM ** -0.5

    def rows(start):
        return pl.ds(start, blk, stride=dil) if dil > 1 else pl.ds(start, blk)

    @pl.when(hg == 0)
    def _():
        lse_ref[...] = jnp.zeros(lse_ref.shape, F32)

    combos = [(r, j, h) for r in range(dil) for j in range(m_blocks) for h in range(hp)]
    lse_tiles = {}
    for c0 in range(0, len(combos), DIL_CHAINS):
        group = combos[c0:c0 + DIL_CHAINS]
        q = [qs[h, rows(r + halo * j), :] for r, j, h in group]
        bias = [bias_ref[hg * hp + h] for _, _, h in group]
        mprev = [band_prev & jnp.logical_not(first_span) if j == 0 else band_prev for _, j, _ in group]
        s_prev = [jnp.where(mp, _dot_nt(qq, ks[h, rows(r + halo * j), :]) * scale + bb[:, :blk], -jnp.inf)
                  for (r, j, h), qq, bb, mp in zip(group, q, bias, mprev)]
        s_own = [jnp.where(mask_own, _dot_nt(qq, ks[h, rows(r + halo * (j + 1)), :]) * scale + bb[:, blk:], -jnp.inf)
                 for (r, j, h), qq, bb in zip(group, q, bias)]
        m = [jnp.maximum(jnp.max(sp, axis=-1, keepdims=True), jnp.max(so, axis=-1, keepdims=True))
             for sp, so in zip(s_prev, s_own)]
        p_prev = [jnp.exp(sp - mm) for sp, mm in zip(s_prev, m)]
        p_own = [jnp.exp(so - mm) for so, mm in zip(s_own, m)]
        den = [jnp.sum(pp, axis=-1, keepdims=True) + jnp.sum(po, axis=-1, keepdims=True)
               for pp, po in zip(p_prev, p_own)]
        o = [_dot_bf16(pp, vs[h, rows(r + halo * j), :]) + _dot_bf16(po, vs[h, rows(r + halo * (j + 1)), :])
             for (r, j, h), pp, po in zip(group, p_prev, p_own)]
        for (r, j, h), oo, dd, mm in zip(group, o, den, m):
            os_[h, rows(r + halo * j), :] = oo / dd
            prev_tile = lse_tiles[r, j] if (r, j) in lse_tiles else lse_ref[rows(r + halo * j), :]
            lse_tiles[r, j] = jnp.where(lane == hg * hp + h, mm + jnp.log(dd), prev_tile)
            if h == hp - 1:
                lse_ref[rows(r + halo * j), :] = lse_tiles.pop((r, j))
    for h in range(hp):
        o_ref[:, h * HEAD_DIM:(h + 1) * HEAD_DIM] = os_[h]


def dilated_group_attention(q, k, v, bias, dil, batch, seq):
    t, kvd = q.shape
    span = min(DIL_STEP_TOKENS[dil], seq)
    hp = DIL_STEP_HEADS[dil]
    halo = dil * DIL_BLOCK
    assert seq % span == 0 and span % halo == 0 and DIL_HEADS % hp == 0
    n_span = seq // span
    ratio = span // halo
    cols = hp * HEAD_DIM
    own = lambda b, i, g: (b * n_span + i, g)
    prev = lambda b, i, g: (jnp.maximum((b * n_span + i) * ratio - 1, 0), g)
    o, lse = pl.pallas_call(
        functools.partial(_dil_attn_kernel, dil=dil),
        out_shape=(jax.ShapeDtypeStruct((t, kvd), F32),
                   jax.ShapeDtypeStruct((t, LANES), F32)),
        grid=(batch, n_span, DIL_HEADS // hp),
        in_specs=[
            pl.BlockSpec((span, cols), own),
            pl.BlockSpec((halo, cols), prev),
            pl.BlockSpec((span, cols), own),
            pl.BlockSpec((halo, cols), prev),
            pl.BlockSpec((span, cols), own),
            pl.BlockSpec((DIL_HEADS, DIL_BLOCK, 2 * DIL_BLOCK), lambda b, i, g: (0, 0, 0)),
        ],
        out_specs=(pl.BlockSpec((span, cols), own), pl.BlockSpec((span, LANES), lambda b, i, g: (b * n_span + i, 0))),
        scratch_shapes=[pltpu.VMEM((hp, span, HEAD_DIM), F32), pltpu.VMEM((hp, halo + span, HEAD_DIM), F32),
                        pltpu.VMEM((hp, halo + span, HEAD_DIM), F32), pltpu.VMEM((hp, span, HEAD_DIM), F32)],
        compiler_params=_params("parallel", "parallel", "arbitrary"),
        name=f"dilated_attention_d{dil}",
    )(q, k, k, v, v, bias)
    return o, lse


def _dil_out_kernel(o0_ref, o1_ref, o2_ref, l0_ref, l1_ref, l2_ref, w_ref, h_ref, gain_ref, bias_ref,
                    o32_ref, o16_ref, orow_ref, mix_ref):
    l0, l1, l2 = l0_ref[...], l1_ref[...], l2_ref[...]
    m = jnp.maximum(jnp.maximum(l0, l1), l2)
    e0, e1, e2 = jnp.exp(l0 - m), jnp.exp(l1 - m), jnp.exp(l2 - m)
    inv = 1.0 / (e0 + e1 + e2)
    w0, w1, w2 = e0 * inv, e1 * inv, e2 * inv
    for hd in range(DIL_HEADS):
        sl = slice(hd * HEAD_DIM, (hd + 1) * HEAD_DIM)
        mix_ref[:, sl] = (w0[:, hd:hd + 1] * o0_ref[:, sl] + w1[:, hd:hd + 1] * o1_ref[:, sl]
                          + w2[:, hd:hd + 1] * o2_ref[:, sl])
    mix = jnp.dot(mix_ref[...].astype(BF16), w_ref[...], preferred_element_type=F32)
    y = _layer_norm_rows(DEEPNORM_ALPHA * h_ref[...] + mix, gain_ref[...], bias_ref[...])
    o32_ref[...] = y
    o16_ref[...] = y.astype(BF16)
    _store_row_contiguous(orow_ref, y)


def dilated_out(outs, lses, w_o, h, gain, bias, tm=LN_TM):
    m, kvd = outs[0].shape
    d = w_o.shape[1]
    tm = min(tm, m)
    row = lambda i: (i, 0)
    fixed = lambda i: (0, 0)
    return pl.pallas_call(
        _dil_out_kernel,
        out_shape=(jax.ShapeDtypeStruct((m, d), F32), jax.ShapeDtypeStruct((m, d), BF16),
                   jax.ShapeDtypeStruct((m * (d // LANES), LANES), F32)),
        grid=(m // tm,),
        in_specs=[pl.BlockSpec((tm, kvd), row)] * 3 + [pl.BlockSpec((tm, LANES), row)] * 3 + [
            pl.BlockSpec((kvd, d), fixed), pl.BlockSpec((tm, d), row),
            pl.BlockSpec((1, d), fixed), pl.BlockSpec((1, d), fixed)],
        out_specs=(pl.BlockSpec((tm, d), row), pl.BlockSpec((tm, d), row), pl.BlockSpec((tm * (d // LANES), LANES), row)),
        scratch_shapes=[pltpu.VMEM((tm, kvd), F32)],
        compiler_params=_params("parallel"),
        name="dilated_out_ln",
    )(*outs, *lses, w_o, h, gain.reshape(1, d), bias.reshape(1, d))


def _t5_bucket(dist):
    exact = REL_BUCKETS // 2
    distf = jnp.maximum(dist, exact).astype(F32)
    large = exact + (jnp.log(distf / exact) / math.log(REL_MAX_DIST / exact)
                     * (REL_BUCKETS - exact)).astype(jnp.int32)
    return jnp.where(dist < exact, dist, jnp.minimum(large, REL_BUCKETS - 1))


def _band_bias(rel_bias_g, dil):
    qi = jnp.arange(DIL_BLOCK)[:, None]
    kj = jnp.arange(2 * DIL_BLOCK)[None, :]
    steps = DIL_BLOCK + qi - kj
    bias = rel_bias_g[_t5_bucket(jnp.maximum(steps, 0) * dil)]
    return jnp.transpose(bias, (2, 0, 1)).astype(F32)


def _router_kernel(h_ref, w_ref, b_ref, o_ref, cnt_ref, running):
    @pl.when(pl.program_id(0) == 0)
    def _():
        running[...] = jnp.zeros(running.shape, F32)

    logits = _dot_split(h_ref[...], w_ref) + b_ref[...]
    lane = lax.broadcasted_iota(jnp.int32, logits.shape, 1)
    neg = -jnp.inf
    big = jnp.int32(LANES)
    is_group = lane < MOE_GROUPS
    gl = jnp.where(is_group, logits, neg)
    gmax = jnp.max(gl, axis=-1, keepdims=True)
    gidx = jnp.min(jnp.where(gl == gmax, lane, big), axis=-1, keepdims=True)
    gden = jnp.sum(jnp.exp(gl - gmax), axis=-1, keepdims=True)
    group_gate = 1.0 / gden
    lo = MOE_GROUPS + gidx * MOE_EXPERTS_PER_GROUP
    in_group = (lane >= lo) & (lane < lo + MOE_EXPERTS_PER_GROUP)
    ll = jnp.where(in_group, logits, neg)
    m1 = jnp.max(ll, axis=-1, keepdims=True)
    i1 = jnp.min(jnp.where(ll == m1, lane, big), axis=-1, keepdims=True)
    rest = jnp.where(lane == i1, neg, ll)
    m2 = jnp.max(rest, axis=-1, keepdims=True)
    i2 = jnp.min(jnp.where(rest == m2, lane, big), axis=-1, keepdims=True)
    den = jnp.sum(jnp.exp(ll - m1), axis=-1, keepdims=True)
    p1 = 1.0 / den
    p2 = jnp.exp(m2 - m1) / den
    g1 = group_gate * p1 / (p1 + p2)
    g2 = group_gate * p2 / (p1 + p2)
    id1 = (i1 - MOE_GROUPS).astype(F32)
    id2 = (i2 - MOE_GROUPS).astype(F32)
    tm = logits.shape[0]
    oh1 = (lane == i1).astype(F32)
    oh2 = (lane == i2).astype(F32)
    both = oh1 + oh2
    r_i = lax.broadcasted_iota(jnp.int32, (tm, tm), 0)
    c_i = lax.broadcasted_iota(jnp.int32, (tm, tm), 1)
    before = jnp.dot((c_i < r_i).astype(BF16), both.astype(BF16), preferred_element_type=F32) + running[...]
    rank1 = jnp.sum(before * oh1, axis=-1, keepdims=True)
    rank2 = jnp.sum(before * oh2, axis=-1, keepdims=True)
    running[...] = running[...] + jnp.sum(both, axis=0, keepdims=True)
    cnt_ref[...] = running[...]
    o_ref[...] = jnp.where(lane == 0, id1, jnp.where(lane == 1, id2, jnp.where(lane == 2, g1, jnp.where(
        lane == 3, g2, jnp.where(lane == 4, rank1, rank2)))))


def moe_router(h, w_group, b_group, w_expert, b_expert, tm=ROUTER_TM):
    m, d = h.shape
    tm = min(tm, m)
    pad = LANES - MOE_GROUPS - MOE_EXPERTS
    w = jnp.pad(jnp.concatenate([w_group, w_expert], axis=1), ((0, 0), (0, pad)))
    b = jnp.pad(jnp.concatenate([b_group, b_expert]), (0, pad)).reshape(1, LANES)
    out, cnt = pl.pallas_call(
        _router_kernel,
        out_shape=(jax.ShapeDtypeStruct((m, LANES), F32), jax.ShapeDtypeStruct((1, LANES), F32)),
        grid=(m // tm,),
        in_specs=[pl.BlockSpec((tm, d), lambda i: (i, 0)),
                  pl.BlockSpec((d, 2 * LANES), lambda i: (0, 0)),
                  pl.BlockSpec((1, LANES), lambda i: (0, 0))],
        out_specs=(pl.BlockSpec((tm, LANES), lambda i: (i, 0)), pl.BlockSpec((1, LANES), lambda i: (0, 0))),
        scratch_shapes=[pltpu.VMEM((1, LANES), F32)],
        compiler_params=_params("arbitrary"),
        name="moe_router",
    )(h, _hi_lo_columns(w), b)
    k = MOE_TOP_K
    counts = cnt[0, MOE_GROUPS:MOE_GROUPS + MOE_EXPERTS].astype(jnp.int32)
    return out[:, :k].astype(jnp.int32), out[:, k:2 * k], out[:, 2 * k:3 * k].astype(jnp.int32), counts


def _gather_rows_start(src_hbm, idx_ref, idx_base, buf, sem, unrolled, row_tiles=None):
    if row_tiles is None:
        groups = buf.shape[0]

        def copy(g, j):
            pltpu.make_async_copy(src_hbm.at[pl.ds(idx_ref[idx_base + g * SUBLANES + j], 1)],
                                  buf.at[g, pl.ds(j, 1)], sem).start()
    else:
        w = row_tiles
        groups = buf.shape[0] // (w * SUBLANES)

        def copy(g, j):
            r = g * SUBLANES + j
            pltpu.make_async_copy(src_hbm.at[pl.ds(pl.multiple_of(idx_ref[idx_base + r], w), w)],
                                  buf.at[pl.ds(pl.multiple_of(r * w, w), w)], sem).start()

    if unrolled:
        for g in range(groups):
            for j in range(SUBLANES):
                copy(g, j)
    else:
        def body(g, carry):
            for j in range(SUBLANES):
                copy(g, j)
            return carry
        lax.fori_loop(0, groups, body, 0)


def _gather_rows_wait(buf, sem):
    pltpu.make_async_copy(buf, buf, sem).wait()


def _gathered_blocks(src_hbm, idx_ref, bufs, sems, compute, row_tiles=None):
    i = pl.program_id(0)
    n = pl.num_programs(0)
    if row_tiles is None:
        rows = bufs[0].shape[0] * bufs[0].shape[1]
        load = lambda buf: buf[...].reshape(rows, buf.shape[2])
    else:
        rows = bufs[0].shape[0] // row_tiles
        load = lambda buf: _load_row_contiguous(buf, rows)

    @pl.when(i == 0)
    def _():
        _gather_rows_start(src_hbm, idx_ref, 0, bufs[0], sems.at[0], False, row_tiles)

    def step(s):
        _gather_rows_wait(bufs[s], sems.at[s])
        _gather_rows_start(src_hbm, idx_ref, jnp.minimum(i + 1, n - 1) * rows, bufs[1 - s], sems.at[1 - s],
                           True, row_tiles)
        compute(load(bufs[s]))

        @pl.when(i == n - 1)
        def _():
            _gather_rows_wait(bufs[1 - s], sems.at[1 - s])

    for s in range(2):
        pl.when(i % 2 == s)(functools.partial(step, s))


def _expert_kernel(bexp_ref, rtok_ref, nused_ref, h_hbm, w1_ref, w3_ref, w2_ref, y_ref,
                   xbuf_a, xbuf_b, sems, wb1, wb3, wb2):
    i = pl.program_id(0)

    @pl.when((i < nused_ref[0]) & ((i == 0) | (bexp_ref[i] != bexp_ref[jnp.maximum(i - 1, 0)])))
    def _():
        wb1[...] = w1_ref[0, 0].astype(BF16)
        wb3[...] = w3_ref[0, 0].astype(BF16)
        wb2[...] = w2_ref[0, 0].astype(BF16)

    def compute(x):
        x = x.astype(BF16)
        h1 = jnp.dot(x, wb1[...], preferred_element_type=F32)
        h3 = jnp.dot(x, wb3[...], preferred_element_type=F32)
        hmid = (h1 * jax.nn.sigmoid(h1) * h3).astype(BF16)
        y_ref[...] = jnp.dot(hmid, wb2[...], preferred_element_type=F32)

    _gathered_blocks(h_hbm, rtok_ref, (xbuf_a, xbuf_b), sems, compute, row_tiles=w1_ref.shape[2] // LANES)


def moe_experts(h_rows, row_token, block_expert, n_used, w1, w3, w2, layer, tm=MOE_TM):
    n_rows = row_token.shape[0]
    n_blocks = n_rows // tm
    d = w1.shape[2]
    row_tiles = d // LANES
    ff = w1.shape[3]
    grid_spec = pltpu.PrefetchScalarGridSpec(
        num_scalar_prefetch=3,
        grid=(n_blocks,),
        in_specs=[
            pl.BlockSpec(memory_space=pl.ANY),
            pl.BlockSpec((1, 1, d, ff), lambda i, be, rt, nu: (layer, be[i], 0, 0)),
            pl.BlockSpec((1, 1, d, ff), lambda i, be, rt, nu: (layer, be[i], 0, 0)),
            pl.BlockSpec((1, 1, ff, d), lambda i, be, rt, nu: (layer, be[i], 0, 0)),
        ],
        out_specs=pl.BlockSpec((tm, d), lambda i, be, rt, nu: (i, 0)),
        scratch_shapes=[pltpu.VMEM((tm * row_tiles, LANES), F32), pltpu.VMEM((tm * row_tiles, LANES), F32),
                        pltpu.SemaphoreType.DMA((2,)),
                        pltpu.VMEM((d, ff), BF16), pltpu.VMEM((d, ff), BF16), pltpu.VMEM((ff, d), BF16)],
    )
    return pl.pallas_call(
        _expert_kernel,
        out_shape=jax.ShapeDtypeStruct((n_rows, d), F32),
        grid_spec=grid_spec,
        compiler_params=_params("arbitrary"),
        name="moe_experts",
    )(block_expert, row_token, n_used, h_rows, w1, w3, w2)


def _combine_kernel(dest_ref, y_hbm, gate_ref, h_ref, gain_ref, bias_ref, o32_ref, o16_ref, ybuf_a, ybuf_b, sems):
    tm = h_ref.shape[0]

    def compute(yy):
        gates = gate_ref[...]
        ffn = yy[0:tm, :] * gates[:, 0:1] + yy[tm:, :] * gates[:, 1:2]
        y = _layer_norm_rows(DEEPNORM_ALPHA * h_ref[...] + ffn, gain_ref[...], bias_ref[...])
        o32_ref[...] = y
        o16_ref[...] = y.astype(BF16)

    _gathered_blocks(y_hbm, dest_ref, (ybuf_a, ybuf_b), sems, compute)


def moe_combine_ln(y_rows, dest, gates, h, gain, bias, tm=COMBINE_TM):
    m, d = h.shape
    tm = min(tm, m)
    grid_spec = pltpu.PrefetchScalarGridSpec(
        num_scalar_prefetch=1,
        grid=(m // tm,),
        in_specs=[
            pl.BlockSpec(memory_space=pl.ANY),
            pl.BlockSpec((tm, MOE_TOP_K), lambda i, de: (i, 0)),
            pl.BlockSpec((tm, d), lambda i, de: (i, 0)),
            pl.BlockSpec((1, d), lambda i, de: (0, 0)),
            pl.BlockSpec((1, d), lambda i, de: (0, 0)),
        ],
        out_specs=(pl.BlockSpec((tm, d), lambda i, de: (i, 0)),
                   pl.BlockSpec((tm, d), lambda i, de: (i, 0))),
        scratch_shapes=[pltpu.VMEM((MOE_TOP_K * tm // SUBLANES, SUBLANES, d), F32),
                        pltpu.VMEM((MOE_TOP_K * tm // SUBLANES, SUBLANES, d), F32), pltpu.SemaphoreType.DMA((2,))],
    )
    return pl.pallas_call(
        _combine_kernel,
        out_shape=(jax.ShapeDtypeStruct((m, d), F32), jax.ShapeDtypeStruct((m, d), BF16)),
        grid_spec=grid_spec,
        compiler_params=_params("arbitrary"),
        name="moe_combine_ln",
    )(dest.reshape(-1), y_rows, gates, h, gain.reshape(1, d), bias.reshape(1, d))


def moe_layer(h32, h_rows, w_group, b_group, w_expert, b_expert, w1, w3, w2, layer, gain, bias, tm=MOE_TM):
    t = h32.shape[0]
    ids, gates, ranks, counts = moe_router(h32, w_group, b_group, w_expert, b_expert)
    eid = ids.reshape(-1)
    n_pairs = eid.shape[0]
    padded = (counts + tm - 1) // tm * tm
    seg_end = jnp.cumsum(padded)
    seg_start = seg_end - padded
    onehot = eid[:, None] == jnp.arange(MOE_EXPERTS, dtype=jnp.int32)[None, :]
    dest = (jnp.sum(jnp.where(onehot, seg_start[None, :], 0), axis=1) + ranks.reshape(-1)).astype(jnp.int32)
    n_blocks = -(-n_pairs // tm) + MOE_EXPERTS
    n_rows = n_blocks * tm
    token_of_pair = (jnp.arange(n_pairs, dtype=jnp.int32) // MOE_TOP_K)
    row_token = jnp.zeros((n_rows,), jnp.int32).at[dest].set(token_of_pair * (h32.shape[1] // LANES))
    block_start = jnp.arange(n_blocks, dtype=jnp.int32) * tm
    block_expert = jnp.minimum(jnp.sum((seg_end[None, :] <= block_start[:, None]).astype(jnp.int32), axis=1),
                               MOE_EXPERTS - 1)
    n_used = (seg_end[-1:] // tm).astype(jnp.int32)
    y_rows = moe_experts(h_rows, row_token, block_expert, n_used, w1, w3, w2, layer, tm=tm)
    ctm = min(COMBINE_TM, t)
    dest_tiles = jnp.transpose(dest.reshape(t // ctm, ctm, MOE_TOP_K), (0, 2, 1))
    return moe_combine_ln(y_rows, dest_tiles, gates, h32, gain, bias, tm=ctm)


def kernel(x, gdn_w_in, gdn_conv, gdn_a_log, gdn_dt_bias, gdn_norm, gdn_w_out, kv_w_k, kv_w_v, dil_w_q, dil_w_o,
           rel_bias, ln_gain, ln_bias, moe_w_group, moe_b_group, moe_w_expert, moe_b_expert, moe_w1, moe_w3, moe_w2):
    batch, seq, d = x.shape
    t = batch * seq
    h32 = x.reshape(t, d)
    h16 = h32.astype(BF16)
    k = v = None
    biases = None
    for layer in range(DEPTH):
        if layer < N_A_LAYERS:
            h32, h16, h_rows = gated_deltanet_layer(
                h32, h16, gdn_w_in[layer], gdn_conv[layer], gdn_a_log[layer], gdn_dt_bias[layer],
                gdn_norm[layer], gdn_w_out[layer], ln_gain[layer, 0], ln_bias[layer, 0], batch, seq)
        else:
            if layer == N_A_LAYERS:
                k = matmul(h16, kv_w_k.astype(BF16), out_dtype=BF16)
                v = matmul(h16, kv_w_v.astype(BF16), out_dtype=BF16)
                biases = [_band_bias(rel_bias[:, gi * DIL_HEADS:(gi + 1) * DIL_HEADS], dil)
                          for gi, (_, dil) in enumerate(DIL_CONFIGS)]
            j = layer - N_A_LAYERS
            outs, lses = [], []
            for gi, (_, dil) in enumerate(DIL_CONFIGS):
                q = matmul(h16, dil_w_q[j][:, gi * DIL_KV_DIM:(gi + 1) * DIL_KV_DIM].astype(BF16), out_dtype=BF16)
                o, lse = dilated_group_attention(q, k, v, biases[gi], dil, batch, seq)
                outs.append(o)
                lses.append(lse)
            h32, h16, h_rows = dilated_out(outs, lses, dil_w_o[j].astype(BF16), h32, ln_gain[layer, 0], ln_bias[layer, 0])
        h32, h16 = moe_layer(h32, h_rows, moe_w_group[layer], moe_b_group[layer], moe_w_expert[layer], moe_b_expert[layer],
                             moe_w1, moe_w3, moe_w2, layer, ln_gain[layer, 1], ln_bias[layer, 1])
    return h32.reshape(batch, seq, d)
```

```python
import functools
import math

import jax
import jax.numpy as jnp
from jax import lax
from jax.experimental import pallas as pl
from jax.experimental.pallas import tpu as pltpu

F32 = jnp.float32
BF16 = jnp.bfloat16

DEPTH = 4
N_A_LAYERS = DEPTH // 2
HEAD_DIM = 128
GDN_QK_HEADS = 16
GDN_V_HEADS = 32
GDN_CONV = 4
GDN_CHUNK = 64
GDN_QK_DIM = GDN_QK_HEADS * HEAD_DIM
GDN_V_DIM = GDN_V_HEADS * HEAD_DIM
GDN_CONV_DIM = 2 * GDN_QK_DIM + GDN_V_DIM
DIL_CONFIGS = ((128, 1), (512, 4), (2048, 16))
N_DIL = len(DIL_CONFIGS)
DIL_HEADS = 8
DIL_BLOCK = 128
DIL_KV_DIM = DIL_HEADS * HEAD_DIM
REL_BUCKETS = 32
REL_MAX_DIST = 2048
MOE_GROUPS = 4
MOE_EXPERTS_PER_GROUP = 8
MOE_EXPERTS = MOE_GROUPS * MOE_EXPERTS_PER_GROUP
MOE_TOP_K = 2
DEEPNORM_ALPHA = (2 * DEPTH) ** 0.25
LN_EPS = 1e-5
RMS_EPS = 1e-6
L2_EPS = 1e-6

LANES = 128
SUBLANES = 8
VMEM_LIMIT_BYTES = 56 * 1024 * 1024

MM_TM = 512
MM_TN = 1024
LN_TM = 256
GATE_TM = 512
GDN_TB = 512
GDN_QK_PER_STEP = 4
MOE_TM = 256
ROUTER_TM = 512
COMBINE_TM = 256


def _params(*semantics):
    return pltpu.CompilerParams(dimension_semantics=semantics, vmem_limit_bytes=VMEM_LIMIT_BYTES)


def _split_bf16(x):
    hi = x.astype(BF16)
    return hi, (x - hi.astype(F32)).astype(BF16)


def _dot_split(x, w2_ref):
    n = w2_ref.shape[1] // 2
    hi, lo = _split_bf16(x)
    a = jnp.dot(hi, w2_ref[...], preferred_element_type=F32)
    b = jnp.dot(lo, w2_ref[:, :n], preferred_element_type=F32)
    return a[:, :n] + a[:, n:] + b


def _hi_lo_columns(w):
    hi = w.astype(BF16)
    return jnp.concatenate([hi, (w - hi.astype(F32)).astype(BF16)], axis=1)


def _store_row_contiguous(ref, y):
    rows, width = y.shape
    w = width // LANES
    for c in range(w):
        ref[pl.ds(c, rows, stride=w), :] = y[:, c * LANES:(c + 1) * LANES]


def _load_row_contiguous(ref, rows):
    w = ref.shape[0] // rows
    return jnp.concatenate([ref[pl.ds(c, rows, stride=w), :] for c in range(w)], axis=1)


def _layer_norm_rows(y, gain, bias):
    mu = jnp.mean(y, axis=-1, keepdims=True)
    yc = y - mu
    var = jnp.mean(yc * yc, axis=-1, keepdims=True)
    return yc * lax.rsqrt(var + LN_EPS) * gain + bias


def _matmul_kernel(a_ref, b_ref, o_ref):
    o_ref[...] = jnp.dot(a_ref[...].astype(BF16), b_ref[...],
                         preferred_element_type=F32).astype(o_ref.dtype)


def matmul(a, b, out_dtype=F32, tm=MM_TM, tn=MM_TN):
    m, k = a.shape
    n = b.shape[1]
    tm, tn = min(tm, m), min(tn, n)
    assert m % tm == 0 and n % tn == 0
    return pl.pallas_call(
        _matmul_kernel,
        out_shape=jax.ShapeDtypeStruct((m, n), out_dtype),
        grid=(n // tn, m // tm),
        in_specs=[pl.BlockSpec((tm, k), lambda j, i: (i, 0)),
                  pl.BlockSpec((k, tn), lambda j, i: (0, j))],
        out_specs=pl.BlockSpec((tm, tn), lambda j, i: (i, j)),
        compiler_params=_params("parallel", "parallel"),
        name="matmul",
    )(a, b)


def _proj_ln_kernel(a_ref, w_ref, h_ref, gain_ref, bias_ref, o32_ref, o16_ref, orow_ref):
    mix = jnp.dot(a_ref[...].astype(BF16), w_ref[...], preferred_element_type=F32)
    y = _layer_norm_rows(DEEPNORM_ALPHA * h_ref[...] + mix, gain_ref[...], bias_ref[...])
    o32_ref[...] = y
    o16_ref[...] = y.astype(BF16)
    _store_row_contiguous(orow_ref, y)


def proj_residual_ln(a, w, h, gain, bias, tm=LN_TM):
    m, k = a.shape
    d = w.shape[1]
    tm = min(tm, m)
    assert m % tm == 0
    return pl.pallas_call(
        _proj_ln_kernel,
        out_shape=(jax.ShapeDtypeStruct((m, d), F32), jax.ShapeDtypeStruct((m, d), BF16),
                   jax.ShapeDtypeStruct((m * (d // LANES), LANES), F32)),
        grid=(m // tm,),
        in_specs=[pl.BlockSpec((tm, k), lambda i: (i, 0)),
                  pl.BlockSpec((k, d), lambda i: (0, 0)),
                  pl.BlockSpec((tm, d), lambda i: (i, 0)),
                  pl.BlockSpec((1, d), lambda i: (0, 0)),
                  pl.BlockSpec((1, d), lambda i: (0, 0))],
        out_specs=(pl.BlockSpec((tm, d), lambda i: (i, 0)),
                   pl.BlockSpec((tm, d), lambda i: (i, 0)),
                   pl.BlockSpec((tm * (d // LANES), LANES), lambda i: (i, 0))),
        compiler_params=_params("parallel"),
        name="proj_residual_ln",
    )(a, w, h, gain.reshape(1, d), bias.reshape(1, d))


def _gdn_gate_kernel(h_ref, w_ref, alog_ref, dt_ref, o_ref, h16_ref):
    tm = h_ref.shape[0]
    n_groups = o_ref.shape[0]
    h16_ref[...] = h_ref[...].astype(BF16)
    width = 2 * GDN_V_HEADS // n_groups
    ba = _dot_split(h_ref[...], w_ref)
    xa = ba + dt_ref[...]
    softplus = jnp.maximum(xa, 0.0) + jnp.log(1.0 + jnp.exp(-jnp.abs(xa)))
    g = -jnp.exp(alog_ref[...]) * softplus
    row = lax.broadcasted_iota(jnp.int32, (tm, tm), 0)
    col = lax.broadcasted_iota(jnp.int32, (tm, tm), 1)
    tri = ((row // GDN_CHUNK == col // GDN_CHUNK) & (col <= row)).astype(BF16)
    g_hi, g_lo = _split_bf16(g)
    gc2 = jnp.dot(tri, jnp.concatenate([g_hi, g_lo], axis=1), preferred_element_type=F32)
    gc = gc2[:, :LANES] + gc2[:, LANES:]
    lane = lax.broadcasted_iota(jnp.int32, ba.shape, 1)
    slab = jnp.where(lane % width < width // 2, jax.nn.sigmoid(ba), gc)
    for r in range(n_groups):
        o_ref[r] = slab if r == 0 else pltpu.roll(slab, LANES - width * r, axis=1)


def gdn_gates(h, w_ba, a_log, dt_bias, n_qk=GDN_QK_PER_STEP, tm=GATE_TM):
    m, d = h.shape
    tm = min(tm, m)
    v = 2 * n_qk
    n_groups = GDN_V_HEADS // v
    cols = [(v * (l // (2 * v)) + l % (2 * v)) if l % (2 * v) < v else (GDN_V_HEADS + v * (l // (2 * v)) + l % (2 * v) - v)
            for l in range(2 * GDN_V_HEADS)]
    is_decay = jnp.array([l % (2 * v) >= v for l in range(2 * GDN_V_HEADS)])
    head = jnp.array([c % GDN_V_HEADS for c in cols], jnp.int32)
    pad = LANES - 2 * GDN_V_HEADS
    w = jnp.pad(w_ba[:, jnp.array(cols, jnp.int32)], ((0, 0), (0, pad)))
    alog = jnp.pad(jnp.where(is_decay, a_log[head], 0.0), (0, pad)).reshape(1, LANES)
    dt = jnp.pad(jnp.where(is_decay, dt_bias[head], 0.0), (0, pad)).reshape(1, LANES)
    return pl.pallas_call(
        _gdn_gate_kernel,
        out_shape=(jax.ShapeDtypeStruct((n_groups, m, LANES), F32), jax.ShapeDtypeStruct((m, d), BF16)),
        grid=(m // tm,),
        in_specs=[pl.BlockSpec((tm, d), lambda i: (i, 0)),
                  pl.BlockSpec((d, 2 * LANES), lambda i: (0, 0)),
                  pl.BlockSpec((1, LANES), lambda i: (0, 0)),
                  pl.BlockSpec((1, LANES), lambda i: (0, 0))],
        out_specs=(pl.BlockSpec((n_groups, tm, LANES), lambda i: (0, i, 0)), pl.BlockSpec((tm, d), lambda i: (i, 0))),
        compiler_params=_params("parallel"),
        name="gdn_gates",
    )(h, _hi_lo_columns(w), alog, dt)


def _dot_bf16(a, b):
    return jnp.dot(a.astype(BF16), b.astype(BF16), preferred_element_type=F32)


def _dot_nt(a, b):
    return lax.dot_general(a.astype(BF16), b.astype(BF16), (((1,), (1,)), ((), ())),
                           preferred_element_type=F32)


def _dot_tn(a, b):
    return lax.dot_general(a.astype(BF16), b.astype(BF16), (((0,), (0,)), ((), ())),
                           preferred_element_type=F32)


def _block_diag2(x0, x1):
    z = jnp.zeros_like(x0)
    return jnp.concatenate([jnp.concatenate([x0, z], axis=1), jnp.concatenate([z, x1], axis=1)], axis=0)


def _gdn_kernel(q_ref, k_ref, v_ref, z_ref, wq_ref, wk_ref, wv_ref, g_ref, nw_ref,
                o_ref, qx, kx, vx, qn, kn, vn, state):
    blk = pl.program_id(2)
    tb = q_ref.shape[0]
    n_chunks = tb // GDN_CHUNK
    n_qk = q_ref.shape[1] // HEAD_DIM
    c = GDN_CHUNK
    hd = HEAD_DIM
    tail = SUBLANES

    @pl.when(blk == 0)
    def _():
        qx[0:tail, :] = jnp.zeros((tail, qx.shape[1]), F32)
        kx[0:tail, :] = jnp.zeros((tail, kx.shape[1]), F32)
        vx[0:tail, :] = jnp.zeros((tail, vx.shape[1]), F32)
        state[...] = jnp.zeros(state.shape, F32)

    def conv_silu(x_ref, ext, w_ref):
        ext[tail:tail + tb, :] = x_ref[...].astype(F32)
        y = None
        for j in range(GDN_CONV):
            term = w_ref[j:j + 1, :] * ext[pl.ds(tail - (GDN_CONV - 1) + j, tb), :]
            y = term if y is None else y + term
        ext[0:tail, :] = ext[tb:tb + tail, :]
        return y * jax.nn.sigmoid(y)

    def l2n(x):
        heads = [x[:, g * hd:(g + 1) * hd] for g in range(n_qk)]
        return jnp.concatenate([xh * lax.rsqrt(jnp.sum(xh * xh, axis=-1, keepdims=True) + L2_EPS) for xh in heads], axis=1)

    qn[...] = l2n(conv_silu(q_ref, qx, wq_ref)) * (HEAD_DIM ** -0.5)
    kn[...] = l2n(conv_silu(k_ref, kx, wk_ref))
    vn[...] = conv_silu(v_ref, vx, wv_ref)

    row = lax.broadcasted_iota(jnp.int32, (c, 2 * c), 0)
    lane = lax.broadcasted_iota(jnp.int32, (c, 2 * c), 1)
    col = lane & (c - 1)
    head0 = lane < c
    causal = col <= row
    strict = col < row
    in_block = (row // 16) == (col // 16)
    eye = (row == col).astype(F32)
    norm_w = nw_ref[...]

    def mm(x, y):
        yb = jnp.concatenate([jnp.where(head0, y, 0.0), jnp.where(head0, 0.0, y)], axis=0)
        return _dot_bf16(x, yb)

    units = [(ci, g) for ci in range(n_chunks) for g in range(n_qk)]
    each = lambda f, *lists: [f(*xs) for xs in zip(*lists)]
    qcs = [qn[ci * c:(ci + 1) * c, g * hd:(g + 1) * hd] for ci, g in units]
    kcs = [kn[ci * c:(ci + 1) * c, g * hd:(g + 1) * hd] for ci, g in units]
    n_v = 2 * n_qk
    gates = g_ref[0]
    gates_t = gates.T
    betas = [[gates[ci * c:(ci + 1) * c, 2 * g + hh:2 * g + hh + 1] for hh in range(2)] for ci, g in units]
    gcs = [[gates[ci * c:(ci + 1) * c, n_v + 2 * g + hh:n_v + 2 * g + hh + 1] for hh in range(2)] for ci, g in units]

    def packed_row(ci, g):
        w0 = (ci // 2) * 2 * c
        r0, r1 = (gates_t[n_v + 2 * g + hh:n_v + 2 * g + hh + 1, w0:w0 + 2 * c] for hh in range(2))
        if ci % 2 == 0:
            return jnp.where(head0[0:1], r0, pltpu.roll(r1, c, axis=1))
        return jnp.where(head0[0:1], pltpu.roll(r0, c, axis=1), r1)

    grps = [packed_row(ci, g) for ci, g in units]
    egs = [[jnp.exp(g[hh]) for hh in range(2)] for g in gcs]
    glast = [[g[hh][c - 1:c, :] for hh in range(2)] for g in gcs]
    both = each(lambda q, k: _dot_nt(jnp.concatenate([q, k], axis=0), jnp.concatenate([k, k], axis=0)), qcs, kcs)
    decayp = each(lambda g, gr: jnp.exp(jnp.where(causal, jnp.where(head0, g[0], g[1]) - gr, -jnp.inf)), gcs, grps)
    ap = each(lambda b, bo, dc: jnp.where(strict, jnp.where(head0, b[0], b[1]) * bo[c:] * dc, 0.0), betas, both, decayp)
    qkd = each(lambda bo, dc: bo[:c] * dc, both, decayp)
    d = each(lambda a: jnp.where(in_block, a, 0.0), ap)
    e = each(lambda a, dd: a - dd, ap, d)
    n1 = each(lambda dd: -dd, d)
    p = each(lambda n: eye + n, n1)
    n2 = each(mm, n1, n1)
    p = each(lambda pp, n: pp + mm(pp, n), p, n2)
    n4 = each(mm, n2, n2)
    p = each(lambda pp, n: pp + mm(pp, n), p, n4)
    n8 = each(mm, n4, n4)
    dinv = each(lambda pp, n: pp + mm(pp, n), p, n8)
    nb = each(mm, dinv, e)
    nb2 = each(mm, nb, nb)
    qq = each(lambda n: eye - n, nb)
    qq = each(lambda x, n: x + mm(x, n), qq, nb2)
    tinvp = each(mm, qq, dinv)
    rhs = [[jnp.concatenate([vn[ci * c:(ci + 1) * c, (2 * g + hh) * hd:(2 * g + hh + 1) * hd] * betas[u][hh],
                             kcs[u] * (betas[u][hh] * egs[u][hh])], axis=1) for hh in range(2)]
           for u, (ci, g) in enumerate(units)]
    uw = each(lambda t, r: _dot_bf16(t, _block_diag2(r[0], r[1])), tinvp, rhs)
    wu = [[jnp.concatenate([x[:, (2 * hh + 1) * hd:(2 * hh + 2) * hd], x[:, 2 * hh * hd:(2 * hh + 1) * hd]], axis=1)
           for hh in range(2)] for x in uw]
    qo = each(lambda x, w2: _dot_bf16(x, _block_diag2(w2[0], w2[1])), qkd, wu)
    pre = {}
    for u, (ci, g) in enumerate(units):
        for hh in range(2):
            rb = _dot_tn(kcs[u] * jnp.exp(glast[u][hh] - gcs[u][hh]), wu[u][hh])
            q_eff = qcs[u] * egs[u][hh] - qo[u][:, 2 * hh * hd:(2 * hh + 1) * hd]
            rq = jnp.concatenate([rb[:, :hd], q_eff], axis=0).astype(BF16)
            pre[ci, 2 * g + hh] = (rq, rb[:, hd:], qo[u][:, (2 * hh + 1) * hd:(2 * hh + 2) * hd], jnp.exp(glast[u][hh]))

    states = [state[vh] for vh in range(n_v)]
    for ci in range(n_chunks):
        r0 = ci * c
        for vh in range(n_v):
            rq, b_c, o_c, cd = pre[ci, vh]
            s = states[vh]
            res = jnp.dot(rq, s.astype(BF16), preferred_element_type=F32)
            states[vh] = s * cd - res[:hd] + b_c
            out = res[hd:] + o_c
            out = out * lax.rsqrt(jnp.mean(out * out, axis=-1, keepdims=True) + RMS_EPS) * norm_w
            zc = z_ref[r0:r0 + c, vh * hd:(vh + 1) * hd].astype(F32)
            out = out * (zc * jax.nn.sigmoid(zc))
            o_ref[r0:r0 + c, vh * hd:(vh + 1) * hd] = out.astype(o_ref.dtype)
    for vh in range(n_v):
        state[vh] = states[vh]


def gdn_delta(proj, conv_w, gates, norm_w, batch, seq, tb=GDN_TB, n_qk=GDN_QK_PER_STEP):
    t = proj.shape[0]
    tb = min(tb, seq)
    nblk = seq // tb
    hd = HEAD_DIM
    qw = n_qk * hd
    vw = 2 * qw
    kq = GDN_QK_DIM // qw
    v0 = (2 * GDN_QK_DIM) // vw
    z0 = GDN_CONV_DIM // vw
    return pl.pallas_call(
        _gdn_kernel,
        out_shape=jax.ShapeDtypeStruct((t, GDN_V_DIM), BF16),
        grid=(batch, GDN_QK_HEADS // n_qk, nblk),
        in_specs=[
            pl.BlockSpec((tb, qw), lambda b, h, i: (b * nblk + i, h)),
            pl.BlockSpec((tb, qw), lambda b, h, i: (b * nblk + i, kq + h)),
            pl.BlockSpec((tb, vw), lambda b, h, i: (b * nblk + i, v0 + h)),
            pl.BlockSpec((tb, vw), lambda b, h, i: (b * nblk + i, z0 + h)),
            pl.BlockSpec((GDN_CONV, qw), lambda b, h, i: (0, h)),
            pl.BlockSpec((GDN_CONV, qw), lambda b, h, i: (0, kq + h)),
            pl.BlockSpec((GDN_CONV, vw), lambda b, h, i: (0, v0 + h)),
            pl.BlockSpec((1, tb, LANES), lambda b, h, i: (h, b * nblk + i, 0)),
            pl.BlockSpec((1, hd), lambda b, h, i: (0, 0)),
        ],
        out_specs=pl.BlockSpec((tb, vw), lambda b, h, i: (b * nblk + i, h)),
        scratch_shapes=[
            pltpu.VMEM((tb + SUBLANES, qw), F32),
            pltpu.VMEM((tb + SUBLANES, qw), F32),
            pltpu.VMEM((tb + SUBLANES, vw), F32),
            pltpu.VMEM((tb, qw), F32),
            pltpu.VMEM((tb, qw), F32),
            pltpu.VMEM((tb, vw), F32),
            pltpu.VMEM((2 * n_qk, hd, hd), F32),
        ],
        compiler_params=_params("parallel", "parallel", "arbitrary"),
        name="gdn_delta",
    )(proj, proj, proj, proj, conv_w, conv_w, conv_w, gates, norm_w.reshape(1, hd))


def gated_deltanet_layer(h32, w_in, conv_w, a_log, dt_bias, norm_w, w_out, gain, bias, batch, seq):
    main = GDN_CONV_DIM + GDN_V_DIM
    gates, h16 = gdn_gates(h32, w_in[:, main:], a_log, dt_bias)
    proj = matmul(h16, w_in[:, :main].astype(BF16))
    o = gdn_delta(proj, conv_w, gates, norm_w, batch, seq)
    return proj_residual_ln(o, w_out.astype(BF16), h32, gain, bias)


DIL_STEP_TOKENS = {1: 512, 4: 512, 16: 2048}
DIL_STEP_HEADS = {1: 8, 4: 8, 16: 2}
DIL_CHAINS = 8


def _dil_attn_kernel(q_ref, kp_ref, ko_ref, vp_ref, vo_ref, bias_ref, o_ref, lse_ref, qs, ks, vs, os_, *, dil):
    blk = DIL_BLOCK
    span = q_ref.shape[0]
    hp = q_ref.shape[1] // HEAD_DIM
    halo = dil * blk
    m_blocks = span // halo
    first_span = pl.program_id(1) == 0
    hg = pl.program_id(2)
    for h in range(hp):
        s = slice(h * HEAD_DIM, (h + 1) * HEAD_DIM)
        qs[h] = q_ref[:, s].astype(F32)
        ks[h, 0:halo, :] = kp_ref[:, s].astype(F32)
        ks[h, halo:halo + span, :] = ko_ref[:, s].astype(F32)
        vs[h, 0:halo, :] = vp_ref[:, s].astype(F32)
        vs[h, halo:halo + span, :] = vo_ref[:, s].astype(F32)

    qi = lax.broadcasted_iota(jnp.int32, (blk, blk), 0)
    kj = lax.broadcasted_iota(jnp.int32, (blk, blk), 1)
    band_prev = qi <= kj
    mask_own = kj <= qi
    lane = lax.broadcasted_iota(jnp.int32, (blk, LANES), 1)
    scale = HEAD_DIM ** -0.5

    def rows(start):
        return pl.ds(start, blk, stride=dil) if dil > 1 else pl.ds(start, blk)

    @pl.when(hg == 0)
    def _():
        lse_ref[...] = jnp.zeros(lse_ref.shape, F32)

    combos = [(r, j, h) for r in range(dil) for j in range(m_blocks) for h in range(hp)]
    lse_tiles = {}
    for c0 in range(0, len(combos), DIL_CHAINS):
        group = combos[c0:c0 + DIL_CHAINS]
        q = [qs[h, rows(r + halo * j), :] for r, j, h in group]
        bias = [bias_ref[hg * hp + h] for _, _, h in group]
        mprev = [band_prev & jnp.logical_not(first_span) if j == 0 else band_prev for _, j, _ in group]
        s_prev = [jnp.where(mp, _dot_nt(qq, ks[h, rows(r + halo * j), :]) * scale + bb[:, :blk], -jnp.inf)
                  for (r, j, h), qq, bb, mp in zip(group, q, bias, mprev)]
        s_own = [jnp.where(mask_own, _dot_nt(qq, ks[h, rows(r + halo * (j + 1)), :]) * scale + bb[:, blk:], -jnp.inf)
                 for (r, j, h), qq, bb in zip(group, q, bias)]
        m = [jnp.maximum(jnp.max(sp, axis=-1, keepdims=True), jnp.max(so, axis=-1, keepdims=True))
             for sp, so in zip(s_prev, s_own)]
        p_prev = [jnp.exp(sp - mm) for sp, mm in zip(s_prev, m)]
        p_own = [jnp.exp(so - mm) for so, mm in zip(s_own, m)]
        den = [jnp.sum(pp, axis=-1, keepdims=True) + jnp.sum(po, axis=-1, keepdims=True)
               for pp, po in zip(p_prev, p_own)]
        o = [_dot_bf16(pp, vs[h, rows(r + halo * j), :]) + _dot_bf16(po, vs[h, rows(r + halo * (j + 1)), :])
             for (r, j, h), pp, po in zip(group, p_prev, p_own)]
        for (r, j, h), oo, dd, mm in zip(group, o, den, m):
            os_[h, rows(r + halo * j), :] = oo / dd
            prev_tile = lse_tiles[r, j] if (r, j) in lse_tiles else lse_ref[rows(r + halo * j), :]
            lse_tiles[r, j] = jnp.where(lane == hg * hp + h, mm + jnp.log(dd), prev_tile)
            if h == hp - 1:
                lse_ref[rows(r + halo * j), :] = lse_tiles.pop((r, j))
    for h in range(hp):
        o_ref[:, h * HEAD_DIM:(h + 1) * HEAD_DIM] = os_[h]


def dilated_group_attention(q, k, v, bias, dil, batch, seq):
    t, kvd = q.shape
    span = min(DIL_STEP_TOKENS[dil], seq)
    hp = DIL_STEP_HEADS[dil]
    halo = dil * DIL_BLOCK
    assert seq % span == 0 and span % halo == 0 and DIL_HEADS % hp == 0
    n_span = seq // span
    ratio = span // halo
    cols = hp * HEAD_DIM
    own = lambda b, i, g: (b * n_span + i, g)
    prev = lambda b, i, g: (jnp.maximum((b * n_span + i) * ratio - 1, 0), g)
    o, lse = pl.pallas_call(
        functools.partial(_dil_attn_kernel, dil=dil),
        out_shape=(jax.ShapeDtypeStruct((t, kvd), F32),
                   jax.ShapeDtypeStruct((t, LANES), F32)),
        grid=(batch, n_span, DIL_HEADS // hp),
        in_specs=[
            pl.BlockSpec((span, cols), own),
            pl.BlockSpec((halo, cols), prev),
            pl.BlockSpec((span, cols), own),
            pl.BlockSpec((halo, cols), prev),
            pl.BlockSpec((span, cols), own),
            pl.BlockSpec((DIL_HEADS, DIL_BLOCK, 2 * DIL_BLOCK), lambda b, i, g: (0, 0, 0)),
        ],
        out_specs=(pl.BlockSpec((span, cols), own), pl.BlockSpec((span, LANES), lambda b, i, g: (b * n_span + i, 0))),
        scratch_shapes=[pltpu.VMEM((hp, span, HEAD_DIM), F32), pltpu.VMEM((hp, halo + span, HEAD_DIM), F32),
                        pltpu.VMEM((hp, halo + span, HEAD_DIM), F32), pltpu.VMEM((hp, span, HEAD_DIM), F32)],
        compiler_params=_params("parallel", "parallel", "arbitrary"),
        name=f"dilated_attention_d{dil}",
    )(q, k, k, v, v, bias)
    return o, lse


def _dil_out_kernel(o0_ref, o1_ref, o2_ref, l0_ref, l1_ref, l2_ref, w_ref, h_ref, gain_ref, bias_ref,
                    o32_ref, o16_ref, orow_ref, mix_ref):
    l0, l1, l2 = l0_ref[...], l1_ref[...], l2_ref[...]
    m = jnp.maximum(jnp.maximum(l0, l1), l2)
    e0, e1, e2 = jnp.exp(l0 - m), jnp.exp(l1 - m), jnp.exp(l2 - m)
    inv = 1.0 / (e0 + e1 + e2)
    w0, w1, w2 = e0 * inv, e1 * inv, e2 * inv
    for hd in range(DIL_HEADS):
        sl = slice(hd * HEAD_DIM, (hd + 1) * HEAD_DIM)
        mix_ref[:, sl] = (w0[:, hd:hd + 1] * o0_ref[:, sl] + w1[:, hd:hd + 1] * o1_ref[:, sl]
                          + w2[:, hd:hd + 1] * o2_ref[:, sl])
    mix = jnp.dot(mix_ref[...].astype(BF16), w_ref[...], preferred_element_type=F32)
    y = _layer_norm_rows(DEEPNORM_ALPHA * h_ref[...] + mix, gain_ref[...], bias_ref[...])
    o32_ref[...] = y
    o16_ref[...] = y.astype(BF16)
    _store_row_contiguous(orow_ref, y)


def dilated_out(outs, lses, w_o, h, gain, bias, tm=LN_TM):
    m, kvd = outs[0].shape
    d = w_o.shape[1]
    tm = min(tm, m)
    row = lambda i: (i, 0)
    fixed = lambda i: (0, 0)
    return pl.pallas_call(
        _dil_out_kernel,
        out_shape=(jax.ShapeDtypeStruct((m, d), F32), jax.ShapeDtypeStruct((m, d), BF16),
                   jax.ShapeDtypeStruct((m * (d // LANES), LANES), F32)),
        grid=(m // tm,),
        in_specs=[pl.BlockSpec((tm, kvd), row)] * 3 + [pl.BlockSpec((tm, LANES), row)] * 3 + [
            pl.BlockSpec((kvd, d), fixed), pl.BlockSpec((tm, d), row),
            pl.BlockSpec((1, d), fixed), pl.BlockSpec((1, d), fixed)],
        out_specs=(pl.BlockSpec((tm, d), row), pl.BlockSpec((tm, d), row), pl.BlockSpec((tm * (d // LANES), LANES), row)),
        scratch_shapes=[pltpu.VMEM((tm, kvd), F32)],
        compiler_params=_params("parallel"),
        name="dilated_out_ln",
    )(*outs, *lses, w_o, h, gain.reshape(1, d), bias.reshape(1, d))


def _t5_bucket(dist):
    exact = REL_BUCKETS // 2
    distf = jnp.maximum(dist, exact).astype(F32)
    large = exact + (jnp.log(distf / exact) / math.log(REL_MAX_DIST / exact)
                     * (REL_BUCKETS - exact)).astype(jnp.int32)
    return jnp.where(dist < exact, dist, jnp.minimum(large, REL_BUCKETS - 1))


def _band_bias(rel_bias_g, dil):
    qi = jnp.arange(DIL_BLOCK)[:, None]
    kj = jnp.arange(2 * DIL_BLOCK)[None, :]
    steps = DIL_BLOCK + qi - kj
    bucket = _t5_bucket(jnp.maximum(steps, 0) * dil)
    onehot = (bucket[:, :, None] == jnp.arange(REL_BUCKETS)[None, None, :]).astype(F32)
    return jnp.einsum('qkb,bh->hqk', onehot, rel_bias_g.astype(F32), precision=lax.Precision.HIGHEST)


def _router_kernel(h_ref, w_ref, b_ref, o_ref, cnt_ref, running):
    @pl.when(pl.program_id(0) == 0)
    def _():
        running[...] = jnp.zeros(running.shape, F32)

    logits = _dot_split(h_ref[...], w_ref) + b_ref[...]
    lane = lax.broadcasted_iota(jnp.int32, logits.shape, 1)
    neg = -jnp.inf
    big = jnp.int32(LANES)
    is_group = lane < MOE_GROUPS
    gl = jnp.where(is_group, logits, neg)
    gmax = jnp.max(gl, axis=-1, keepdims=True)
    gidx = jnp.min(jnp.where(gl == gmax, lane, big), axis=-1, keepdims=True)
    gden = jnp.sum(jnp.exp(gl - gmax), axis=-1, keepdims=True)
    group_gate = 1.0 / gden
    lo = MOE_GROUPS + gidx * MOE_EXPERTS_PER_GROUP
    in_group = (lane >= lo) & (lane < lo + MOE_EXPERTS_PER_GROUP)
    ll = jnp.where(in_group, logits, neg)
    m1 = jnp.max(ll, axis=-1, keepdims=True)
    i1 = jnp.min(jnp.where(ll == m1, lane, big), axis=-1, keepdims=True)
    rest = jnp.where(lane == i1, neg, ll)
    m2 = jnp.max(rest, axis=-1, keepdims=True)
    i2 = jnp.min(jnp.where(rest == m2, lane, big), axis=-1, keepdims=True)
    den = jnp.sum(jnp.exp(ll - m1), axis=-1, keepdims=True)
    p1 = 1.0 / den
    p2 = jnp.exp(m2 - m1) / den
    g1 = group_gate * p1 / (p1 + p2)
    g2 = group_gate * p2 / (p1 + p2)
    id1 = (i1 - MOE_GROUPS).astype(F32)
    id2 = (i2 - MOE_GROUPS).astype(F32)
    tm = logits.shape[0]
    oh1 = (lane == i1).astype(F32)
    oh2 = (lane == i2).astype(F32)
    both = oh1 + oh2
    r_i = lax.broadcasted_iota(jnp.int32, (tm, tm), 0)
    c_i = lax.broadcasted_iota(jnp.int32, (tm, tm), 1)
    before = jnp.dot((c_i < r_i).astype(BF16), both.astype(BF16), preferred_element_type=F32) + running[...]
    rank1 = jnp.sum(before * oh1, axis=-1, keepdims=True)
    rank2 = jnp.sum(before * oh2, axis=-1, keepdims=True)
    running[...] = running[...] + jnp.sum(both, axis=0, keepdims=True)
    cnt_ref[...] = running[...]
    o_ref[...] = jnp.where(lane == 0, id1, jnp.where(lane == 1, id2, jnp.where(lane == 2, g1, jnp.where(
        lane == 3, g2, jnp.where(lane == 4, rank1, rank2)))))


def moe_router(h, w_group, b_group, w_expert, b_expert, tm=ROUTER_TM):
    m, d = h.shape
    tm = min(tm, m)
    pad = LANES - MOE_GROUPS - MOE_EXPERTS
    w = jnp.pad(jnp.concatenate([w_group, w_expert], axis=1), ((0, 0), (0, pad)))
    b = jnp.pad(jnp.concatenate([b_group, b_expert]), (0, pad)).reshape(1, LANES)
    out, cnt = pl.pallas_call(
        _router_kernel,
        out_shape=(jax.ShapeDtypeStruct((m, LANES), F32), jax.ShapeDtypeStruct((1, LANES), F32)),
        grid=(m // tm,),
        in_specs=[pl.BlockSpec((tm, d), lambda i: (i, 0)),
                  pl.BlockSpec((d, 2 * LANES), lambda i: (0, 0)),
                  pl.BlockSpec((1, LANES), lambda i: (0, 0))],
        out_specs=(pl.BlockSpec((tm, LANES), lambda i: (i, 0)), pl.BlockSpec((1, LANES), lambda i: (0, 0))),
        scratch_shapes=[pltpu.VMEM((1, LANES), F32)],
        compiler_params=_params("arbitrary"),
        name="moe_router",
    )(h, _hi_lo_columns(w), b)
    k = MOE_TOP_K
    counts = cnt[0, MOE_GROUPS:MOE_GROUPS + MOE_EXPERTS].astype(jnp.int32)
    return out[:, :k].astype(jnp.int32), out[:, k:2 * k], out[:, 2 * k:3 * k].astype(jnp.int32), counts


def _gather_rows_start(src_hbm, idx_ref, idx_base, buf, sem, unrolled, row_tiles=None):
    if row_tiles is None:
        groups = buf.shape[0]

        def copy(g, j):
            pltpu.make_async_copy(src_hbm.at[pl.ds(idx_ref[idx_base + g * SUBLANES + j], 1)],
                                  buf.at[g, pl.ds(j, 1)], sem).start()
    else:
        w = row_tiles
        groups = buf.shape[0] // (w * SUBLANES)

        def copy(g, j):
            r = g * SUBLANES + j
            pltpu.make_async_copy(src_hbm.at[pl.ds(pl.multiple_of(idx_ref[idx_base + r], w), w)],
                                  buf.at[pl.ds(pl.multiple_of(r * w, w), w)], sem).start()

    if unrolled:
        for g in range(groups):
            for j in range(SUBLANES):
                copy(g, j)
    else:
        def body(g, carry):
            for j in range(SUBLANES):
                copy(g, j)
            return carry
        lax.fori_loop(0, groups, body, 0)


def _gather_rows_wait(buf, sem):
    pltpu.make_async_copy(buf, buf, sem).wait()


def _gathered_blocks(src_hbm, idx_ref, bufs, sems, compute, row_tiles=None, n_active=None):
    i = pl.program_id(0)
    n = pl.num_programs(0) if n_active is None else n_active
    if row_tiles is None:
        rows = bufs[0].shape[0] * bufs[0].shape[1]
        load = lambda buf: buf[...].reshape(rows, buf.shape[2])
    else:
        rows = bufs[0].shape[0] // row_tiles
        load = lambda buf: _load_row_contiguous(buf, rows)

    @pl.when(i == 0)
    def _():
        _gather_rows_start(src_hbm, idx_ref, 0, bufs[0], sems.at[0], False, row_tiles)

    def step(s):
        _gather_rows_wait(bufs[s], sems.at[s])
        _gather_rows_start(src_hbm, idx_ref, jnp.minimum(i + 1, n - 1) * rows, bufs[1 - s], sems.at[1 - s],
                           True, row_tiles)
        compute(load(bufs[s]))

        @pl.when(i == n - 1)
        def _():
            _gather_rows_wait(bufs[1 - s], sems.at[1 - s])

    for s in range(2):
        pl.when((i % 2 == s) & (i < n))(functools.partial(step, s))


def _expert_kernel(bexp_ref, rtok_ref, nused_ref, h_hbm, w1_ref, w3_ref, w2_ref, y_ref,
                   xbuf_a, xbuf_b, sems, wb1, wb3, wb2):
    i = pl.program_id(0)

    @pl.when((i < nused_ref[0]) & ((i == 0) | (bexp_ref[i] != bexp_ref[jnp.maximum(i - 1, 0)])))
    def _():
        wb1[...] = w1_ref[0, 0].astype(BF16)
        wb3[...] = w3_ref[0, 0].astype(BF16)
        wb2[...] = w2_ref[0, 0].astype(BF16)

    def compute(x):
        x = x.astype(BF16)
        h1 = jnp.dot(x, wb1[...], preferred_element_type=F32)
        h3 = jnp.dot(x, wb3[...], preferred_element_type=F32)
        hmid = (h1 * jax.nn.sigmoid(h1) * h3).astype(BF16)
        y_ref[...] = jnp.dot(hmid, wb2[...], preferred_element_type=F32)

    _gathered_blocks(h_hbm, rtok_ref, (xbuf_a, xbuf_b), sems, compute, row_tiles=w1_ref.shape[2] // LANES,
                     n_active=nused_ref[0])

    @pl.when(i >= nused_ref[0])
    def _():
        y_ref[...] = jnp.zeros(y_ref.shape, y_ref.dtype)


def moe_experts(h_rows, row_token, block_expert, n_used, w1, w3, w2, layer, tm=MOE_TM):
    n_rows = row_token.shape[0]
    n_blocks = n_rows // tm
    d = w1.shape[2]
    row_tiles = d // LANES
    ff = w1.shape[3]
    grid_spec = pltpu.PrefetchScalarGridSpec(
        num_scalar_prefetch=3,
        grid=(n_blocks,),
        in_specs=[
            pl.BlockSpec(memory_space=pl.ANY),
            pl.BlockSpec((1, 1, d, ff), lambda i, be, rt, nu: (layer, be[i], 0, 0)),
            pl.BlockSpec((1, 1, d, ff), lambda i, be, rt, nu: (layer, be[i], 0, 0)),
            pl.BlockSpec((1, 1, ff, d), lambda i, be, rt, nu: (layer, be[i], 0, 0)),
        ],
        out_specs=pl.BlockSpec((tm, d), lambda i, be, rt, nu: (i, 0)),
        scratch_shapes=[pltpu.VMEM((tm * row_tiles, LANES), F32), pltpu.VMEM((tm * row_tiles, LANES), F32),
                        pltpu.SemaphoreType.DMA((2,)),
                        pltpu.VMEM((d, ff), BF16), pltpu.VMEM((d, ff), BF16), pltpu.VMEM((ff, d), BF16)],
    )
    return pl.pallas_call(
        _expert_kernel,
        out_shape=jax.ShapeDtypeStruct((n_rows, d), F32),
        grid_spec=grid_spec,
        compiler_params=_params("arbitrary"),
        name="moe_experts",
    )(block_expert, row_token, n_used, h_rows, w1, w3, w2)


def _combine_kernel(dest_ref, y_hbm, gate_ref, h_ref, gain_ref, bias_ref, o32_ref, o16_ref, ybuf_a, ybuf_b, sems):
    tm = h_ref.shape[0]

    def compute(yy):
        gates = gate_ref[...]
        ffn = yy[0:tm, :] * gates[:, 0:1] + yy[tm:, :] * gates[:, 1:2]
        y = _layer_norm_rows(DEEPNORM_ALPHA * h_ref[...] + ffn, gain_ref[...], bias_ref[...])
        o32_ref[...] = y
        o16_ref[...] = y.astype(BF16)

    _gathered_blocks(y_hbm, dest_ref, (ybuf_a, ybuf_b), sems, compute)


def moe_combine_ln(y_rows, dest, gates, h, gain, bias, tm=COMBINE_TM):
    m, d = h.shape
    tm = min(tm, m)
    grid_spec = pltpu.PrefetchScalarGridSpec(
        num_scalar_prefetch=1,
        grid=(m // tm,),
        in_specs=[
            pl.BlockSpec(memory_space=pl.ANY),
            pl.BlockSpec((tm, MOE_TOP_K), lambda i, de: (i, 0)),
            pl.BlockSpec((tm, d), lambda i, de: (i, 0)),
            pl.BlockSpec((1, d), lambda i, de: (0, 0)),
            pl.BlockSpec((1, d), lambda i, de: (0, 0)),
        ],
        out_specs=(pl.BlockSpec((tm, d), lambda i, de: (i, 0)),
                   pl.BlockSpec((tm, d), lambda i, de: (i, 0))),
        scratch_shapes=[pltpu.VMEM((MOE_TOP_K * tm // SUBLANES, SUBLANES, d), F32),
                        pltpu.VMEM((MOE_TOP_K * tm // SUBLANES, SUBLANES, d), F32), pltpu.SemaphoreType.DMA((2,))],
    )
    return pl.pallas_call(
        _combine_kernel,
        out_shape=(jax.ShapeDtypeStruct((m, d), F32), jax.ShapeDtypeStruct((m, d), BF16)),
        grid_spec=grid_spec,
        compiler_params=_params("arbitrary"),
        name="moe_combine_ln",
    )(dest.reshape(-1), y_rows, gates, h, gain.reshape(1, d), bias.reshape(1, d))


def moe_layer(h32, h_rows, w_group, b_group, w_expert, b_expert, w1, w3, w2, layer, gain, bias, tm=MOE_TM):
    t = h32.shape[0]
    ids, gates, ranks, counts = moe_router(h32, w_group, b_group, w_expert, b_expert)
    eid = ids.reshape(-1)
    n_pairs = eid.shape[0]
    padded = (counts + tm - 1) // tm * tm
    seg_end = jnp.cumsum(padded)
    seg_start = seg_end - padded
    onehot = eid[:, None] == jnp.arange(MOE_EXPERTS, dtype=jnp.int32)[None, :]
    dest = (jnp.sum(jnp.where(onehot, seg_start[None, :], 0), axis=1) + ranks.reshape(-1)).astype(jnp.int32)
    n_blocks = -(-n_pairs // tm) + MOE_EXPERTS
    n_rows = n_blocks * tm
    token_of_pair = (jnp.arange(n_pairs, dtype=jnp.int32) // MOE_TOP_K)
    row_token = jnp.zeros((n_rows,), jnp.int32).at[dest].set(token_of_pair * (h32.shape[1] // LANES))
    block_start = jnp.arange(n_blocks, dtype=jnp.int32) * tm
    block_expert = jnp.minimum(jnp.sum((seg_end[None, :] <= block_start[:, None]).astype(jnp.int32), axis=1),
                               MOE_EXPERTS - 1)
    n_used = (seg_end[-1:] // tm).astype(jnp.int32)
    y_rows = moe_experts(h_rows, row_token, block_expert, n_used, w1, w3, w2, layer, tm=tm)
    ctm = min(COMBINE_TM, t)
    dest_tiles = jnp.transpose(dest.reshape(t // ctm, ctm, MOE_TOP_K), (0, 2, 1))
    return moe_combine_ln(y_rows, dest_tiles, gates, h32, gain, bias, tm=ctm)


def kernel(x, gdn_w_in, gdn_conv, gdn_a_log, gdn_dt_bias, gdn_norm, gdn_w_out, kv_w_k, kv_w_v, dil_w_q, dil_w_o,
           rel_bias, ln_gain, ln_bias, moe_w_group, moe_b_group, moe_w_expert, moe_b_expert, moe_w1, moe_w3, moe_w2):
    batch, seq, d = x.shape
    t = batch * seq
    h32 = x.reshape(t, d)
    h16 = None
    k = v = None
    biases = None
    for layer in range(DEPTH):
        if layer < N_A_LAYERS:
            h32, h16, h_rows = gated_deltanet_layer(
                h32, gdn_w_in[layer], gdn_conv[layer], gdn_a_log[layer], gdn_dt_bias[layer],
                gdn_norm[layer], gdn_w_out[layer], ln_gain[layer, 0], ln_bias[layer, 0], batch, seq)
        else:
            if layer == N_A_LAYERS:
                k = matmul(h16, kv_w_k.astype(BF16), out_dtype=BF16)
                v = matmul(h16, kv_w_v.astype(BF16), out_dtype=BF16)
                biases = [_band_bias(rel_bias[:, gi * DIL_HEADS:(gi + 1) * DIL_HEADS], dil)
                          for gi, (_, dil) in enumerate(DIL_CONFIGS)]
            j = layer - N_A_LAYERS
            outs, lses = [], []
            for gi, (_, dil) in enumerate(DIL_CONFIGS):
                q = matmul(h16, dil_w_q[j][:, gi * DIL_KV_DIM:(gi + 1) * DIL_KV_DIM].astype(BF16), out_dtype=BF16)
                o, lse = dilated_group_attention(q, k, v, biases[gi], dil, batch, seq)
                outs.append(o)
                lses.append(lse)
            h32, h16, h_rows = dilated_out(outs, lses, dil_w_o[j].astype(BF16), h32, ln_gain[layer, 0], ln_bias[layer, 0])
        h32, h16 = moe_layer(h32, h_rows, moe_w_group[layer], moe_b_group[layer], moe_w_expert[layer], moe_b_expert[layer],
                             moe_w1, moe_w3, moe_w2, layer, ln_gain[layer, 1], ln_bias[layer, 1])
    return h32.reshape(batch, seq, d)
```

```python
import functools
import math

import jax
import jax.numpy as jnp
from jax import lax
from jax.experimental import pallas as pl
from jax.experimental.pallas import tpu as pltpu

F32 = jnp.float32
BF16 = jnp.bfloat16

DEPTH = 4
N_A_LAYERS = DEPTH // 2
HEAD_DIM = 128
GDN_QK_HEADS = 16
GDN_V_HEADS = 32
GDN_CONV = 4
GDN_CHUNK = 64
GDN_QK_DIM = GDN_QK_HEADS * HEAD_DIM
GDN_V_DIM = GDN_V_HEADS * HEAD_DIM
GDN_CONV_DIM = 2 * GDN_QK_DIM + GDN_V_DIM
DIL_CONFIGS = ((128, 1), (512, 4), (2048, 16))
N_DIL = len(DIL_CONFIGS)
DIL_HEADS = 8
DIL_BLOCK = 128
DIL_KV_DIM = DIL_HEADS * HEAD_DIM
REL_BUCKETS = 32
REL_MAX_DIST = 2048
MOE_GROUPS = 4
MOE_EXPERTS_PER_GROUP = 8
MOE_EXPERTS = MOE_GROUPS * MOE_EXPERTS_PER_GROUP
MOE_TOP_K = 2
DEEPNORM_ALPHA = (2 * DEPTH) ** 0.25
LN_EPS = 1e-5
RMS_EPS = 1e-6
L2_EPS = 1e-6

LANES = 128
SUBLANES = 8
VMEM_LIMIT_BYTES = 56 * 1024 * 1024

MM_TM = 1024
MM_TN = 1024
LN_TM = 256
GATE_TM = 512
GDN_TB = 512
GDN_QK_PER_STEP = 4
MOE_TM = 256
ROUTER_TM = 512
COMBINE_TM = 256


def _params(*semantics):
    return pltpu.CompilerParams(dimension_semantics=semantics, vmem_limit_bytes=VMEM_LIMIT_BYTES)


def _split_bf16(x):
    hi = x.astype(BF16)
    return hi, (x - hi.astype(F32)).astype(BF16)


def _dot_split(x, w2_ref):
    n = w2_ref.shape[1] // 2
    hi, lo = _split_bf16(x)
    a = jnp.dot(hi, w2_ref[...], preferred_element_type=F32)
    b = jnp.dot(lo, w2_ref[:, :n], preferred_element_type=F32)
    return a[:, :n] + a[:, n:] + b


def _hi_lo_columns(w):
    hi = w.astype(BF16)
    return jnp.concatenate([hi, (w - hi.astype(F32)).astype(BF16)], axis=1)


def _store_row_contiguous(ref, y):
    rows, width = y.shape
    w = width // LANES
    for c in range(w):
        ref[pl.ds(c, rows, stride=w), :] = y[:, c * LANES:(c + 1) * LANES]


def _load_row_contiguous(ref, rows):
    w = ref.shape[0] // rows
    return jnp.concatenate([ref[pl.ds(c, rows, stride=w), :] for c in range(w)], axis=1)


def _layer_norm_rows(y, gain, bias):
    mu = jnp.mean(y, axis=-1, keepdims=True)
    yc = y - mu
    var = jnp.mean(yc * yc, axis=-1, keepdims=True)
    return yc * lax.rsqrt(var + LN_EPS) * gain + bias


def _matmul_kernel(a_ref, b_ref, o_ref):
    o_ref[...] = jnp.dot(a_ref[...].astype(BF16), b_ref[...],
                         preferred_element_type=F32).astype(o_ref.dtype)


def matmul(a, b, out_dtype=F32, tm=MM_TM, tn=MM_TN):
    m, k = a.shape
    n = b.shape[1]
    tm, tn = min(tm, m), min(tn, n)
    assert m % tm == 0 and n % tn == 0
    return pl.pallas_call(
        _matmul_kernel,
        out_shape=jax.ShapeDtypeStruct((m, n), out_dtype),
        grid=(n // tn, m // tm),
        in_specs=[pl.BlockSpec((tm, k), lambda j, i: (i, 0)),
                  pl.BlockSpec((k, tn), lambda j, i: (0, j))],
        out_specs=pl.BlockSpec((tm, tn), lambda j, i: (i, j)),
        compiler_params=_params("parallel", "parallel"),
        name="matmul",
    )(a, b)


def _proj_ln_kernel(a_ref, w_ref, h_ref, gain_ref, bias_ref, o32_ref, o16_ref, orow_ref):
    mix = jnp.dot(a_ref[...].astype(BF16), w_ref[...], preferred_element_type=F32)
    y = _layer_norm_rows(DEEPNORM_ALPHA * h_ref[...] + mix, gain_ref[...], bias_ref[...])
    o32_ref[...] = y
    o16_ref[...] = y.astype(BF16)
    _store_row_contiguous(orow_ref, y)


def proj_residual_ln(a, w, h, gain, bias, tm=LN_TM):
    m, k = a.shape
    d = w.shape[1]
    tm = min(tm, m)
    assert m % tm == 0
    return pl.pallas_call(
        _proj_ln_kernel,
        out_shape=(jax.ShapeDtypeStruct((m, d), F32), jax.ShapeDtypeStruct((m, d), BF16),
                   jax.ShapeDtypeStruct((m * (d // LANES), LANES), F32)),
        grid=(m // tm,),
        in_specs=[pl.BlockSpec((tm, k), lambda i: (i, 0)),
                  pl.BlockSpec((k, d), lambda i: (0, 0)),
                  pl.BlockSpec((tm, d), lambda i: (i, 0)),
                  pl.BlockSpec((1, d), lambda i: (0, 0)),
                  pl.BlockSpec((1, d), lambda i: (0, 0))],
        out_specs=(pl.BlockSpec((tm, d), lambda i: (i, 0)),
                   pl.BlockSpec((tm, d), lambda i: (i, 0)),
                   pl.BlockSpec((tm * (d // LANES), LANES), lambda i: (i, 0))),
        compiler_params=_params("parallel"),
        name="proj_residual_ln",
    )(a, w, h, gain.reshape(1, d), bias.reshape(1, d))


def _gdn_gate_kernel(h_ref, w_ref, alog_ref, dt_ref, o_ref, h16_ref):
    tm = h_ref.shape[0]
    n_groups = o_ref.shape[0]
    h16_ref[...] = h_ref[...].astype(BF16)
    width = 2 * GDN_V_HEADS // n_groups
    ba = _dot_split(h_ref[...], w_ref)
    xa = ba + dt_ref[...]
    softplus = jnp.maximum(xa, 0.0) + jnp.log(1.0 + jnp.exp(-jnp.abs(xa)))
    g = -jnp.exp(alog_ref[...]) * softplus
    row = lax.broadcasted_iota(jnp.int32, (tm, tm), 0)
    col = lax.broadcasted_iota(jnp.int32, (tm, tm), 1)
    tri = ((row // GDN_CHUNK == col // GDN_CHUNK) & (col <= row)).astype(BF16)
    g_hi, g_lo = _split_bf16(g)
    gc2 = jnp.dot(tri, jnp.concatenate([g_hi, g_lo], axis=1), preferred_element_type=F32)
    gc = gc2[:, :LANES] + gc2[:, LANES:]
    lane = lax.broadcasted_iota(jnp.int32, ba.shape, 1)
    slab = jnp.where(lane % width < width // 2, jax.nn.sigmoid(ba), gc)
    for r in range(n_groups):
        o_ref[r] = slab if r == 0 else pltpu.roll(slab, LANES - width * r, axis=1)


def gdn_gates(h, w_ba, a_log, dt_bias, n_qk=GDN_QK_PER_STEP, tm=GATE_TM):
    m, d = h.shape
    tm = min(tm, m)
    v = 2 * n_qk
    n_groups = GDN_V_HEADS // v
    cols = [(v * (l // (2 * v)) + l % (2 * v)) if l % (2 * v) < v else (GDN_V_HEADS + v * (l // (2 * v)) + l % (2 * v) - v)
            for l in range(2 * GDN_V_HEADS)]
    is_decay = jnp.array([l % (2 * v) >= v for l in range(2 * GDN_V_HEADS)])
    head = jnp.array([c % GDN_V_HEADS for c in cols], jnp.int32)
    pad = LANES - 2 * GDN_V_HEADS
    w = jnp.pad(w_ba[:, jnp.array(cols, jnp.int32)], ((0, 0), (0, pad)))
    alog = jnp.pad(jnp.where(is_decay, a_log[head], 0.0), (0, pad)).reshape(1, LANES)
    dt = jnp.pad(jnp.where(is_decay, dt_bias[head], 0.0), (0, pad)).reshape(1, LANES)
    return pl.pallas_call(
        _gdn_gate_kernel,
        out_shape=(jax.ShapeDtypeStruct((n_groups, m, LANES), F32), jax.ShapeDtypeStruct((m, d), BF16)),
        grid=(m // tm,),
        in_specs=[pl.BlockSpec((tm, d), lambda i: (i, 0)),
                  pl.BlockSpec((d, 2 * LANES), lambda i: (0, 0)),
                  pl.BlockSpec((1, LANES), lambda i: (0, 0)),
                  pl.BlockSpec((1, LANES), lambda i: (0, 0))],
        out_specs=(pl.BlockSpec((n_groups, tm, LANES), lambda i: (0, i, 0)), pl.BlockSpec((tm, d), lambda i: (i, 0))),
        compiler_params=_params("parallel"),
        name="gdn_gates",
    )(h, _hi_lo_columns(w), alog, dt)


def _dot_bf16(a, b):
    return jnp.dot(a.astype(BF16), b.astype(BF16), preferred_element_type=F32)


def _dot_nt(a, b):
    return lax.dot_general(a.astype(BF16), b.astype(BF16), (((1,), (1,)), ((), ())),
                           preferred_element_type=F32)


def _dot_tn(a, b):
    return lax.dot_general(a.astype(BF16), b.astype(BF16), (((0,), (0,)), ((), ())),
                           preferred_element_type=F32)


def _block_diag2(x0, x1):
    z = jnp.zeros_like(x0)
    return jnp.concatenate([jnp.concatenate([x0, z], axis=1), jnp.concatenate([z, x1], axis=1)], axis=0)


def _gdn_kernel(q_ref, k_ref, v_ref, z_ref, wq_ref, wk_ref, wv_ref, g_ref, nw_ref,
                o_ref, qx, kx, vx, qn, kn, vn, state):
    blk = pl.program_id(2)
    tb = q_ref.shape[0]
    n_chunks = tb // GDN_CHUNK
    n_qk = q_ref.shape[1] // HEAD_DIM
    c = GDN_CHUNK
    hd = HEAD_DIM
    tail = SUBLANES

    @pl.when(blk == 0)
    def _():
        qx[0:tail, :] = jnp.zeros((tail, qx.shape[1]), F32)
        kx[0:tail, :] = jnp.zeros((tail, kx.shape[1]), F32)
        vx[0:tail, :] = jnp.zeros((tail, vx.shape[1]), F32)
        state[...] = jnp.zeros(state.shape, F32)

    def conv_silu(x_ref, ext, w_ref):
        ext[tail:tail + tb, :] = x_ref[...].astype(F32)
        y = None
        for j in range(GDN_CONV):
            term = w_ref[j:j + 1, :] * ext[pl.ds(tail - (GDN_CONV - 1) + j, tb), :]
            y = term if y is None else y + term
        ext[0:tail, :] = ext[tb:tb + tail, :]
        return y * jax.nn.sigmoid(y)

    def l2n(x):
        heads = [x[:, g * hd:(g + 1) * hd] for g in range(n_qk)]
        return jnp.concatenate([xh * lax.rsqrt(jnp.sum(xh * xh, axis=-1, keepdims=True) + L2_EPS) for xh in heads], axis=1)

    qn[...] = l2n(conv_silu(q_ref, qx, wq_ref)) * (HEAD_DIM ** -0.5)
    kn[...] = l2n(conv_silu(k_ref, kx, wk_ref))
    vn[...] = conv_silu(v_ref, vx, wv_ref)

    row = lax.broadcasted_iota(jnp.int32, (c, 2 * c), 0)
    lane = lax.broadcasted_iota(jnp.int32, (c, 2 * c), 1)
    col = lane & (c - 1)
    head0 = lane < c
    causal = col <= row
    strict = col < row
    in_block = (row // 16) == (col // 16)
    eye = (row == col).astype(F32)
    norm_w = nw_ref[...]

    def mm(x, y):
        yb = jnp.concatenate([jnp.where(head0, y, 0.0), jnp.where(head0, 0.0, y)], axis=0)
        return _dot_bf16(x, yb)

    units = [(ci, g) for ci in range(n_chunks) for g in range(n_qk)]
    each = lambda f, *lists: [f(*xs) for xs in zip(*lists)]
    qcs = [qn[ci * c:(ci + 1) * c, g * hd:(g + 1) * hd] for ci, g in units]
    kcs = [kn[ci * c:(ci + 1) * c, g * hd:(g + 1) * hd] for ci, g in units]
    n_v = 2 * n_qk
    gates = g_ref[0]
    gates_t = gates.T
    betas = [[gates[ci * c:(ci + 1) * c, 2 * g + hh:2 * g + hh + 1] for hh in range(2)] for ci, g in units]
    gcs = [[gates[ci * c:(ci + 1) * c, n_v + 2 * g + hh:n_v + 2 * g + hh + 1] for hh in range(2)] for ci, g in units]

    def packed_row(ci, g):
        w0 = (ci // 2) * 2 * c
        r0, r1 = (gates_t[n_v + 2 * g + hh:n_v + 2 * g + hh + 1, w0:w0 + 2 * c] for hh in range(2))
        if ci % 2 == 0:
            return jnp.where(head0[0:1], r0, pltpu.roll(r1, c, axis=1))
        return jnp.where(head0[0:1], pltpu.roll(r0, c, axis=1), r1)

    grps = [packed_row(ci, g) for ci, g in units]
    egs = [[jnp.exp(g[hh]) for hh in range(2)] for g in gcs]
    glast = [[g[hh][c - 1:c, :] for hh in range(2)] for g in gcs]
    both = each(lambda q, k: _dot_nt(jnp.concatenate([q, k], axis=0), jnp.concatenate([k, k], axis=0)), qcs, kcs)
    decayp = each(lambda g, gr: jnp.exp(jnp.where(causal, jnp.where(head0, g[0], g[1]) - gr, -jnp.inf)), gcs, grps)
    ap = each(lambda b, bo, dc: jnp.where(strict, jnp.where(head0, b[0], b[1]) * bo[c:] * dc, 0.0), betas, both, decayp)
    qkd = each(lambda bo, dc: bo[:c] * dc, both, decayp)
    d = each(lambda a: jnp.where(in_block, a, 0.0), ap)
    e = each(lambda a, dd: a - dd, ap, d)
    n1 = each(lambda dd: -dd, d)
    p = each(lambda n: eye + n, n1)
    n2 = each(mm, n1, n1)
    p = each(lambda pp, n: pp + mm(pp, n), p, n2)
    n4 = each(mm, n2, n2)
    p = each(lambda pp, n: pp + mm(pp, n), p, n4)
    n8 = each(mm, n4, n4)
    dinv = each(lambda pp, n: pp + mm(pp, n), p, n8)
    nb = each(mm, dinv, e)
    nb2 = each(mm, nb, nb)
    qq = each(lambda n: eye - n, nb)
    qq = each(lambda x, n: x + mm(x, n), qq, nb2)
    tinvp = each(mm, qq, dinv)
    rhs = [[jnp.concatenate([vn[ci * c:(ci + 1) * c, (2 * g + hh) * hd:(2 * g + hh + 1) * hd] * betas[u][hh],
                             kcs[u] * (betas[u][hh] * egs[u][hh])], axis=1) for hh in range(2)]
           for u, (ci, g) in enumerate(units)]
    uw = each(lambda t, r: _dot_bf16(t, _block_diag2(r[0], r[1])), tinvp, rhs)
    wu = [[jnp.concatenate([x[:, (2 * hh + 1) * hd:(2 * hh + 2) * hd], x[:, 2 * hh * hd:(2 * hh + 1) * hd]], axis=1)
           for hh in range(2)] for x in uw]
    qo = each(lambda x, w2: _dot_bf16(x, _block_diag2(w2[0], w2[1])), qkd, wu)
    pre = {}
    for u, (ci, g) in enumerate(units):
        for hh in range(2):
            rb = _dot_tn(kcs[u] * jnp.exp(glast[u][hh] - gcs[u][hh]), wu[u][hh])
            q_eff = qcs[u] * egs[u][hh] - qo[u][:, 2 * hh * hd:(2 * hh + 1) * hd]
            rq = jnp.concatenate([rb[:, :hd], q_eff], axis=0).astype(BF16)
            pre[ci, 2 * g + hh] = (rq, rb[:, hd:], qo[u][:, (2 * hh + 1) * hd:(2 * hh + 2) * hd], jnp.exp(glast[u][hh]))

    states = [state[vh] for vh in range(n_v)]
    for ci in range(n_chunks):
        r0 = ci * c
        for vh in range(n_v):
            rq, b_c, o_c, cd = pre[ci, vh]
            s = states[vh]
            res = jnp.dot(rq, s.astype(BF16), preferred_element_type=F32)
            states[vh] = s * cd - res[:hd] + b_c
            out = res[hd:] + o_c
            out = out * lax.rsqrt(jnp.mean(out * out, axis=-1, keepdims=True) + RMS_EPS) * norm_w
            zc = z_ref[r0:r0 + c, vh * hd:(vh + 1) * hd].astype(F32)
            out = out * (zc * jax.nn.sigmoid(zc))
            o_ref[r0:r0 + c, vh * hd:(vh + 1) * hd] = out.astype(o_ref.dtype)
    for vh in range(n_v):
        state[vh] = states[vh]


def gdn_delta(proj, conv_w, gates, norm_w, batch, seq, tb=GDN_TB, n_qk=GDN_QK_PER_STEP):
    t = proj.shape[0]
    tb = min(tb, seq)
    nblk = seq // tb
    hd = HEAD_DIM
    qw = n_qk * hd
    vw = 2 * qw
    kq = GDN_QK_DIM // qw
    v0 = (2 * GDN_QK_DIM) // vw
    z0 = GDN_CONV_DIM // vw
    return pl.pallas_call(
        _gdn_kernel,
        out_shape=jax.ShapeDtypeStruct((t, GDN_V_DIM), BF16),
        grid=(batch, GDN_QK_HEADS // n_qk, nblk),
        in_specs=[
            pl.BlockSpec((tb, qw), lambda b, h, i: (b * nblk + i, h)),
            pl.BlockSpec((tb, qw), lambda b, h, i: (b * nblk + i, kq + h)),
            pl.BlockSpec((tb, vw), lambda b, h, i: (b * nblk + i, v0 + h)),
            pl.BlockSpec((tb, vw), lambda b, h, i: (b * nblk + i, z0 + h)),
            pl.BlockSpec((GDN_CONV, qw), lambda b, h, i: (0, h)),
            pl.BlockSpec((GDN_CONV, qw), lambda b, h, i: (0, kq + h)),
            pl.BlockSpec((GDN_CONV, vw), lambda b, h, i: (0, v0 + h)),
            pl.BlockSpec((1, tb, LANES), lambda b, h, i: (h, b * nblk + i, 0)),
            pl.BlockSpec((1, hd), lambda b, h, i: (0, 0)),
        ],
        out_specs=pl.BlockSpec((tb, vw), lambda b, h, i: (b * nblk + i, h)),
        scratch_shapes=[
            pltpu.VMEM((tb + SUBLANES, qw), F32),
            pltpu.VMEM((tb + SUBLANES, qw), F32),
            pltpu.VMEM((tb + SUBLANES, vw), F32),
            pltpu.VMEM((tb, qw), F32),
            pltpu.VMEM((tb, qw), F32),
            pltpu.VMEM((tb, vw), F32),
            pltpu.VMEM((2 * n_qk, hd, hd), F32),
        ],
        compiler_params=_params("parallel", "parallel", "arbitrary"),
        name="gdn_delta",
    )(proj, proj, proj, proj, conv_w, conv_w, conv_w, gates, norm_w.reshape(1, hd))


def gated_deltanet_layer(h32, w_in, conv_w, a_log, dt_bias, norm_w, w_out, gain, bias, batch, seq):
    main = GDN_CONV_DIM + GDN_V_DIM
    gates, h16 = gdn_gates(h32, w_in[:, main:], a_log, dt_bias)
    proj = matmul(h16, w_in[:, :main].astype(BF16))
    o = gdn_delta(proj, conv_w, gates, norm_w, batch, seq)
    return proj_residual_ln(o, w_out.astype(BF16), h32, gain, bias)


DIL_STEP_TOKENS = {1: 512, 4: 512, 16: 2048}
DIL_STEP_HEADS = {1: 8, 4: 8, 16: 2}
DIL_CHAINS = 8


def _dil_attn_kernel(q_ref, kp_ref, ko_ref, vp_ref, vo_ref, bias_ref, o_ref, lse_ref, qs, ks, vs, os_, *, dil):
    blk = DIL_BLOCK
    span = q_ref.shape[0]
    hp = q_ref.shape[1] // HEAD_DIM
    halo = dil * blk
    m_blocks = span // halo
    first_span = pl.program_id(1) == 0
    hg = pl.program_id(2)
    for h in range(hp):
        s = slice(h * HEAD_DIM, (h + 1) * HEAD_DIM)
        qs[h] = q_ref[:, s].astype(F32)
        ks[h, 0:halo, :] = kp_ref[:, s].astype(F32)
        ks[h, halo:halo + span, :] = ko_ref[:, s].astype(F32)
        vs[h, 0:halo, :] = vp_ref[:, s].astype(F32)
        vs[h, halo:halo + span, :] = vo_ref[:, s].astype(F32)

    qi = lax.broadcasted_iota(jnp.int32, (blk, blk), 0)
    kj = lax.broadcasted_iota(jnp.int32, (blk, blk), 1)
    band_prev = qi <= kj
    mask_own = kj <= qi
    lane = lax.broadcasted_iota(jnp.int32, (blk, LANES), 1)
    scale = HEAD_DIM ** -0.5

    def rows(start):
        return pl.ds(start, blk, stride=dil) if dil > 1 else pl.ds(start, blk)

    @pl.when(hg == 0)
    def _():
        lse_ref[...] = jnp.zeros(lse_ref.shape, F32)

    combos = [(r, j, h) for r in range(dil) for j in range(m_blocks) for h in range(hp)]
    lse_tiles = {}
    for c0 in range(0, len(combos), DIL_CHAINS):
        group = combos[c0:c0 + DIL_CHAINS]
        q = [qs[h, rows(r + halo * j), :] for r, j, h in group]
        bias = [bias_ref[hg * hp + h] for _, _, h in group]
        mprev = [band_prev & jnp.logical_not(first_span) if j == 0 else band_prev for _, j, _ in group]
        s_prev = [jnp.where(mp, _dot_nt(qq, ks[h, rows(r + halo * j), :]) * scale + bb[:, :blk], -jnp.inf)
                  for (r, j, h), qq, bb, mp in zip(group, q, bias, mprev)]
        s_own = [jnp.where(mask_own, _dot_nt(qq, ks[h, rows(r + halo * (j + 1)), :]) * scale + bb[:, blk:], -jnp.inf)
                 for (r, j, h), qq, bb in zip(group, q, bias)]
        m = [jnp.maximum(jnp.max(sp, axis=-1, keepdims=True), jnp.max(so, axis=-1, keepdims=True))
             for sp, so in zip(s_prev, s_own)]
        p_prev = [jnp.exp(sp - mm) for sp, mm in zip(s_prev, m)]
        p_own = [jnp.exp(so - mm) for so, mm in zip(s_own, m)]
        den = [jnp.sum(pp, axis=-1, keepdims=True) + jnp.sum(po, axis=-1, keepdims=True)
               for pp, po in zip(p_prev, p_own)]
        o = [_dot_bf16(pp, vs[h, rows(r + halo * j), :]) + _dot_bf16(po, vs[h, rows(r + halo * (j + 1)), :])
             for (r, j, h), pp, po in zip(group, p_prev, p_own)]
        for (r, j, h), oo, dd, mm in zip(group, o, den, m):
            os_[h, rows(r + halo * j), :] = oo / dd
            prev_tile = lse_tiles[r, j] if (r, j) in lse_tiles else lse_ref[rows(r + halo * j), :]
            lse_tiles[r, j] = jnp.where(lane == hg * hp + h, mm + jnp.log(dd), prev_tile)
            if h == hp - 1:
                lse_ref[rows(r + halo * j), :] = lse_tiles.pop((r, j))
    for h in range(hp):
        o_ref[:, h * HEAD_DIM:(h + 1) * HEAD_DIM] = os_[h]


def dilated_group_attention(q, k, v, bias, dil, batch, seq):
    t, kvd = q.shape
    span = min(DIL_STEP_TOKENS[dil], seq)
    hp = DIL_STEP_HEADS[dil]
    halo = dil * DIL_BLOCK
    assert seq % span == 0 and span % halo == 0 and DIL_HEADS % hp == 0
    n_span = seq // span
    ratio = span // halo
    cols = hp * HEAD_DIM
    own = lambda b, i, g: (b * n_span + i, g)
    prev = lambda b, i, g: (jnp.maximum((b * n_span + i) * ratio - 1, 0), g)
    o, lse = pl.pallas_call(
        functools.partial(_dil_attn_kernel, dil=dil),
        out_shape=(jax.ShapeDtypeStruct((t, kvd), F32),
                   jax.ShapeDtypeStruct((t, LANES), F32)),
        grid=(batch, n_span, DIL_HEADS // hp),
        in_specs=[
            pl.BlockSpec((span, cols), own),
            pl.BlockSpec((halo, cols), prev),
            pl.BlockSpec((span, cols), own),
            pl.BlockSpec((halo, cols), prev),
            pl.BlockSpec((span, cols), own),
            pl.BlockSpec((DIL_HEADS, DIL_BLOCK, 2 * DIL_BLOCK), lambda b, i, g: (0, 0, 0)),
        ],
        out_specs=(pl.BlockSpec((span, cols), own), pl.BlockSpec((span, LANES), lambda b, i, g: (b * n_span + i, 0))),
        scratch_shapes=[pltpu.VMEM((hp, span, HEAD_DIM), F32), pltpu.VMEM((hp, halo + span, HEAD_DIM), F32),
                        pltpu.VMEM((hp, halo + span, HEAD_DIM), F32), pltpu.VMEM((hp, span, HEAD_DIM), F32)],
        compiler_params=_params("parallel", "parallel", "arbitrary"),
        name=f"dilated_attention_d{dil}",
    )(q, k, k, v, v, bias)
    return o, lse


def _dil_out_kernel(o0_ref, o1_ref, o2_ref, l0_ref, l1_ref, l2_ref, w_ref, h_ref, gain_ref, bias_ref,
                    o32_ref, o16_ref, orow_ref, mix_ref):
    l0, l1, l2 = l0_ref[...], l1_ref[...], l2_ref[...]
    m = jnp.maximum(jnp.maximum(l0, l1), l2)
    e0, e1, e2 = jnp.exp(l0 - m), jnp.exp(l1 - m), jnp.exp(l2 - m)
    inv = 1.0 / (e0 + e1 + e2)
    w0, w1, w2 = e0 * inv, e1 * inv, e2 * inv
    for hd in range(DIL_HEADS):
        sl = slice(hd * HEAD_DIM, (hd + 1) * HEAD_DIM)
        mix_ref[:, sl] = (w0[:, hd:hd + 1] * o0_ref[:, sl] + w1[:, hd:hd + 1] * o1_ref[:, sl]
                          + w2[:, hd:hd + 1] * o2_ref[:, sl])
    mix = jnp.dot(mix_ref[...].astype(BF16), w_ref[...], preferred_element_type=F32)
    y = _layer_norm_rows(DEEPNORM_ALPHA * h_ref[...] + mix, gain_ref[...], bias_ref[...])
    o32_ref[...] = y
    o16_ref[...] = y.astype(BF16)
    _store_row_contiguous(orow_ref, y)


def dilated_out(outs, lses, w_o, h, gain, bias, tm=LN_TM):
    m, kvd = outs[0].shape
    d = w_o.shape[1]
    tm = min(tm, m)
    row = lambda i: (i, 0)
    fixed = lambda i: (0, 0)
    return pl.pallas_call(
        _dil_out_kernel,
        out_shape=(jax.ShapeDtypeStruct((m, d), F32), jax.ShapeDtypeStruct((m, d), BF16),
                   jax.ShapeDtypeStruct((m * (d // LANES), LANES), F32)),
        grid=(m // tm,),
        in_specs=[pl.BlockSpec((tm, kvd), row)] * 3 + [pl.BlockSpec((tm, LANES), row)] * 3 + [
            pl.BlockSpec((kvd, d), fixed), pl.BlockSpec((tm, d), row),
            pl.BlockSpec((1, d), fixed), pl.BlockSpec((1, d), fixed)],
        out_specs=(pl.BlockSpec((tm, d), row), pl.BlockSpec((tm, d), row), pl.BlockSpec((tm * (d // LANES), LANES), row)),
        scratch_shapes=[pltpu.VMEM((tm, kvd), F32)],
        compiler_params=_params("parallel"),
        name="dilated_out_ln",
    )(*outs, *lses, w_o, h, gain.reshape(1, d), bias.reshape(1, d))


def _t5_bucket(dist):
    exact = REL_BUCKETS // 2
    distf = jnp.maximum(dist, exact).astype(F32)
    large = exact + (jnp.log(distf / exact) / math.log(REL_MAX_DIST / exact)
                     * (REL_BUCKETS - exact)).astype(jnp.int32)
    return jnp.where(dist < exact, dist, jnp.minimum(large, REL_BUCKETS - 1))


def _band_bias(rel_bias_g, dil):
    qi = jnp.arange(DIL_BLOCK)[:, None]
    kj = jnp.arange(2 * DIL_BLOCK)[None, :]
    steps = DIL_BLOCK + qi - kj
    bucket = _t5_bucket(jnp.maximum(steps, 0) * dil)
    onehot = (bucket[:, :, None] == jnp.arange(REL_BUCKETS)[None, None, :]).astype(F32)
    return jnp.einsum('qkb,bh->hqk', onehot, rel_bias_g.astype(F32), precision=lax.Precision.HIGHEST)


def _router_kernel(h_ref, w_ref, b_ref, o_ref, cnt_ref, running):
    @pl.when(pl.program_id(0) == 0)
    def _():
        running[...] = jnp.zeros(running.shape, F32)

    logits = _dot_split(h_ref[...], w_ref) + b_ref[...]
    lane = lax.broadcasted_iota(jnp.int32, logits.shape, 1)
    neg = -jnp.inf
    big = jnp.int32(LANES)
    is_group = lane < MOE_GROUPS
    gl = jnp.where(is_group, logits, neg)
    gmax = jnp.max(gl, axis=-1, keepdims=True)
    gidx = jnp.min(jnp.where(gl == gmax, lane, big), axis=-1, keepdims=True)
    gden = jnp.sum(jnp.exp(gl - gmax), axis=-1, keepdims=True)
    group_gate = 1.0 / gden
    lo = MOE_GROUPS + gidx * MOE_EXPERTS_PER_GROUP
    in_group = (lane >= lo) & (lane < lo + MOE_EXPERTS_PER_GROUP)
    ll = jnp.where(in_group, logits, neg)
    m1 = jnp.max(ll, axis=-1, keepdims=True)
    i1 = jnp.min(jnp.where(ll == m1, lane, big), axis=-1, keepdims=True)
    rest = jnp.where(lane == i1, neg, ll)
    m2 = jnp.max(rest, axis=-1, keepdims=True)
    i2 = jnp.min(jnp.where(rest == m2, lane, big), axis=-1, keepdims=True)
    den = jnp.sum(jnp.exp(ll - m1), axis=-1, keepdims=True)
    p1 = 1.0 / den
    p2 = jnp.exp(m2 - m1) / den
    g1 = group_gate * p1 / (p1 + p2)
    g2 = group_gate * p2 / (p1 + p2)
    id1 = (i1 - MOE_GROUPS).astype(F32)
    id2 = (i2 - MOE_GROUPS).astype(F32)
    tm = logits.shape[0]
    oh1 = (lane == i1).astype(F32)
    oh2 = (lane == i2).astype(F32)
    both = oh1 + oh2
    r_i = lax.broadcasted_iota(jnp.int32, (tm, tm), 0)
    c_i = lax.broadcasted_iota(jnp.int32, (tm, tm), 1)
    before = jnp.dot((c_i < r_i).astype(BF16), both.astype(BF16), preferred_element_type=F32) + running[...]
    rank1 = jnp.sum(before * oh1, axis=-1, keepdims=True)
    rank2 = jnp.sum(before * oh2, axis=-1, keepdims=True)
    running[...] = running[...] + jnp.sum(both, axis=0, keepdims=True)
    cnt_ref[...] = running[...]
    o_ref[...] = jnp.where(lane == 0, id1, jnp.where(lane == 1, id2, jnp.where(lane == 2, g1, jnp.where(
        lane == 3, g2, jnp.where(lane == 4, rank1, rank2)))))


def moe_router(h, w_group, b_group, w_expert, b_expert, tm=ROUTER_TM):
    m, d = h.shape
    tm = min(tm, m)
    pad = LANES - MOE_GROUPS - MOE_EXPERTS
    w = jnp.pad(jnp.concatenate([w_group, w_expert], axis=1), ((0, 0), (0, pad)))
    b = jnp.pad(jnp.concatenate([b_group, b_expert]), (0, pad)).reshape(1, LANES)
    out, cnt = pl.pallas_call(
        _router_kernel,
        out_shape=(jax.ShapeDtypeStruct((m, LANES), F32), jax.ShapeDtypeStruct((1, LANES), F32)),
        grid=(m // tm,),
        in_specs=[pl.BlockSpec((tm, d), lambda i: (i, 0)),
                  pl.BlockSpec((d, 2 * LANES), lambda i: (0, 0)),
                  pl.BlockSpec((1, LANES), lambda i: (0, 0))],
        out_specs=(pl.BlockSpec((tm, LANES), lambda i: (i, 0)), pl.BlockSpec((1, LANES), lambda i: (0, 0))),
        scratch_shapes=[pltpu.VMEM((1, LANES), F32)],
        compiler_params=_params("arbitrary"),
        name="moe_router",
    )(h, _hi_lo_columns(w), b)
    k = MOE_TOP_K
    counts = cnt[0, MOE_GROUPS:MOE_GROUPS + MOE_EXPERTS].astype(jnp.int32)
    return out[:, :k].astype(jnp.int32), out[:, k:2 * k], out[:, 2 * k:3 * k].astype(jnp.int32), counts


def _gather_rows_start(src_hbm, idx_ref, idx_base, buf, sem, unrolled, row_tiles=None):
    if row_tiles is None:
        groups = buf.shape[0]

        def copy(g, j):
            pltpu.make_async_copy(src_hbm.at[pl.ds(idx_ref[idx_base + g * SUBLANES + j], 1)],
                                  buf.at[g, pl.ds(j, 1)], sem).start()
    else:
        w = row_tiles
        groups = buf.shape[0] // (w * SUBLANES)

        def copy(g, j):
            r = g * SUBLANES + j
            pltpu.make_async_copy(src_hbm.at[pl.ds(pl.multiple_of(idx_ref[idx_base + r], w), w)],
                                  buf.at[pl.ds(pl.multiple_of(r * w, w), w)], sem).start()

    if unrolled:
        for g in range(groups):
            for j in range(SUBLANES):
                copy(g, j)
    else:
        def body(g, carry):
            for j in range(SUBLANES):
                copy(g, j)
            return carry
        lax.fori_loop(0, groups, body, 0)


def _gather_rows_wait(buf, sem):
    pltpu.make_async_copy(buf, buf, sem).wait()


def _gathered_blocks(src_hbm, idx_ref, bufs, sems, compute, row_tiles=None, n_active=None):
    i = pl.program_id(0)
    n = pl.num_programs(0) if n_active is None else n_active
    if row_tiles is None:
        rows = bufs[0].shape[0] * bufs[0].shape[1]
        load = lambda buf: buf[...].reshape(rows, buf.shape[2])
    else:
        rows = bufs[0].shape[0] // row_tiles
        load = lambda buf: _load_row_contiguous(buf, rows)

    @pl.when(i == 0)
    def _():
        _gather_rows_start(src_hbm, idx_ref, 0, bufs[0], sems.at[0], False, row_tiles)

    def step(s):
        _gather_rows_wait(bufs[s], sems.at[s])
        _gather_rows_start(src_hbm, idx_ref, jnp.minimum(i + 1, n - 1) * rows, bufs[1 - s], sems.at[1 - s],
                           True, row_tiles)
        compute(load(bufs[s]))

        @pl.when(i == n - 1)
        def _():
            _gather_rows_wait(bufs[1 - s], sems.at[1 - s])

    for s in range(2):
        pl.when((i % 2 == s) & (i < n))(functools.partial(step, s))


def _expert_kernel(bexp_ref, rtok_ref, nused_ref, h_hbm, w1_ref, w3_ref, w2_ref, y_ref,
                   xbuf_a, xbuf_b, sems, wb1, wb3, wb2):
    i = pl.program_id(0)

    @pl.when((i < nused_ref[0]) & ((i == 0) | (bexp_ref[i] != bexp_ref[jnp.maximum(i - 1, 0)])))
    def _():
        wb1[...] = w1_ref[0, 0].astype(BF16)
        wb3[...] = w3_ref[0, 0].astype(BF16)
        wb2[...] = w2_ref[0, 0].astype(BF16)

    def compute(x):
        x = x.astype(BF16)
        h1 = jnp.dot(x, wb1[...], preferred_element_type=F32)
        h3 = jnp.dot(x, wb3[...], preferred_element_type=F32)
        hmid = (h1 * jax.nn.sigmoid(h1) * h3).astype(BF16)
        y_ref[...] = jnp.dot(hmid, wb2[...], preferred_element_type=F32)

    _gathered_blocks(h_hbm, rtok_ref, (xbuf_a, xbuf_b), sems, compute, row_tiles=w1_ref.shape[2] // LANES,
                     n_active=nused_ref[0])

    @pl.when(i >= nused_ref[0])
    def _():
        y_ref[...] = jnp.zeros(y_ref.shape, y_ref.dtype)


def moe_experts(h_rows, row_token, block_expert, n_used, w1, w3, w2, layer, tm=MOE_TM):
    n_rows = row_token.shape[0]
    n_blocks = n_rows // tm
    d = w1.shape[2]
    row_tiles = d // LANES
    ff = w1.shape[3]
    grid_spec = pltpu.PrefetchScalarGridSpec(
        num_scalar_prefetch=3,
        grid=(n_blocks,),
        in_specs=[
            pl.BlockSpec(memory_space=pl.ANY),
            pl.BlockSpec((1, 1, d, ff), lambda i, be, rt, nu: (layer, be[i], 0, 0)),
            pl.BlockSpec((1, 1, d, ff), lambda i, be, rt, nu: (layer, be[i], 0, 0)),
            pl.BlockSpec((1, 1, ff, d), lambda i, be, rt, nu: (layer, be[i], 0, 0)),
        ],
        out_specs=pl.BlockSpec((tm, d), lambda i, be, rt, nu: (i, 0)),
        scratch_shapes=[pltpu.VMEM((tm * row_tiles, LANES), F32), pltpu.VMEM((tm * row_tiles, LANES), F32),
                        pltpu.SemaphoreType.DMA((2,)),
                        pltpu.VMEM((d, ff), BF16), pltpu.VMEM((d, ff), BF16), pltpu.VMEM((ff, d), BF16)],
    )
    return pl.pallas_call(
        _expert_kernel,
        out_shape=jax.ShapeDtypeStruct((n_rows, d), F32),
        grid_spec=grid_spec,
        compiler_params=_params("arbitrary"),
        name="moe_experts",
    )(block_expert, row_token, n_used, h_rows, w1, w3, w2)


def _combine_kernel(dest_ref, y_hbm, gate_ref, h_ref, gain_ref, bias_ref, o32_ref, o16_ref, ybuf_a, ybuf_b, sems):
    tm = h_ref.shape[0]

    def compute(yy):
        gates = gate_ref[...]
        ffn = yy[0:tm, :] * gates[:, 0:1] + yy[tm:, :] * gates[:, 1:2]
        y = _layer_norm_rows(DEEPNORM_ALPHA * h_ref[...] + ffn, gain_ref[...], bias_ref[...])
        o32_ref[...] = y
        o16_ref[...] = y.astype(BF16)

    _gathered_blocks(y_hbm, dest_ref, (ybuf_a, ybuf_b), sems, compute)


def moe_combine_ln(y_rows, dest, gates, h, gain, bias, tm=COMBINE_TM):
    m, d = h.shape
    tm = min(tm, m)
    grid_spec = pltpu.PrefetchScalarGridSpec(
        num_scalar_prefetch=1,
        grid=(m // tm,),
        in_specs=[
            pl.BlockSpec(memory_space=pl.ANY),
            pl.BlockSpec((tm, MOE_TOP_K), lambda i, de: (i, 0)),
            pl.BlockSpec((tm, d), lambda i, de: (i, 0)),
            pl.BlockSpec((1, d), lambda i, de: (0, 0)),
            pl.BlockSpec((1, d), lambda i, de: (0, 0)),
        ],
        out_specs=(pl.BlockSpec((tm, d), lambda i, de: (i, 0)),
                   pl.BlockSpec((tm, d), lambda i, de: (i, 0))),
        scratch_shapes=[pltpu.VMEM((MOE_TOP_K * tm // SUBLANES, SUBLANES, d), F32),
                        pltpu.VMEM((MOE_TOP_K * tm // SUBLANES, SUBLANES, d), F32), pltpu.SemaphoreType.DMA((2,))],
    )
    return pl.pallas_call(
        _combine_kernel,
        out_shape=(jax.ShapeDtypeStruct((m, d), F32), jax.ShapeDtypeStruct((m, d), BF16)),
        grid_spec=grid_spec,
        compiler_params=_params("arbitrary"),
        name="moe_combine_ln",
    )(dest.reshape(-1), y_rows, gates, h, gain.reshape(1, d), bias.reshape(1, d))


def moe_layer(h32, h_rows, w_group, b_group, w_expert, b_expert, w1, w3, w2, layer, gain, bias, tm=MOE_TM):
    t = h32.shape[0]
    ids, gates, ranks, counts = moe_router(h32, w_group, b_group, w_expert, b_expert)
    eid = ids.reshape(-1)
    n_pairs = eid.shape[0]
    padded = (counts + tm - 1) // tm * tm
    seg_end = jnp.cumsum(padded)
    seg_start = seg_end - padded
    onehot = eid[:, None] == jnp.arange(MOE_EXPERTS, dtype=jnp.int32)[None, :]
    dest = (jnp.sum(jnp.where(onehot, seg_start[None, :], 0), axis=1) + ranks.reshape(-1)).astype(jnp.int32)
    n_blocks = -(-n_pairs // tm) + MOE_EXPERTS
    n_rows = n_blocks * tm
    token_of_pair = (jnp.arange(n_pairs, dtype=jnp.int32) // MOE_TOP_K)
    filler = jnp.arange(n_rows, dtype=jnp.int32) % t
    row_token = filler.at[dest].set(token_of_pair) * (h32.shape[1] // LANES)
    block_start = jnp.arange(n_blocks, dtype=jnp.int32) * tm
    block_expert = jnp.minimum(jnp.sum((seg_end[None, :] <= block_start[:, None]).astype(jnp.int32), axis=1),
                               MOE_EXPERTS - 1)
    n_used = (seg_end[-1:] // tm).astype(jnp.int32)
    y_rows = moe_experts(h_rows, row_token, block_expert, n_used, w1, w3, w2, layer, tm=tm)
    ctm = min(COMBINE_TM, t)
    dest_tiles = jnp.transpose(dest.reshape(t // ctm, ctm, MOE_TOP_K), (0, 2, 1))
    return moe_combine_ln(y_rows, dest_tiles, gates, h32, gain, bias, tm=ctm)


def kernel(x, gdn_w_in, gdn_conv, gdn_a_log, gdn_dt_bias, gdn_norm, gdn_w_out, kv_w_k, kv_w_v, dil_w_q, dil_w_o,
           rel_bias, ln_gain, ln_bias, moe_w_group, moe_b_group, moe_w_expert, moe_b_expert, moe_w1, moe_w3, moe_w2):
    batch, seq, d = x.shape
    t = batch * seq
    h32 = x.reshape(t, d)
    h16 = None
    k = v = None
    biases = None
    for layer in range(DEPTH):
        if layer < N_A_LAYERS:
            h32, h16, h_rows = gated_deltanet_layer(
                h32, gdn_w_in[layer], gdn_conv[layer], gdn_a_log[layer], gdn_dt_bias[layer],
                gdn_norm[layer], gdn_w_out[layer], ln_gain[layer, 0], ln_bias[layer, 0], batch, seq)
        else:
            if layer == N_A_LAYERS:
                k = matmul(h16, kv_w_k.astype(BF16), out_dtype=BF16)
                v = matmul(h16, kv_w_v.astype(BF16), out_dtype=BF16)
                biases = [_band_bias(rel_bias[:, gi * DIL_HEADS:(gi + 1) * DIL_HEADS], dil)
                          for gi, (_, dil) in enumerate(DIL_CONFIGS)]
            j = layer - N_A_LAYERS
            outs, lses = [], []
            for gi, (_, dil) in enumerate(DIL_CONFIGS):
                q = matmul(h16, dil_w_q[j][:, gi * DIL_KV_DIM:(gi + 1) * DIL_KV_DIM].astype(BF16), out_dtype=BF16)
                o, lse = dilated_group_attention(q, k, v, biases[gi], dil, batch, seq)
                outs.append(o)
                lses.append(lse)
            h32, h16, h_rows = dilated_out(outs, lses, dil_w_o[j].astype(BF16), h32, ln_gain[layer, 0], ln_bias[layer, 0])
        h32, h16 = moe_layer(h32, h_rows, moe_w_group[layer], moe_b_group[layer], moe_w_expert[layer], moe_b_expert[layer],
                             moe_w1, moe_w3, moe_w2, layer, ln_gain[layer, 1], ln_bias[layer, 1])
    return h32.reshape(batch, seq, d)
```

```python
import functools
import math

import jax
import jax.numpy as jnp
from jax import lax
from jax.experimental import pallas as pl
from jax.experimental.pallas import tpu as pltpu

F32 = jnp.float32
BF16 = jnp.bfloat16

DEPTH = 4
N_A_LAYERS = DEPTH // 2
HEAD_DIM = 128
GDN_QK_HEADS = 16
GDN_V_HEADS = 32
GDN_CONV = 4
GDN_CHUNK = 64
GDN_QK_DIM = GDN_QK_HEADS * HEAD_DIM
GDN_V_DIM = GDN_V_HEADS * HEAD_DIM
GDN_CONV_DIM = 2 * GDN_QK_DIM + GDN_V_DIM
DIL_CONFIGS = ((128, 1), (512, 4), (2048, 16))
N_DIL = len(DIL_CONFIGS)
DIL_HEADS = 8
DIL_BLOCK = 128
DIL_KV_DIM = DIL_HEADS * HEAD_DIM
REL_BUCKETS = 32
REL_MAX_DIST = 2048
MOE_GROUPS = 4
MOE_EXPERTS_PER_GROUP = 8
MOE_EXPERTS = MOE_GROUPS * MOE_EXPERTS_PER_GROUP
MOE_TOP_K = 2
DEEPNORM_ALPHA = (2 * DEPTH) ** 0.25
LN_EPS = 1e-5
RMS_EPS = 1e-6
L2_EPS = 1e-6

LANES = 128
SUBLANES = 8
VMEM_LIMIT_BYTES = 56 * 1024 * 1024

MM_TM = 1024
MM_TN = 1024
LN_TM = 256
GATE_TM = 1024
GDN_TB = 512
GDN_QK_PER_STEP = 4
MOE_TM = 256
ROUTER_TM = 1024
COMBINE_TM = 256


def _params(*semantics):
    return pltpu.CompilerParams(dimension_semantics=semantics, vmem_limit_bytes=VMEM_LIMIT_BYTES)


def _split_bf16(x):
    hi = x.astype(BF16)
    return hi, (x - hi.astype(F32)).astype(BF16)


def _dot_split(x, w2_ref):
    n = w2_ref.shape[1] // 2
    hi, lo = _split_bf16(x)
    a = jnp.dot(hi, w2_ref[...], preferred_element_type=F32)
    b = jnp.dot(lo, w2_ref[:, :n], preferred_element_type=F32)
    return a[:, :n] + a[:, n:] + b


def _hi_lo_columns(w):
    hi = w.astype(BF16)
    return jnp.concatenate([hi, (w - hi.astype(F32)).astype(BF16)], axis=1)


def _store_row_contiguous(ref, y):
    rows, width = y.shape
    w = width // LANES
    for c in range(w):
        ref[pl.ds(c, rows, stride=w), :] = y[:, c * LANES:(c + 1) * LANES]


def _load_row_contiguous(ref, rows):
    w = ref.shape[0] // rows
    return jnp.concatenate([ref[pl.ds(c, rows, stride=w), :] for c in range(w)], axis=1)


def _layer_norm_rows(y, gain, bias):
    mu = jnp.mean(y, axis=-1, keepdims=True)
    yc = y - mu
    var = jnp.mean(yc * yc, axis=-1, keepdims=True)
    return yc * lax.rsqrt(var + LN_EPS) * gain + bias


def _matmul_kernel(a_ref, b_ref, o_ref):
    o_ref[...] = jnp.dot(a_ref[...].astype(BF16), b_ref[...],
                         preferred_element_type=F32).astype(o_ref.dtype)


def matmul(a, b, out_dtype=F32, tm=MM_TM, tn=MM_TN):
    m, k = a.shape
    n = b.shape[1]
    tm, tn = min(tm, m), min(tn, n)
    assert m % tm == 0 and n % tn == 0
    return pl.pallas_call(
        _matmul_kernel,
        out_shape=jax.ShapeDtypeStruct((m, n), out_dtype),
        grid=(n // tn, m // tm),
        in_specs=[pl.BlockSpec((tm, k), lambda j, i: (i, 0)),
                  pl.BlockSpec((k, tn), lambda j, i: (0, j))],
        out_specs=pl.BlockSpec((tm, tn), lambda j, i: (i, j)),
        compiler_params=_params("parallel", "parallel"),
        name="matmul",
    )(a, b)


def _proj_ln_kernel(a_ref, w_ref, h_ref, gain_ref, bias_ref, o32_ref, o16_ref, orow_ref):
    mix = jnp.dot(a_ref[...].astype(BF16), w_ref[...], preferred_element_type=F32)
    y = _layer_norm_rows(DEEPNORM_ALPHA * h_ref[...] + mix, gain_ref[...], bias_ref[...])
    o32_ref[...] = y
    o16_ref[...] = y.astype(BF16)
    _store_row_contiguous(orow_ref, y)


def proj_residual_ln(a, w, h, gain, bias, tm=LN_TM):
    m, k = a.shape
    d = w.shape[1]
    tm = min(tm, m)
    assert m % tm == 0
    return pl.pallas_call(
        _proj_ln_kernel,
        out_shape=(jax.ShapeDtypeStruct((m, d), F32), jax.ShapeDtypeStruct((m, d), BF16),
                   jax.ShapeDtypeStruct((m * (d // LANES), LANES), F32)),
        grid=(m // tm,),
        in_specs=[pl.BlockSpec((tm, k), lambda i: (i, 0)),
                  pl.BlockSpec((k, d), lambda i: (0, 0)),
                  pl.BlockSpec((tm, d), lambda i: (i, 0)),
                  pl.BlockSpec((1, d), lambda i: (0, 0)),
                  pl.BlockSpec((1, d), lambda i: (0, 0))],
        out_specs=(pl.BlockSpec((tm, d), lambda i: (i, 0)),
                   pl.BlockSpec((tm, d), lambda i: (i, 0)),
                   pl.BlockSpec((tm * (d // LANES), LANES), lambda i: (i, 0))),
        compiler_params=_params("parallel"),
        name="proj_residual_ln",
    )(a, w, h, gain.reshape(1, d), bias.reshape(1, d))


def _gdn_gate_kernel(h_ref, w_ref, alog_ref, dt_ref, o_ref, h16_ref):
    tm = h_ref.shape[0]
    n_groups = o_ref.shape[0]
    h16_ref[...] = h_ref[...].astype(BF16)
    width = 2 * GDN_V_HEADS // n_groups
    ba = _dot_split(h_ref[...], w_ref)
    xa = ba + dt_ref[...]
    softplus = jnp.maximum(xa, 0.0) + jnp.log(1.0 + jnp.exp(-jnp.abs(xa)))
    g = -jnp.exp(alog_ref[...]) * softplus
    row = lax.broadcasted_iota(jnp.int32, (tm, tm), 0)
    col = lax.broadcasted_iota(jnp.int32, (tm, tm), 1)
    tri = ((row // GDN_CHUNK == col // GDN_CHUNK) & (col <= row)).astype(BF16)
    g_hi, g_lo = _split_bf16(g)
    gc2 = jnp.dot(tri, jnp.concatenate([g_hi, g_lo], axis=1), preferred_element_type=F32)
    gc = gc2[:, :LANES] + gc2[:, LANES:]
    lane = lax.broadcasted_iota(jnp.int32, ba.shape, 1)
    slab = jnp.where(lane % width < width // 2, jax.nn.sigmoid(ba), gc)
    for r in range(n_groups):
        o_ref[r] = slab if r == 0 else pltpu.roll(slab, LANES - width * r, axis=1)


def gdn_gates(h, w_ba, a_log, dt_bias, n_qk=GDN_QK_PER_STEP, tm=GATE_TM):
    m, d = h.shape
    tm = min(tm, m)
    v = 2 * n_qk
    n_groups = GDN_V_HEADS // v
    cols = [(v * (l // (2 * v)) + l % (2 * v)) if l % (2 * v) < v else (GDN_V_HEADS + v * (l // (2 * v)) + l % (2 * v) - v)
            for l in range(2 * GDN_V_HEADS)]
    is_decay = jnp.array([l % (2 * v) >= v for l in range(2 * GDN_V_HEADS)])
    head = jnp.array([c % GDN_V_HEADS for c in cols], jnp.int32)
    pad = LANES - 2 * GDN_V_HEADS
    w = jnp.pad(w_ba[:, jnp.array(cols, jnp.int32)], ((0, 0), (0, pad)))
    alog = jnp.pad(jnp.where(is_decay, a_log[head], 0.0), (0, pad)).reshape(1, LANES)
    dt = jnp.pad(jnp.where(is_decay, dt_bias[head], 0.0), (0, pad)).reshape(1, LANES)
    return pl.pallas_call(
        _gdn_gate_kernel,
        out_shape=(jax.ShapeDtypeStruct((n_groups, m, LANES), F32), jax.ShapeDtypeStruct((m, d), BF16)),
        grid=(m // tm,),
        in_specs=[pl.BlockSpec((tm, d), lambda i: (i, 0)),
                  pl.BlockSpec((d, 2 * LANES), lambda i: (0, 0)),
                  pl.BlockSpec((1, LANES), lambda i: (0, 0)),
                  pl.BlockSpec((1, LANES), lambda i: (0, 0))],
        out_specs=(pl.BlockSpec((n_groups, tm, LANES), lambda i: (0, i, 0)), pl.BlockSpec((tm, d), lambda i: (i, 0))),
        compiler_params=_params("parallel"),
        name="gdn_gates",
    )(h, _hi_lo_columns(w), alog, dt)


def _dot_bf16(a, b):
    return jnp.dot(a.astype(BF16), b.astype(BF16), preferred_element_type=F32)


def _dot_nt(a, b):
    return lax.dot_general(a.astype(BF16), b.astype(BF16), (((1,), (1,)), ((), ())),
                           preferred_element_type=F32)


def _dot_tn(a, b):
    return lax.dot_general(a.astype(BF16), b.astype(BF16), (((0,), (0,)), ((), ())),
                           preferred_element_type=F32)


def _block_diag2(x0, x1):
    z = jnp.zeros_like(x0)
    return jnp.concatenate([jnp.concatenate([x0, z], axis=1), jnp.concatenate([z, x1], axis=1)], axis=0)


def _gdn_kernel(q_ref, k_ref, v_ref, z_ref, wq_ref, wk_ref, wv_ref, g_ref, nw_ref,
                o_ref, qx, kx, vx, qn, kn, vn, state):
    blk = pl.program_id(2)
    tb = q_ref.shape[0]
    n_chunks = tb // GDN_CHUNK
    n_qk = q_ref.shape[1] // HEAD_DIM
    c = GDN_CHUNK
    hd = HEAD_DIM
    tail = SUBLANES

    @pl.when(blk == 0)
    def _():
        qx[0:tail, :] = jnp.zeros((tail, qx.shape[1]), F32)
        kx[0:tail, :] = jnp.zeros((tail, kx.shape[1]), F32)
        vx[0:tail, :] = jnp.zeros((tail, vx.shape[1]), F32)
        state[...] = jnp.zeros(state.shape, F32)

    def conv_silu(x_ref, ext, w_ref):
        ext[tail:tail + tb, :] = x_ref[...].astype(F32)
        y = None
        for j in range(GDN_CONV):
            term = w_ref[j:j + 1, :] * ext[pl.ds(tail - (GDN_CONV - 1) + j, tb), :]
            y = term if y is None else y + term
        ext[0:tail, :] = ext[tb:tb + tail, :]
        return y * jax.nn.sigmoid(y)

    def l2n(x):
        heads = [x[:, g * hd:(g + 1) * hd] for g in range(n_qk)]
        return jnp.concatenate([xh * lax.rsqrt(jnp.sum(xh * xh, axis=-1, keepdims=True) + L2_EPS) for xh in heads], axis=1)

    qn[...] = l2n(conv_silu(q_ref, qx, wq_ref)) * (HEAD_DIM ** -0.5)
    kn[...] = l2n(conv_silu(k_ref, kx, wk_ref))
    vn[...] = conv_silu(v_ref, vx, wv_ref)

    row = lax.broadcasted_iota(jnp.int32, (c, 2 * c), 0)
    lane = lax.broadcasted_iota(jnp.int32, (c, 2 * c), 1)
    col = lane & (c - 1)
    head0 = lane < c
    causal = col <= row
    strict = col < row
    in_block = (row // 16) == (col // 16)
    eye = (row == col).astype(F32)
    norm_w = nw_ref[...]

    def mm(x, y):
        yb = jnp.concatenate([jnp.where(head0, y, 0.0), jnp.where(head0, 0.0, y)], axis=0)
        return _dot_bf16(x, yb)

    units = [(ci, g) for ci in range(n_chunks) for g in range(n_qk)]
    each = lambda f, *lists: [f(*xs) for xs in zip(*lists)]
    qcs = [qn[ci * c:(ci + 1) * c, g * hd:(g + 1) * hd] for ci, g in units]
    kcs = [kn[ci * c:(ci + 1) * c, g * hd:(g + 1) * hd] for ci, g in units]
    n_v = 2 * n_qk
    gates = g_ref[0]
    gates_t = gates.T
    betas = [[gates[ci * c:(ci + 1) * c, 2 * g + hh:2 * g + hh + 1] for hh in range(2)] for ci, g in units]
    gcs = [[gates[ci * c:(ci + 1) * c, n_v + 2 * g + hh:n_v + 2 * g + hh + 1] for hh in range(2)] for ci, g in units]

    def packed_row(ci, g):
        w0 = (ci // 2) * 2 * c
        r0, r1 = (gates_t[n_v + 2 * g + hh:n_v + 2 * g + hh + 1, w0:w0 + 2 * c] for hh in range(2))
        if ci % 2 == 0:
            return jnp.where(head0[0:1], r0, pltpu.roll(r1, c, axis=1))
        return jnp.where(head0[0:1], pltpu.roll(r0, c, axis=1), r1)

    grps = [packed_row(ci, g) for ci, g in units]
    egs = [[jnp.exp(g[hh]) for hh in range(2)] for g in gcs]
    glast = [[g[hh][c - 1:c, :] for hh in range(2)] for g in gcs]
    both = each(lambda q, k: _dot_nt(jnp.concatenate([q, k], axis=0), jnp.concatenate([k, k], axis=0)), qcs, kcs)
    decayp = each(lambda g, gr: jnp.exp(jnp.where(causal, jnp.where(head0, g[0], g[1]) - gr, -jnp.inf)), gcs, grps)
    ap = each(lambda b, bo, dc: jnp.where(strict, jnp.where(head0, b[0], b[1]) * bo[c:] * dc, 0.0), betas, both, decayp)
    qkd = each(lambda bo, dc: bo[:c] * dc, both, decayp)
    d = each(lambda a: jnp.where(in_block, a, 0.0), ap)
    e = each(lambda a, dd: a - dd, ap, d)
    n1 = each(lambda dd: -dd, d)
    p = each(lambda n: eye + n, n1)
    n2 = each(mm, n1, n1)
    p = each(lambda pp, n: pp + mm(pp, n), p, n2)
    n4 = each(mm, n2, n2)
    p = each(lambda pp, n: pp + mm(pp, n), p, n4)
    n8 = each(mm, n4, n4)
    dinv = each(lambda pp, n: pp + mm(pp, n), p, n8)
    nb = each(mm, dinv, e)
    nb2 = each(mm, nb, nb)
    qq = each(lambda n: eye - n, nb)
    qq = each(lambda x, n: x + mm(x, n), qq, nb2)
    tinvp = each(mm, qq, dinv)
    rhs = [[jnp.concatenate([vn[ci * c:(ci + 1) * c, (2 * g + hh) * hd:(2 * g + hh + 1) * hd] * betas[u][hh],
                             kcs[u] * (betas[u][hh] * egs[u][hh])], axis=1) for hh in range(2)]
           for u, (ci, g) in enumerate(units)]
    uw = each(lambda t, r: _dot_bf16(t, _block_diag2(r[0], r[1])), tinvp, rhs)
    wu = [[jnp.concatenate([x[:, (2 * hh + 1) * hd:(2 * hh + 2) * hd], x[:, 2 * hh * hd:(2 * hh + 1) * hd]], axis=1)
           for hh in range(2)] for x in uw]
    qo = each(lambda x, w2: _dot_bf16(x, _block_diag2(w2[0], w2[1])), qkd, wu)
    pre = {}
    for u, (ci, g) in enumerate(units):
        for hh in range(2):
            rb = _dot_tn(kcs[u] * jnp.exp(glast[u][hh] - gcs[u][hh]), wu[u][hh])
            q_eff = qcs[u] * egs[u][hh] - qo[u][:, 2 * hh * hd:(2 * hh + 1) * hd]
            rq = jnp.concatenate([rb[:, :hd], q_eff], axis=0).astype(BF16)
            pre[ci, 2 * g + hh] = (rq, rb[:, hd:], qo[u][:, (2 * hh + 1) * hd:(2 * hh + 2) * hd], jnp.exp(glast[u][hh]))

    states = [state[vh] for vh in range(n_v)]
    for ci in range(n_chunks):
        r0 = ci * c
        for vh in range(n_v):
            rq, b_c, o_c, cd = pre[ci, vh]
            s = states[vh]
            res = jnp.dot(rq, s.astype(BF16), preferred_element_type=F32)
            states[vh] = s * cd - res[:hd] + b_c
            out = res[hd:] + o_c
            out = out * lax.rsqrt(jnp.mean(out * out, axis=-1, keepdims=True) + RMS_EPS) * norm_w
            zc = z_ref[r0:r0 + c, vh * hd:(vh + 1) * hd].astype(F32)
            out = out * (zc * jax.nn.sigmoid(zc))
            o_ref[r0:r0 + c, vh * hd:(vh + 1) * hd] = out.astype(o_ref.dtype)
    for vh in range(n_v):
        state[vh] = states[vh]


def gdn_delta(proj, conv_w, gates, norm_w, batch, seq, tb=GDN_TB, n_qk=GDN_QK_PER_STEP):
    t = proj.shape[0]
    tb = min(tb, seq)
    nblk = seq // tb
    hd = HEAD_DIM
    qw = n_qk * hd
    vw = 2 * qw
    kq = GDN_QK_DIM // qw
    v0 = (2 * GDN_QK_DIM) // vw
    z0 = GDN_CONV_DIM // vw
    return pl.pallas_call(
        _gdn_kernel,
        out_shape=jax.ShapeDtypeStruct((t, GDN_V_DIM), BF16),
        grid=(batch, GDN_QK_HEADS // n_qk, nblk),
        in_specs=[
            pl.BlockSpec((tb, qw), lambda b, h, i: (b * nblk + i, h)),
            pl.BlockSpec((tb, qw), lambda b, h, i: (b * nblk + i, kq + h)),
            pl.BlockSpec((tb, vw), lambda b, h, i: (b * nblk + i, v0 + h)),
            pl.BlockSpec((tb, vw), lambda b, h, i: (b * nblk + i, z0 + h)),
            pl.BlockSpec((GDN_CONV, qw), lambda b, h, i: (0, h)),
            pl.BlockSpec((GDN_CONV, qw), lambda b, h, i: (0, kq + h)),
            pl.BlockSpec((GDN_CONV, vw), lambda b, h, i: (0, v0 + h)),
            pl.BlockSpec((1, tb, LANES), lambda b, h, i: (h, b * nblk + i, 0)),
            pl.BlockSpec((1, hd), lambda b, h, i: (0, 0)),
        ],
        out_specs=pl.BlockSpec((tb, vw), lambda b, h, i: (b * nblk + i, h)),
        scratch_shapes=[
            pltpu.VMEM((tb + SUBLANES, qw), F32),
            pltpu.VMEM((tb + SUBLANES, qw), F32),
            pltpu.VMEM((tb + SUBLANES, vw), F32),
            pltpu.VMEM((tb, qw), F32),
            pltpu.VMEM((tb, qw), F32),
            pltpu.VMEM((tb, vw), F32),
            pltpu.VMEM((2 * n_qk, hd, hd), F32),
        ],
        compiler_params=_params("parallel", "parallel", "arbitrary"),
        name="gdn_delta",
    )(proj, proj, proj, proj, conv_w, conv_w, conv_w, gates, norm_w.reshape(1, hd))


def gated_deltanet_layer(h32, w_in, conv_w, a_log, dt_bias, norm_w, w_out, gain, bias, batch, seq):
    main = GDN_CONV_DIM + GDN_V_DIM
    gates, h16 = gdn_gates(h32, w_in[:, main:], a_log, dt_bias)
    proj = matmul(h16, w_in[:, :main].astype(BF16))
    o = gdn_delta(proj, conv_w, gates, norm_w, batch, seq)
    return proj_residual_ln(o, w_out.astype(BF16), h32, gain, bias)


DIL_STEP_TOKENS = {1: 1024, 4: 1024, 16: 4096}
DIL_STEP_HEADS = {1: 8, 4: 8, 16: 2}
DIL_CHAINS = 8


def _dil_attn_kernel(q_ref, kp_ref, ko_ref, vp_ref, vo_ref, bias_ref, o_ref, lse_ref, qs, ks, vs, os_, *, dil):
    blk = DIL_BLOCK
    span = q_ref.shape[0]
    hp = q_ref.shape[1] // HEAD_DIM
    halo = dil * blk
    m_blocks = span // halo
    first_span = pl.program_id(1) == 0
    hg = pl.program_id(2)
    for h in range(hp):
        s = slice(h * HEAD_DIM, (h + 1) * HEAD_DIM)
        qs[h] = q_ref[:, s].astype(F32)
        ks[h, 0:halo, :] = kp_ref[:, s].astype(F32)
        ks[h, halo:halo + span, :] = ko_ref[:, s].astype(F32)
        vs[h, 0:halo, :] = vp_ref[:, s].astype(F32)
        vs[h, halo:halo + span, :] = vo_ref[:, s].astype(F32)

    qi = lax.broadcasted_iota(jnp.int32, (blk, blk), 0)
    kj = lax.broadcasted_iota(jnp.int32, (blk, blk), 1)
    band_prev = qi <= kj
    mask_own = kj <= qi
    lane = lax.broadcasted_iota(jnp.int32, (blk, LANES), 1)
    scale = HEAD_DIM ** -0.5

    def rows(start):
        return pl.ds(start, blk, stride=dil) if dil > 1 else pl.ds(start, blk)

    @pl.when(hg == 0)
    def _():
        lse_ref[...] = jnp.zeros(lse_ref.shape, F32)

    combos = [(r, j, h) for r in range(dil) for j in range(m_blocks) for h in range(hp)]
    lse_tiles = {}
    for c0 in range(0, len(combos), DIL_CHAINS):
        group = combos[c0:c0 + DIL_CHAINS]
        q = [qs[h, rows(r + halo * j), :] for r, j, h in group]
        bias = [bias_ref[hg * hp + h] for _, _, h in group]
        mprev = [band_prev & jnp.logical_not(first_span) if j == 0 else band_prev for _, j, _ in group]
        s_prev = [jnp.where(mp, _dot_nt(qq, ks[h, rows(r + halo * j), :]) * scale + bb[:, :blk], -jnp.inf)
                  for (r, j, h), qq, bb, mp in zip(group, q, bias, mprev)]
        s_own = [jnp.where(mask_own, _dot_nt(qq, ks[h, rows(r + halo * (j + 1)), :]) * scale + bb[:, blk:], -jnp.inf)
                 for (r, j, h), qq, bb in zip(group, q, bias)]
        m = [jnp.maximum(jnp.max(sp, axis=-1, keepdims=True), jnp.max(so, axis=-1, keepdims=True))
             for sp, so in zip(s_prev, s_own)]
        p_prev = [jnp.exp(sp - mm) for sp, mm in zip(s_prev, m)]
        p_own = [jnp.exp(so - mm) for so, mm in zip(s_own, m)]
        den = [jnp.sum(pp, axis=-1, keepdims=True) + jnp.sum(po, axis=-1, keepdims=True)
               for pp, po in zip(p_prev, p_own)]
        o = [_dot_bf16(pp, vs[h, rows(r + halo * j), :]) + _dot_bf16(po, vs[h, rows(r + halo * (j + 1)), :])
             for (r, j, h), pp, po in zip(group, p_prev, p_own)]
        for (r, j, h), oo, dd, mm in zip(group, o, den, m):
            os_[h, rows(r + halo * j), :] = oo / dd
            prev_tile = lse_tiles[r, j] if (r, j) in lse_tiles else lse_ref[rows(r + halo * j), :]
            lse_tiles[r, j] = jnp.where(lane == hg * hp + h, mm + jnp.log(dd), prev_tile)
            if h == hp - 1:
                lse_ref[rows(r + halo * j), :] = lse_tiles.pop((r, j))
    for h in range(hp):
        o_ref[:, h * HEAD_DIM:(h + 1) * HEAD_DIM] = os_[h]


def dilated_group_attention(q, k, v, bias, dil, batch, seq):
    t, kvd = q.shape
    span = min(DIL_STEP_TOKENS[dil], seq)
    hp = DIL_STEP_HEADS[dil]
    halo = dil * DIL_BLOCK
    assert seq % span == 0 and span % halo == 0 and DIL_HEADS % hp == 0
    n_span = seq // span
    ratio = span // halo
    cols = hp * HEAD_DIM
    own = lambda b, i, g: (b * n_span + i, g)
    prev = lambda b, i, g: (jnp.maximum((b * n_span + i) * ratio - 1, 0), g)
    o, lse = pl.pallas_call(
        functools.partial(_dil_attn_kernel, dil=dil),
        out_shape=(jax.ShapeDtypeStruct((t, kvd), F32),
                   jax.ShapeDtypeStruct((t, LANES), F32)),
        grid=(batch, n_span, DIL_HEADS // hp),
        in_specs=[
            pl.BlockSpec((span, cols), own),
            pl.BlockSpec((halo, cols), prev),
            pl.BlockSpec((span, cols), own),
            pl.BlockSpec((halo, cols), prev),
            pl.BlockSpec((span, cols), own),
            pl.BlockSpec((DIL_HEADS, DIL_BLOCK, 2 * DIL_BLOCK), lambda b, i, g: (0, 0, 0)),
        ],
        out_specs=(pl.BlockSpec((span, cols), own), pl.BlockSpec((span, LANES), lambda b, i, g: (b * n_span + i, 0))),
        scratch_shapes=[pltpu.VMEM((hp, span, HEAD_DIM), F32), pltpu.VMEM((hp, halo + span, HEAD_DIM), F32),
                        pltpu.VMEM((hp, halo + span, HEAD_DIM), F32), pltpu.VMEM((hp, span, HEAD_DIM), F32)],
        compiler_params=_params("parallel", "parallel", "arbitrary"),
        name=f"dilated_attention_d{dil}",
    )(q, k, k, v, v, bias)
    return o, lse


def _dil_out_kernel(o0_ref, o1_ref, o2_ref, l0_ref, l1_ref, l2_ref, w_ref, h_ref, gain_ref, bias_ref,
                    o32_ref, o16_ref, orow_ref, mix_ref):
    l0, l1, l2 = l0_ref[...], l1_ref[...], l2_ref[...]
    m = jnp.maximum(jnp.maximum(l0, l1), l2)
    e0, e1, e2 = jnp.exp(l0 - m), jnp.exp(l1 - m), jnp.exp(l2 - m)
    inv = 1.0 / (e0 + e1 + e2)
    w0, w1, w2 = e0 * inv, e1 * inv, e2 * inv
    for hd in range(DIL_HEADS):
        sl = slice(hd * HEAD_DIM, (hd + 1) * HEAD_DIM)
        mix_ref[:, sl] = (w0[:, hd:hd + 1] * o0_ref[:, sl] + w1[:, hd:hd + 1] * o1_ref[:, sl]
                          + w2[:, hd:hd + 1] * o2_ref[:, sl])
    mix = jnp.dot(mix_ref[...].astype(BF16), w_ref[...], preferred_element_type=F32)
    y = _layer_norm_rows(DEEPNORM_ALPHA * h_ref[...] + mix, gain_ref[...], bias_ref[...])
    o32_ref[...] = y
    o16_ref[...] = y.astype(BF16)
    _store_row_contiguous(orow_ref, y)


def dilated_out(outs, lses, w_o, h, gain, bias, tm=LN_TM):
    m, kvd = outs[0].shape
    d = w_o.shape[1]
    tm = min(tm, m)
    row = lambda i: (i, 0)
    fixed = lambda i: (0, 0)
    return pl.pallas_call(
        _dil_out_kernel,
        out_shape=(jax.ShapeDtypeStruct((m, d), F32), jax.ShapeDtypeStruct((m, d), BF16),
                   jax.ShapeDtypeStruct((m * (d // LANES), LANES), F32)),
        grid=(m // tm,),
        in_specs=[pl.BlockSpec((tm, kvd), row)] * 3 + [pl.BlockSpec((tm, LANES), row)] * 3 + [
            pl.BlockSpec((kvd, d), fixed), pl.BlockSpec((tm, d), row),
            pl.BlockSpec((1, d), fixed), pl.BlockSpec((1, d), fixed)],
        out_specs=(pl.BlockSpec((tm, d), row), pl.BlockSpec((tm, d), row), pl.BlockSpec((tm * (d // LANES), LANES), row)),
        scratch_shapes=[pltpu.VMEM((tm, kvd), F32)],
        compiler_params=_params("parallel"),
        name="dilated_out_ln",
    )(*outs, *lses, w_o, h, gain.reshape(1, d), bias.reshape(1, d))


def _t5_bucket(dist):
    exact = REL_BUCKETS // 2
    distf = jnp.maximum(dist, exact).astype(F32)
    large = exact + (jnp.log(distf / exact) / math.log(REL_MAX_DIST / exact)
                     * (REL_BUCKETS - exact)).astype(jnp.int32)
    return jnp.where(dist < exact, dist, jnp.minimum(large, REL_BUCKETS - 1))


def _band_bias(rel_bias_g, dil):
    qi = jnp.arange(DIL_BLOCK)[:, None]
    kj = jnp.arange(2 * DIL_BLOCK)[None, :]
    steps = DIL_BLOCK + qi - kj
    bucket = _t5_bucket(jnp.maximum(steps, 0) * dil)
    onehot = (bucket[:, :, None] == jnp.arange(REL_BUCKETS)[None, None, :]).astype(F32)
    return jnp.einsum('qkb,bh->hqk', onehot, rel_bias_g.astype(F32), precision=lax.Precision.HIGHEST)


def _router_kernel(h_ref, w_ref, b_ref, o_ref, cnt_ref, running):
    @pl.when(pl.program_id(0) == 0)
    def _():
        running[...] = jnp.zeros(running.shape, F32)

    logits = _dot_split(h_ref[...], w_ref) + b_ref[...]
    lane = lax.broadcasted_iota(jnp.int32, logits.shape, 1)
    neg = -jnp.inf
    big = jnp.int32(LANES)
    is_group = lane < MOE_GROUPS
    gl = jnp.where(is_group, logits, neg)
    gmax = jnp.max(gl, axis=-1, keepdims=True)
    gidx = jnp.min(jnp.where(gl == gmax, lane, big), axis=-1, keepdims=True)
    gden = jnp.sum(jnp.exp(gl - gmax), axis=-1, keepdims=True)
    group_gate = 1.0 / gden
    lo = MOE_GROUPS + gidx * MOE_EXPERTS_PER_GROUP
    in_group = (lane >= lo) & (lane < lo + MOE_EXPERTS_PER_GROUP)
    ll = jnp.where(in_group, logits, neg)
    m1 = jnp.max(ll, axis=-1, keepdims=True)
    i1 = jnp.min(jnp.where(ll == m1, lane, big), axis=-1, keepdims=True)
    rest = jnp.where(lane == i1, neg, ll)
    m2 = jnp.max(rest, axis=-1, keepdims=True)
    i2 = jnp.min(jnp.where(rest == m2, lane, big), axis=-1, keepdims=True)
    den = jnp.sum(jnp.exp(ll - m1), axis=-1, keepdims=True)
    p1 = 1.0 / den
    p2 = jnp.exp(m2 - m1) / den
    g1 = group_gate * p1 / (p1 + p2)
    g2 = group_gate * p2 / (p1 + p2)
    id1 = (i1 - MOE_GROUPS).astype(F32)
    id2 = (i2 - MOE_GROUPS).astype(F32)
    tm = logits.shape[0]
    oh1 = (lane == i1).astype(F32)
    oh2 = (lane == i2).astype(F32)
    both = oh1 + oh2
    r_i = lax.broadcasted_iota(jnp.int32, (tm, tm), 0)
    c_i = lax.broadcasted_iota(jnp.int32, (tm, tm), 1)
    before = jnp.dot((c_i < r_i).astype(BF16), both.astype(BF16), preferred_element_type=F32) + running[...]
    rank1 = jnp.sum(before * oh1, axis=-1, keepdims=True)
    rank2 = jnp.sum(before * oh2, axis=-1, keepdims=True)
    running[...] = running[...] + jnp.sum(both, axis=0, keepdims=True)
    cnt_ref[...] = running[...]
    o_ref[...] = jnp.where(lane == 0, id1, jnp.where(lane == 1, id2, jnp.where(lane == 2, g1, jnp.where(
        lane == 3, g2, jnp.where(lane == 4, rank1, rank2)))))


def moe_router(h, w_group, b_group, w_expert, b_expert, tm=ROUTER_TM):
    m, d = h.shape
    tm = min(tm, m)
    pad = LANES - MOE_GROUPS - MOE_EXPERTS
    w = jnp.pad(jnp.concatenate([w_group, w_expert], axis=1), ((0, 0), (0, pad)))
    b = jnp.pad(jnp.concatenate([b_group, b_expert]), (0, pad)).reshape(1, LANES)
    out, cnt = pl.pallas_call(
        _router_kernel,
        out_shape=(jax.ShapeDtypeStruct((m, LANES), F32), jax.ShapeDtypeStruct((1, LANES), F32)),
        grid=(m // tm,),
        in_specs=[pl.BlockSpec((tm, d), lambda i: (i, 0)),
                  pl.BlockSpec((d, 2 * LANES), lambda i: (0, 0)),
                  pl.BlockSpec((1, LANES), lambda i: (0, 0))],
        out_specs=(pl.BlockSpec((tm, LANES), lambda i: (i, 0)), pl.BlockSpec((1, LANES), lambda i: (0, 0))),
        scratch_shapes=[pltpu.VMEM((1, LANES), F32)],
        compiler_params=_params("arbitrary"),
        name="moe_router",
    )(h, _hi_lo_columns(w), b)
    k = MOE_TOP_K
    counts = cnt[0, MOE_GROUPS:MOE_GROUPS + MOE_EXPERTS].astype(jnp.int32)
    return out[:, :k].astype(jnp.int32), out[:, k:2 * k], out[:, 2 * k:3 * k].astype(jnp.int32), counts


def _gather_rows_start(src_hbm, idx_ref, idx_base, buf, sem, unrolled, row_tiles=None):
    if row_tiles is None:
        groups = buf.shape[0]

        def copy(g, j):
            pltpu.make_async_copy(src_hbm.at[pl.ds(idx_ref[idx_base + g * SUBLANES + j], 1)],
                                  buf.at[g, pl.ds(j, 1)], sem).start()
    else:
        w = row_tiles
        groups = buf.shape[0] // (w * SUBLANES)

        def copy(g, j):
            r = g * SUBLANES + j
            pltpu.make_async_copy(src_hbm.at[pl.ds(pl.multiple_of(idx_ref[idx_base + r], w), w)],
                                  buf.at[pl.ds(pl.multiple_of(r * w, w), w)], sem).start()

    if unrolled:
        for g in range(groups):
            for j in range(SUBLANES):
                copy(g, j)
    else:
        def body(g, carry):
            for j in range(SUBLANES):
                copy(g, j)
            return carry
        lax.fori_loop(0, groups, body, 0)


def _gather_rows_wait(buf, sem):
    pltpu.make_async_copy(buf, buf, sem).wait()


def _gathered_blocks(src_hbm, idx_ref, bufs, sems, compute, row_tiles=None, n_active=None):
    i = pl.program_id(0)
    n = pl.num_programs(0) if n_active is None else n_active
    if row_tiles is None:
        rows = bufs[0].shape[0] * bufs[0].shape[1]
        load = lambda buf: buf[...].reshape(rows, buf.shape[2])
    else:
        rows = bufs[0].shape[0] // row_tiles
        load = lambda buf: _load_row_contiguous(buf, rows)

    @pl.when(i == 0)
    def _():
        _gather_rows_start(src_hbm, idx_ref, 0, bufs[0], sems.at[0], False, row_tiles)

    def step(s):
        _gather_rows_wait(bufs[s], sems.at[s])
        _gather_rows_start(src_hbm, idx_ref, jnp.minimum(i + 1, n - 1) * rows, bufs[1 - s], sems.at[1 - s],
                           True, row_tiles)
        compute(load(bufs[s]))

        @pl.when(i == n - 1)
        def _():
            _gather_rows_wait(bufs[1 - s], sems.at[1 - s])

    for s in range(2):
        pl.when((i % 2 == s) & (i < n))(functools.partial(step, s))


def _expert_kernel(bexp_ref, rtok_ref, nused_ref, h_hbm, w1_ref, w3_ref, w2_ref, y_ref,
                   xbuf_a, xbuf_b, sems, wb1, wb3, wb2):
    i = pl.program_id(0)

    @pl.when((i < nused_ref[0]) & ((i == 0) | (bexp_ref[i] != bexp_ref[jnp.maximum(i - 1, 0)])))
    def _():
        wb1[...] = w1_ref[0, 0].astype(BF16)
        wb3[...] = w3_ref[0, 0].astype(BF16)
        wb2[...] = w2_ref[0, 0].astype(BF16)

    def compute(x):
        x = x.astype(BF16)
        h1 = jnp.dot(x, wb1[...], preferred_element_type=F32)
        h3 = jnp.dot(x, wb3[...], preferred_element_type=F32)
        hmid = (h1 * jax.nn.sigmoid(h1) * h3).astype(BF16)
        y_ref[...] = jnp.dot(hmid, wb2[...], preferred_element_type=F32)

    _gathered_blocks(h_hbm, rtok_ref, (xbuf_a, xbuf_b), sems, compute, row_tiles=w1_ref.shape[2] // LANES,
                     n_active=nused_ref[0])

    @pl.when(i >= nused_ref[0])
    def _():
        y_ref[...] = jnp.zeros(y_ref.shape, y_ref.dtype)


def moe_experts(h_rows, row_token, block_expert, n_used, w1, w3, w2, layer, tm=MOE_TM):
    n_rows = row_token.shape[0]
    n_blocks = n_rows // tm
    d = w1.shape[2]
    row_tiles = d // LANES
    ff = w1.shape[3]
    grid_spec = pltpu.PrefetchScalarGridSpec(
        num_scalar_prefetch=3,
        grid=(n_blocks,),
        in_specs=[
            pl.BlockSpec(memory_space=pl.ANY),
            pl.BlockSpec((1, 1, d, ff), lambda i, be, rt, nu: (layer, be[i], 0, 0)),
            pl.BlockSpec((1, 1, d, ff), lambda i, be, rt, nu: (layer, be[i], 0, 0)),
            pl.BlockSpec((1, 1, ff, d), lambda i, be, rt, nu: (layer, be[i], 0, 0)),
        ],
        out_specs=pl.BlockSpec((tm, d), lambda i, be, rt, nu: (i, 0)),
        scratch_shapes=[pltpu.VMEM((tm * row_tiles, LANES), F32), pltpu.VMEM((tm * row_tiles, LANES), F32),
                        pltpu.SemaphoreType.DMA((2,)),
                        pltpu.VMEM((d, ff), BF16), pltpu.VMEM((d, ff), BF16), pltpu.VMEM((ff, d), BF16)],
    )
    return pl.pallas_call(
        _expert_kernel,
        out_shape=jax.ShapeDtypeStruct((n_rows, d), F32),
        grid_spec=grid_spec,
        compiler_params=_params("arbitrary"),
        name="moe_experts",
    )(block_expert, row_token, n_used, h_rows, w1, w3, w2)


def _combine_kernel(dest_ref, y_hbm, gate_ref, h_ref, gain_ref, bias_ref, o32_ref, o16_ref, ybuf_a, ybuf_b, sems):
    tm = h_ref.shape[0]

    def compute(yy):
        gates = gate_ref[...]
        ffn = yy[0:tm, :] * gates[:, 0:1] + yy[tm:, :] * gates[:, 1:2]
        y = _layer_norm_rows(DEEPNORM_ALPHA * h_ref[...] + ffn, gain_ref[...], bias_ref[...])
        o32_ref[...] = y
        o16_ref[...] = y.astype(BF16)

    _gathered_blocks(y_hbm, dest_ref, (ybuf_a, ybuf_b), sems, compute)


def moe_combine_ln(y_rows, dest, gates, h, gain, bias, tm=COMBINE_TM):
    m, d = h.shape
    tm = min(tm, m)
    grid_spec = pltpu.PrefetchScalarGridSpec(
        num_scalar_prefetch=1,
        grid=(m // tm,),
        in_specs=[
            pl.BlockSpec(memory_space=pl.ANY),
            pl.BlockSpec((tm, MOE_TOP_K), lambda i, de: (i, 0)),
            pl.BlockSpec((tm, d), lambda i, de: (i, 0)),
            pl.BlockSpec((1, d), lambda i, de: (0, 0)),
            pl.BlockSpec((1, d), lambda i, de: (0, 0)),
        ],
        out_specs=(pl.BlockSpec((tm, d), lambda i, de: (i, 0)),
                   pl.BlockSpec((tm, d), lambda i, de: (i, 0))),
        scratch_shapes=[pltpu.VMEM((MOE_TOP_K * tm // SUBLANES, SUBLANES, d), F32),
                        pltpu.VMEM((MOE_TOP_K * tm // SUBLANES, SUBLANES, d), F32), pltpu.SemaphoreType.DMA((2,))],
    )
    return pl.pallas_call(
        _combine_kernel,
        out_shape=(jax.ShapeDtypeStruct((m, d), F32), jax.ShapeDtypeStruct((m, d), BF16)),
        grid_spec=grid_spec,
        compiler_params=_params("arbitrary"),
        name="moe_combine_ln",
    )(dest.reshape(-1), y_rows, gates, h, gain.reshape(1, d), bias.reshape(1, d))


def moe_layer(h32, h_rows, w_group, b_group, w_expert, b_expert, w1, w3, w2, layer, gain, bias, tm=MOE_TM):
    t = h32.shape[0]
    ids, gates, ranks, counts = moe_router(h32, w_group, b_group, w_expert, b_expert)
    eid = ids.reshape(-1)
    n_pairs = eid.shape[0]
    padded = (counts + tm - 1) // tm * tm
    seg_end = jnp.cumsum(padded)
    seg_start = seg_end - padded
    onehot = eid[:, None] == jnp.arange(MOE_EXPERTS, dtype=jnp.int32)[None, :]
    dest = (jnp.sum(jnp.where(onehot, seg_start[None, :], 0), axis=1) + ranks.reshape(-1)).astype(jnp.int32)
    n_blocks = -(-n_pairs // tm) + MOE_EXPERTS
    n_rows = n_blocks * tm
    token_of_pair = (jnp.arange(n_pairs, dtype=jnp.int32) // MOE_TOP_K)
    filler = jnp.arange(n_rows, dtype=jnp.int32) % t
    row_token = filler.at[dest].set(token_of_pair) * (h32.shape[1] // LANES)
    block_start = jnp.arange(n_blocks, dtype=jnp.int32) * tm
    block_expert = jnp.minimum(jnp.sum((seg_end[None, :] <= block_start[:, None]).astype(jnp.int32), axis=1),
                               MOE_EXPERTS - 1)
    n_used = (seg_end[-1:] // tm).astype(jnp.int32)
    y_rows = moe_experts(h_rows, row_token, block_expert, n_used, w1, w3, w2, layer, tm=tm)
    ctm = min(COMBINE_TM, t)
    dest_tiles = jnp.transpose(dest.reshape(t // ctm, ctm, MOE_TOP_K), (0, 2, 1))
    return moe_combine_ln(y_rows, dest_tiles, gates, h32, gain, bias, tm=ctm)


def kernel(x, gdn_w_in, gdn_conv, gdn_a_log, gdn_dt_bias, gdn_norm, gdn_w_out, kv_w_k, kv_w_v, dil_w_q, dil_w_o,
           rel_bias, ln_gain, ln_bias, moe_w_group, moe_b_group, moe_w_expert, moe_b_expert, moe_w1, moe_w3, moe_w2):
    batch, seq, d = x.shape
    t = batch * seq
    h32 = x.reshape(t, d)
    h16 = None
    k = v = None
    biases = None
    for layer in range(DEPTH):
        if layer < N_A_LAYERS:
            h32, h16, h_rows = gated_deltanet_layer(
                h32, gdn_w_in[layer], gdn_conv[layer], gdn_a_log[layer], gdn_dt_bias[layer],
                gdn_norm[layer], gdn_w_out[layer], ln_gain[layer, 0], ln_bias[layer, 0], batch, seq)
        else:
            if layer == N_A_LAYERS:
                k = matmul(h16, kv_w_k.astype(BF16), out_dtype=BF16)
                v = matmul(h16, kv_w_v.astype(BF16), out_dtype=BF16)
                biases = [_band_bias(rel_bias[:, gi * DIL_HEADS:(gi + 1) * DIL_HEADS], dil)
                          for gi, (_, dil) in enumerate(DIL_CONFIGS)]
            j = layer - N_A_LAYERS
            outs, lses = [], []
            for gi, (_, dil) in enumerate(DIL_CONFIGS):
                q = matmul(h16, dil_w_q[j][:, gi * DIL_KV_DIM:(gi + 1) * DIL_KV_DIM].astype(BF16), out_dtype=BF16)
                o, lse = dilated_group_attention(q, k, v, biases[gi], dil, batch, seq)
                outs.append(o)
                lses.append(lse)
            h32, h16, h_rows = dilated_out(outs, lses, dil_w_o[j].astype(BF16), h32, ln_gain[layer, 0], ln_bias[layer, 0])
        h32, h16 = moe_layer(h32, h_rows, moe_w_group[layer], moe_b_group[layer], moe_w_expert[layer], moe_b_expert[layer],
                             moe_w1, moe_w3, moe_w2, layer, ln_gain[layer, 1], ln_bias[layer, 1])
    return h32.reshape(batch, seq, d)
```

```python
import functools
import math

import jax
import jax.numpy as jnp
from jax import lax
from jax.experimental import pallas as pl
from jax.experimental.pallas import tpu as pltpu

F32 = jnp.float32
BF16 = jnp.bfloat16

DEPTH = 4
N_A_LAYERS = DEPTH // 2
HEAD_DIM = 128
GDN_QK_HEADS = 16
GDN_V_HEADS = 32
GDN_CONV = 4
GDN_CHUNK = 64
GDN_INV_BLOCK = 16
GDN_QK_DIM = GDN_QK_HEADS * HEAD_DIM
GDN_V_DIM = GDN_V_HEADS * HEAD_DIM
GDN_CONV_DIM = 2 * GDN_QK_DIM + GDN_V_DIM
DIL_CONFIGS = ((128, 1), (512, 4), (2048, 16))
N_DIL = len(DIL_CONFIGS)
DIL_HEADS = 8
DIL_BLOCK = 128
DIL_KV_DIM = DIL_HEADS * HEAD_DIM
REL_BUCKETS = 32
REL_MAX_DIST = 2048
MOE_GROUPS = 4
MOE_EXPERTS_PER_GROUP = 8
MOE_EXPERTS = MOE_GROUPS * MOE_EXPERTS_PER_GROUP
MOE_TOP_K = 2
DEEPNORM_ALPHA = (2 * DEPTH) ** 0.25
LN_EPS = 1e-5
RMS_EPS = 1e-6
L2_EPS = 1e-6

LANES = 128
SUBLANES = 8
VMEM_LIMIT_BYTES = 56 * 1024 * 1024

MM_TM = 1024
MM_TN = 1024
LN_TM = 256
GATE_TM = 1024
GDN_TB = 512
GDN_QK_PER_STEP = 4
MOE_TM = 256
ROUTER_TM = 1024
COMBINE_TM = 256


def _params(*semantics):
    return pltpu.CompilerParams(dimension_semantics=semantics, vmem_limit_bytes=VMEM_LIMIT_BYTES)


def _split_bf16(x):
    hi = x.astype(BF16)
    return hi, (x - hi.astype(F32)).astype(BF16)


def _dot_split(x, w2_ref):
    n = w2_ref.shape[1] // 2
    hi, lo = _split_bf16(x)
    a = jnp.dot(hi, w2_ref[...], preferred_element_type=F32)
    b = jnp.dot(lo, w2_ref[:, :n], preferred_element_type=F32)
    return a[:, :n] + a[:, n:] + b


def _hi_lo_columns(w):
    hi = w.astype(BF16)
    return jnp.concatenate([hi, (w - hi.astype(F32)).astype(BF16)], axis=1)


def _store_row_contiguous(ref, y):
    rows, width = y.shape
    w = width // LANES
    for c in range(w):
        ref[pl.ds(c, rows, stride=w), :] = y[:, c * LANES:(c + 1) * LANES]


def _load_row_contiguous(ref, rows):
    w = ref.shape[0] // rows
    return jnp.concatenate([ref[pl.ds(c, rows, stride=w), :] for c in range(w)], axis=1)


def _layer_norm_rows(y, gain, bias):
    mu = jnp.mean(y, axis=-1, keepdims=True)
    yc = y - mu
    var = jnp.mean(yc * yc, axis=-1, keepdims=True)
    return yc * lax.rsqrt(var + LN_EPS) * gain + bias


def _matmul_kernel(a_ref, b_ref, o_ref):
    o_ref[...] = jnp.dot(a_ref[...].astype(BF16), b_ref[...],
                         preferred_element_type=F32).astype(o_ref.dtype)


def matmul(a, b, out_dtype=F32, tm=MM_TM, tn=MM_TN):
    m, k = a.shape
    n = b.shape[1]
    tm, tn = min(tm, m), min(tn, n)
    assert m % tm == 0 and n % tn == 0
    return pl.pallas_call(
        _matmul_kernel,
        out_shape=jax.ShapeDtypeStruct((m, n), out_dtype),
        grid=(n // tn, m // tm),
        in_specs=[pl.BlockSpec((tm, k), lambda j, i: (i, 0)),
                  pl.BlockSpec((k, tn), lambda j, i: (0, j))],
        out_specs=pl.BlockSpec((tm, tn), lambda j, i: (i, j)),
        compiler_params=_params("parallel", "parallel"),
        name="matmul",
    )(a, b)


def _proj_ln_kernel(a_ref, w_ref, h_ref, gain_ref, bias_ref, o32_ref, o16_ref, orow_ref):
    mix = jnp.dot(a_ref[...].astype(BF16), w_ref[...], preferred_element_type=F32)
    y = _layer_norm_rows(DEEPNORM_ALPHA * h_ref[...] + mix, gain_ref[...], bias_ref[...])
    o32_ref[...] = y
    o16_ref[...] = y.astype(BF16)
    _store_row_contiguous(orow_ref, y)


def proj_residual_ln(a, w, h, gain, bias, tm=LN_TM):
    m, k = a.shape
    d = w.shape[1]
    tm = min(tm, m)
    assert m % tm == 0
    return pl.pallas_call(
        _proj_ln_kernel,
        out_shape=(jax.ShapeDtypeStruct((m, d), F32), jax.ShapeDtypeStruct((m, d), BF16),
                   jax.ShapeDtypeStruct((m * (d // LANES), LANES), F32)),
        grid=(m // tm,),
        in_specs=[pl.BlockSpec((tm, k), lambda i: (i, 0)),
                  pl.BlockSpec((k, d), lambda i: (0, 0)),
                  pl.BlockSpec((tm, d), lambda i: (i, 0)),
                  pl.BlockSpec((1, d), lambda i: (0, 0)),
                  pl.BlockSpec((1, d), lambda i: (0, 0))],
        out_specs=(pl.BlockSpec((tm, d), lambda i: (i, 0)),
                   pl.BlockSpec((tm, d), lambda i: (i, 0)),
                   pl.BlockSpec((tm * (d // LANES), LANES), lambda i: (i, 0))),
        compiler_params=_params("parallel"),
        name="proj_residual_ln",
    )(a, w, h, gain.reshape(1, d), bias.reshape(1, d))


def _gdn_gate_kernel(h_ref, w_ref, alog_ref, dt_ref, o_ref, h16_ref):
    tm = h_ref.shape[0]
    n_groups = o_ref.shape[0]
    h16_ref[...] = h_ref[...].astype(BF16)
    width = 2 * GDN_V_HEADS // n_groups
    ba = _dot_split(h_ref[...], w_ref)
    xa = ba + dt_ref[...]
    softplus = jnp.maximum(xa, 0.0) + jnp.log(1.0 + jnp.exp(-jnp.abs(xa)))
    g = -jnp.exp(alog_ref[...]) * softplus
    row = lax.broadcasted_iota(jnp.int32, (tm, tm), 0)
    col = lax.broadcasted_iota(jnp.int32, (tm, tm), 1)
    tri = ((row // GDN_CHUNK == col // GDN_CHUNK) & (col <= row)).astype(BF16)
    g_hi, g_lo = _split_bf16(g)
    gc2 = jnp.dot(tri, jnp.concatenate([g_hi, g_lo], axis=1), preferred_element_type=F32)
    gc = gc2[:, :LANES] + gc2[:, LANES:]
    lane = lax.broadcasted_iota(jnp.int32, ba.shape, 1)
    slab = jnp.where(lane % width < width // 2, jax.nn.sigmoid(ba), gc)
    for r in range(n_groups):
        o_ref[r] = slab if r == 0 else pltpu.roll(slab, LANES - width * r, axis=1)


def gdn_gates(h, w_ba, a_log, dt_bias, n_qk=GDN_QK_PER_STEP, tm=GATE_TM):
    m, d = h.shape
    tm = min(tm, m)
    v = 2 * n_qk
    n_groups = GDN_V_HEADS // v
    cols = [(v * (l // (2 * v)) + l % (2 * v)) if l % (2 * v) < v else (GDN_V_HEADS + v * (l // (2 * v)) + l % (2 * v) - v)
            for l in range(2 * GDN_V_HEADS)]
    is_decay = jnp.array([l % (2 * v) >= v for l in range(2 * GDN_V_HEADS)])
    head = jnp.array([c % GDN_V_HEADS for c in cols], jnp.int32)
    pad = LANES - 2 * GDN_V_HEADS
    w = jnp.pad(w_ba[:, jnp.array(cols, jnp.int32)], ((0, 0), (0, pad)))
    alog = jnp.pad(jnp.where(is_decay, a_log[head], 0.0), (0, pad)).reshape(1, LANES)
    dt = jnp.pad(jnp.where(is_decay, dt_bias[head], 0.0), (0, pad)).reshape(1, LANES)
    return pl.pallas_call(
        _gdn_gate_kernel,
        out_shape=(jax.ShapeDtypeStruct((n_groups, m, LANES), F32), jax.ShapeDtypeStruct((m, d), BF16)),
        grid=(m // tm,),
        in_specs=[pl.BlockSpec((tm, d), lambda i: (i, 0)),
                  pl.BlockSpec((d, 2 * LANES), lambda i: (0, 0)),
                  pl.BlockSpec((1, LANES), lambda i: (0, 0)),
                  pl.BlockSpec((1, LANES), lambda i: (0, 0))],
        out_specs=(pl.BlockSpec((n_groups, tm, LANES), lambda i: (0, i, 0)), pl.BlockSpec((tm, d), lambda i: (i, 0))),
        compiler_params=_params("parallel"),
        name="gdn_gates",
    )(h, _hi_lo_columns(w), alog, dt)


def _dot_bf16(a, b):
    return jnp.dot(a.astype(BF16), b.astype(BF16), preferred_element_type=F32)


def _dot_nt(a, b):
    return lax.dot_general(a.astype(BF16), b.astype(BF16), (((1,), (1,)), ((), ())),
                           preferred_element_type=F32)


def _dot_tn(a, b):
    return lax.dot_general(a.astype(BF16), b.astype(BF16), (((0,), (0,)), ((), ())),
                           preferred_element_type=F32)


def _block_diag2(x0, x1):
    z = jnp.zeros_like(x0)
    return jnp.concatenate([jnp.concatenate([x0, z], axis=1), jnp.concatenate([z, x1], axis=1)], axis=0)


def _gdn_kernel(q_ref, k_ref, v_ref, z_ref, wq_ref, wk_ref, wv_ref, g_ref, nw_ref,
                o_ref, qx, kx, vx, qn, kn, vn, state):
    blk = pl.program_id(2)
    tb = q_ref.shape[0]
    n_chunks = tb // GDN_CHUNK
    n_qk = q_ref.shape[1] // HEAD_DIM
    c = GDN_CHUNK
    hd = HEAD_DIM
    tail = SUBLANES

    @pl.when(blk == 0)
    def _():
        qx[0:tail, :] = jnp.zeros((tail, qx.shape[1]), F32)
        kx[0:tail, :] = jnp.zeros((tail, kx.shape[1]), F32)
        vx[0:tail, :] = jnp.zeros((tail, vx.shape[1]), F32)
        state[...] = jnp.zeros(state.shape, F32)

    def conv_silu(x_ref, ext, w_ref):
        ext[tail:tail + tb, :] = x_ref[...].astype(F32)
        y = None
        for j in range(GDN_CONV):
            term = w_ref[j:j + 1, :] * ext[pl.ds(tail - (GDN_CONV - 1) + j, tb), :]
            y = term if y is None else y + term
        ext[0:tail, :] = ext[tb:tb + tail, :]
        return y * jax.nn.sigmoid(y)

    def l2n(x):
        heads = [x[:, g * hd:(g + 1) * hd] for g in range(n_qk)]
        return jnp.concatenate([xh * lax.rsqrt(jnp.sum(xh * xh, axis=-1, keepdims=True) + L2_EPS) for xh in heads], axis=1)

    qn[...] = l2n(conv_silu(q_ref, qx, wq_ref)) * (HEAD_DIM ** -0.5)
    kn[...] = l2n(conv_silu(k_ref, kx, wk_ref))
    vn[...] = conv_silu(v_ref, vx, wv_ref)

    row = lax.broadcasted_iota(jnp.int32, (c, 2 * c), 0)
    lane = lax.broadcasted_iota(jnp.int32, (c, 2 * c), 1)
    col = lane & (c - 1)
    head0 = lane < c
    causal = col <= row
    strict = col < row
    assert c == 4 * GDN_INV_BLOCK
    in_block = (row // GDN_INV_BLOCK) == (col // GDN_INV_BLOCK)
    eye = (row == col).astype(F32)
    norm_w = nw_ref[...]

    def mm(x, y):
        yb = jnp.concatenate([jnp.where(head0, y, 0.0), jnp.where(head0, 0.0, y)], axis=0)
        return _dot_bf16(x, yb)

    units = [(ci, g) for ci in range(n_chunks) for g in range(n_qk)]
    each = lambda f, *lists: [f(*xs) for xs in zip(*lists)]
    qcs = [qn[ci * c:(ci + 1) * c, g * hd:(g + 1) * hd] for ci, g in units]
    kcs = [kn[ci * c:(ci + 1) * c, g * hd:(g + 1) * hd] for ci, g in units]
    n_v = 2 * n_qk
    gates = g_ref[0]
    gates_t = gates.T
    betas = [[gates[ci * c:(ci + 1) * c, 2 * g + hh:2 * g + hh + 1] for hh in range(2)] for ci, g in units]
    gcs = [[gates[ci * c:(ci + 1) * c, n_v + 2 * g + hh:n_v + 2 * g + hh + 1] for hh in range(2)] for ci, g in units]

    def packed_row(ci, g):
        w0 = (ci // 2) * 2 * c
        r0, r1 = (gates_t[n_v + 2 * g + hh:n_v + 2 * g + hh + 1, w0:w0 + 2 * c] for hh in range(2))
        if ci % 2 == 0:
            return jnp.where(head0[0:1], r0, pltpu.roll(r1, c, axis=1))
        return jnp.where(head0[0:1], pltpu.roll(r0, c, axis=1), r1)

    grps = [packed_row(ci, g) for ci, g in units]
    egs = [[jnp.exp(g[hh]) for hh in range(2)] for g in gcs]
    glast = [[g[hh][c - 1:c, :] for hh in range(2)] for g in gcs]
    both = each(lambda q, k: _dot_nt(jnp.concatenate([q, k], axis=0), jnp.concatenate([k, k], axis=0)), qcs, kcs)
    decayp = each(lambda g, gr: jnp.exp(jnp.where(causal, jnp.where(head0, g[0], g[1]) - gr, -jnp.inf)), gcs, grps)
    ap = each(lambda b, bo, dc: jnp.where(strict, jnp.where(head0, b[0], b[1]) * bo[c:] * dc, 0.0), betas, both, decayp)
    qkd = each(lambda bo, dc: bo[:c] * dc, both, decayp)
    d = each(lambda a: jnp.where(in_block, a, 0.0), ap)
    e = each(lambda a, dd: a - dd, ap, d)
    n1 = each(lambda dd: -dd, d)
    p = each(lambda n: eye + n, n1)
    n2 = each(mm, n1, n1)
    p = each(lambda pp, n: pp + mm(pp, n), p, n2)
    n4 = each(mm, n2, n2)
    p = each(lambda pp, n: pp + mm(pp, n), p, n4)
    n8 = each(mm, n4, n4)
    dinv = each(lambda pp, n: pp + mm(pp, n), p, n8)
    nb = each(mm, dinv, e)
    nb2 = each(mm, nb, nb)
    qq = each(lambda n: eye - n, nb)
    qq = each(lambda x, n: x + mm(x, n), qq, nb2)
    tinvp = each(mm, qq, dinv)
    rhs = [[jnp.concatenate([vn[ci * c:(ci + 1) * c, (2 * g + hh) * hd:(2 * g + hh + 1) * hd] * betas[u][hh],
                             kcs[u] * (betas[u][hh] * egs[u][hh])], axis=1) for hh in range(2)]
           for u, (ci, g) in enumerate(units)]
    uw = each(lambda t, r: _dot_bf16(t, _block_diag2(r[0], r[1])), tinvp, rhs)
    wu = [[jnp.concatenate([x[:, (2 * hh + 1) * hd:(2 * hh + 2) * hd], x[:, 2 * hh * hd:(2 * hh + 1) * hd]], axis=1)
           for hh in range(2)] for x in uw]
    qo = each(lambda x, w2: _dot_bf16(x, _block_diag2(w2[0], w2[1])), qkd, wu)
    pre = {}
    for u, (ci, g) in enumerate(units):
        for hh in range(2):
            rb = _dot_tn(kcs[u] * jnp.exp(glast[u][hh] - gcs[u][hh]), wu[u][hh])
            q_eff = qcs[u] * egs[u][hh] - qo[u][:, 2 * hh * hd:(2 * hh + 1) * hd]
            rq = jnp.concatenate([rb[:, :hd], q_eff], axis=0).astype(BF16)
            pre[ci, 2 * g + hh] = (rq, rb[:, hd:], qo[u][:, (2 * hh + 1) * hd:(2 * hh + 2) * hd], jnp.exp(glast[u][hh]))

    states = [state[vh] for vh in range(n_v)]
    for ci in range(n_chunks):
        r0 = ci * c
        for vh in range(n_v):
            rq, b_c, o_c, cd = pre[ci, vh]
            s = states[vh]
            res = jnp.dot(rq, s.astype(BF16), preferred_element_type=F32)
            states[vh] = s * cd - res[:hd] + b_c
            out = res[hd:] + o_c
            out = out * lax.rsqrt(jnp.mean(out * out, axis=-1, keepdims=True) + RMS_EPS) * norm_w
            zc = z_ref[r0:r0 + c, vh * hd:(vh + 1) * hd].astype(F32)
            out = out * (zc * jax.nn.sigmoid(zc))
            o_ref[r0:r0 + c, vh * hd:(vh + 1) * hd] = out.astype(o_ref.dtype)
    for vh in range(n_v):
        state[vh] = states[vh]


def gdn_delta(proj, conv_w, gates, norm_w, batch, seq, tb=GDN_TB, n_qk=GDN_QK_PER_STEP):
    t = proj.shape[0]
    tb = min(tb, seq)
    nblk = seq // tb
    hd = HEAD_DIM
    qw = n_qk * hd
    vw = 2 * qw
    kq = GDN_QK_DIM // qw
    v0 = (2 * GDN_QK_DIM) // vw
    z0 = GDN_CONV_DIM // vw
    return pl.pallas_call(
        _gdn_kernel,
        out_shape=jax.ShapeDtypeStruct((t, GDN_V_DIM), BF16),
        grid=(batch, GDN_QK_HEADS // n_qk, nblk),
        in_specs=[
            pl.BlockSpec((tb, qw), lambda b, h, i: (b * nblk + i, h)),
            pl.BlockSpec((tb, qw), lambda b, h, i: (b * nblk + i, kq + h)),
            pl.BlockSpec((tb, vw), lambda b, h, i: (b * nblk + i, v0 + h)),
            pl.BlockSpec((tb, vw), lambda b, h, i: (b * nblk + i, z0 + h)),
            pl.BlockSpec((GDN_CONV, qw), lambda b, h, i: (0, h)),
            pl.BlockSpec((GDN_CONV, qw), lambda b, h, i: (0, kq + h)),
            pl.BlockSpec((GDN_CONV, vw), lambda b, h, i: (0, v0 + h)),
            pl.BlockSpec((1, tb, LANES), lambda b, h, i: (h, b * nblk + i, 0)),
            pl.BlockSpec((1, hd), lambda b, h, i: (0, 0)),
        ],
        out_specs=pl.BlockSpec((tb, vw), lambda b, h, i: (b * nblk + i, h)),
        scratch_shapes=[
            pltpu.VMEM((tb + SUBLANES, qw), F32),
            pltpu.VMEM((tb + SUBLANES, qw), F32),
            pltpu.VMEM((tb + SUBLANES, vw), F32),
            pltpu.VMEM((tb, qw), F32),
            pltpu.VMEM((tb, qw), F32),
            pltpu.VMEM((tb, vw), F32),
            pltpu.VMEM((2 * n_qk, hd, hd), F32),
        ],
        compiler_params=_params("parallel", "parallel", "arbitrary"),
        name="gdn_delta",
    )(proj, proj, proj, proj, conv_w, conv_w, conv_w, gates, norm_w.reshape(1, hd))


def gated_deltanet_layer(h32, w_in, conv_w, a_log, dt_bias, norm_w, w_out, gain, bias, batch, seq):
    main = GDN_CONV_DIM + GDN_V_DIM
    gates, h16 = gdn_gates(h32, w_in[:, main:], a_log, dt_bias)
    proj = matmul(h16, w_in[:, :main].astype(BF16))
    o = gdn_delta(proj, conv_w, gates, norm_w, batch, seq)
    return proj_residual_ln(o, w_out.astype(BF16), h32, gain, bias)


DIL_STEP_TOKENS = {1: 1024, 4: 1024, 16: 4096}
DIL_STEP_HEADS = {1: 8, 4: 8, 16: 2}
DIL_CHAINS = 8


def _dil_attn_kernel(q_ref, kp_ref, ko_ref, vp_ref, vo_ref, bias_ref, o_ref, lse_ref, qs, ks, vs, os_, *, dil):
    blk = DIL_BLOCK
    span = q_ref.shape[0]
    hp = q_ref.shape[1] // HEAD_DIM
    halo = dil * blk
    m_blocks = span // halo
    first_span = pl.program_id(1) == 0
    hg = pl.program_id(2)
    for h in range(hp):
        s = slice(h * HEAD_DIM, (h + 1) * HEAD_DIM)
        qs[h] = q_ref[:, s].astype(F32)
        ks[h, 0:halo, :] = kp_ref[:, s].astype(F32)
        ks[h, halo:halo + span, :] = ko_ref[:, s].astype(F32)
        vs[h, 0:halo, :] = vp_ref[:, s].astype(F32)
        vs[h, halo:halo + span, :] = vo_ref[:, s].astype(F32)

    qi = lax.broadcasted_iota(jnp.int32, (blk, blk), 0)
    kj = lax.broadcasted_iota(jnp.int32, (blk, blk), 1)
    band_prev = qi <= kj
    mask_own = kj <= qi
    lane = lax.broadcasted_iota(jnp.int32, (blk, LANES), 1)
    scale = HEAD_DIM ** -0.5

    def rows(start):
        return pl.ds(start, blk, stride=dil) if dil > 1 else pl.ds(start, blk)

    @pl.when(hg == 0)
    def _():
        lse_ref[...] = jnp.zeros(lse_ref.shape, F32)

    combos = [(r, j, h) for r in range(dil) for j in range(m_blocks) for h in range(hp)]
    lse_tiles = {}
    for c0 in range(0, len(combos), DIL_CHAINS):
        group = combos[c0:c0 + DIL_CHAINS]
        q = [qs[h, rows(r + halo * j), :] for r, j, h in group]
        bias = [bias_ref[hg * hp + h] for _, _, h in group]
        mprev = [band_prev & jnp.logical_not(first_span) if j == 0 else band_prev for _, j, _ in group]
        s_prev = [jnp.where(mp, _dot_nt(qq, ks[h, rows(r + halo * j), :]) * scale + bb[:, :blk], -jnp.inf)
                  for (r, j, h), qq, bb, mp in zip(group, q, bias, mprev)]
        s_own = [jnp.where(mask_own, _dot_nt(qq, ks[h, rows(r + halo * (j + 1)), :]) * scale + bb[:, blk:], -jnp.inf)
                 for (r, j, h), qq, bb in zip(group, q, bias)]
        m = [jnp.maximum(jnp.max(sp, axis=-1, keepdims=True), jnp.max(so, axis=-1, keepdims=True))
             for sp, so in zip(s_prev, s_own)]
        p_prev = [jnp.exp(sp - mm) for sp, mm in zip(s_prev, m)]
        p_own = [jnp.exp(so - mm) for so, mm in zip(s_own, m)]
        den = [jnp.sum(pp, axis=-1, keepdims=True) + jnp.sum(po, axis=-1, keepdims=True)
               for pp, po in zip(p_prev, p_own)]
        o = [_dot_bf16(pp, vs[h, rows(r + halo * j), :]) + _dot_bf16(po, vs[h, rows(r + halo * (j + 1)), :])
             for (r, j, h), pp, po in zip(group, p_prev, p_own)]
        for (r, j, h), oo, dd, mm in zip(group, o, den, m):
            os_[h, rows(r + halo * j), :] = oo / dd
            prev_tile = lse_tiles[r, j] if (r, j) in lse_tiles else lse_ref[rows(r + halo * j), :]
            lse_tiles[r, j] = jnp.where(lane == hg * hp + h, mm + jnp.log(dd), prev_tile)
            if h == hp - 1:
                lse_ref[rows(r + halo * j), :] = lse_tiles.pop((r, j))
    for h in range(hp):
        o_ref[:, h * HEAD_DIM:(h + 1) * HEAD_DIM] = os_[h]


def dilated_group_attention(q, k, v, bias, dil, batch, seq):
    t, kvd = q.shape
    span = min(DIL_STEP_TOKENS[dil], seq)
    hp = DIL_STEP_HEADS[dil]
    halo = dil * DIL_BLOCK
    assert seq % span == 0 and span % halo == 0 and DIL_HEADS % hp == 0
    n_span = seq // span
    ratio = span // halo
    cols = hp * HEAD_DIM
    own = lambda b, i, g: (b * n_span + i, g)
    prev = lambda b, i, g: (jnp.maximum((b * n_span + i) * ratio - 1, 0), g)
    o, lse = pl.pallas_call(
        functools.partial(_dil_attn_kernel, dil=dil),
        out_shape=(jax.ShapeDtypeStruct((t, kvd), F32),
                   jax.ShapeDtypeStruct((t, LANES), F32)),
        grid=(batch, n_span, DIL_HEADS // hp),
        in_specs=[
            pl.BlockSpec((span, cols), own),
            pl.BlockSpec((halo, cols), prev),
            pl.BlockSpec((span, cols), own),
            pl.BlockSpec((halo, cols), prev),
            pl.BlockSpec((span, cols), own),
            pl.BlockSpec((DIL_HEADS, DIL_BLOCK, 2 * DIL_BLOCK), lambda b, i, g: (0, 0, 0)),
        ],
        out_specs=(pl.BlockSpec((span, cols), own), pl.BlockSpec((span, LANES), lambda b, i, g: (b * n_span + i, 0))),
        scratch_shapes=[pltpu.VMEM((hp, span, HEAD_DIM), F32), pltpu.VMEM((hp, halo + span, HEAD_DIM), F32),
                        pltpu.VMEM((hp, halo + span, HEAD_DIM), F32), pltpu.VMEM((hp, span, HEAD_DIM), F32)],
        compiler_params=_params("parallel", "parallel", "arbitrary"),
        name=f"dilated_attention_d{dil}",
    )(q, k, k, v, v, bias)
    return o, lse


def _dil_out_kernel(o0_ref, o1_ref, o2_ref, l0_ref, l1_ref, l2_ref, w_ref, h_ref, gain_ref, bias_ref,
                    o32_ref, o16_ref, orow_ref, mix_ref):
    l0, l1, l2 = l0_ref[...], l1_ref[...], l2_ref[...]
    m = jnp.maximum(jnp.maximum(l0, l1), l2)
    e0, e1, e2 = jnp.exp(l0 - m), jnp.exp(l1 - m), jnp.exp(l2 - m)
    inv = 1.0 / (e0 + e1 + e2)
    w0, w1, w2 = e0 * inv, e1 * inv, e2 * inv
    for hd in range(DIL_HEADS):
        sl = slice(hd * HEAD_DIM, (hd + 1) * HEAD_DIM)
        mix_ref[:, sl] = (w0[:, hd:hd + 1] * o0_ref[:, sl] + w1[:, hd:hd + 1] * o1_ref[:, sl]
                          + w2[:, hd:hd + 1] * o2_ref[:, sl])
    mix = jnp.dot(mix_ref[...].astype(BF16), w_ref[...], preferred_element_type=F32)
    y = _layer_norm_rows(DEEPNORM_ALPHA * h_ref[...] + mix, gain_ref[...], bias_ref[...])
    o32_ref[...] = y
    o16_ref[...] = y.astype(BF16)
    _store_row_contiguous(orow_ref, y)


def dilated_out(outs, lses, w_o, h, gain, bias, tm=LN_TM):
    m, kvd = outs[0].shape
    d = w_o.shape[1]
    tm = min(tm, m)
    row = lambda i: (i, 0)
    fixed = lambda i: (0, 0)
    return pl.pallas_call(
        _dil_out_kernel,
        out_shape=(jax.ShapeDtypeStruct((m, d), F32), jax.ShapeDtypeStruct((m, d), BF16),
                   jax.ShapeDtypeStruct((m * (d // LANES), LANES), F32)),
        grid=(m // tm,),
        in_specs=[pl.BlockSpec((tm, kvd), row)] * 3 + [pl.BlockSpec((tm, LANES), row)] * 3 + [
            pl.BlockSpec((kvd, d), fixed), pl.BlockSpec((tm, d), row),
            pl.BlockSpec((1, d), fixed), pl.BlockSpec((1, d), fixed)],
        out_specs=(pl.BlockSpec((tm, d), row), pl.BlockSpec((tm, d), row), pl.BlockSpec((tm * (d // LANES), LANES), row)),
        scratch_shapes=[pltpu.VMEM((tm, kvd), F32)],
        compiler_params=_params("parallel"),
        name="dilated_out_ln",
    )(*outs, *lses, w_o, h, gain.reshape(1, d), bias.reshape(1, d))


def _t5_bucket(dist):
    exact = REL_BUCKETS // 2
    distf = jnp.maximum(dist, exact).astype(F32)
    large = exact + (jnp.log(distf / exact) / math.log(REL_MAX_DIST / exact)
                     * (REL_BUCKETS - exact)).astype(jnp.int32)
    return jnp.where(dist < exact, dist, jnp.minimum(large, REL_BUCKETS - 1))


def _band_bias(rel_bias_g, dil):
    qi = jnp.arange(DIL_BLOCK)[:, None]
    kj = jnp.arange(2 * DIL_BLOCK)[None, :]
    steps = DIL_BLOCK + qi - kj
    bucket = _t5_bucket(jnp.maximum(steps, 0) * dil)
    onehot = (bucket[:, :, None] == jnp.arange(REL_BUCKETS)[None, None, :]).astype(F32)
    return jnp.einsum('qkb,bh->hqk', onehot, rel_bias_g.astype(F32), precision=lax.Precision.HIGHEST)


def _router_kernel(h_ref, w_ref, b_ref, o_ref, cnt_ref, running):
    @pl.when(pl.program_id(0) == 0)
    def _():
        running[...] = jnp.zeros(running.shape, F32)

    logits = _dot_split(h_ref[...], w_ref) + b_ref[...]
    lane = lax.broadcasted_iota(jnp.int32, logits.shape, 1)
    neg = -jnp.inf
    big = jnp.int32(LANES)
    is_group = lane < MOE_GROUPS
    gl = jnp.where(is_group, logits, neg)
    gmax = jnp.max(gl, axis=-1, keepdims=True)
    gidx = jnp.min(jnp.where(gl == gmax, lane, big), axis=-1, keepdims=True)
    gden = jnp.sum(jnp.exp(gl - gmax), axis=-1, keepdims=True)
    group_gate = 1.0 / gden
    lo = MOE_GROUPS + gidx * MOE_EXPERTS_PER_GROUP
    in_group = (lane >= lo) & (lane < lo + MOE_EXPERTS_PER_GROUP)
    ll = jnp.where(in_group, logits, neg)
    m1 = jnp.max(ll, axis=-1, keepdims=True)
    i1 = jnp.min(jnp.where(ll == m1, lane, big), axis=-1, keepdims=True)
    rest = jnp.where(lane == i1, neg, ll)
    m2 = jnp.max(rest, axis=-1, keepdims=True)
    i2 = jnp.min(jnp.where(rest == m2, lane, big), axis=-1, keepdims=True)
    den = jnp.sum(jnp.exp(ll - m1), axis=-1, keepdims=True)
    p1 = 1.0 / den
    p2 = jnp.exp(m2 - m1) / den
    g1 = group_gate * p1 / (p1 + p2)
    g2 = group_gate * p2 / (p1 + p2)
    id1 = (i1 - MOE_GROUPS).astype(F32)
    id2 = (i2 - MOE_GROUPS).astype(F32)
    tm = logits.shape[0]
    oh1 = (lane == i1).astype(F32)
    oh2 = (lane == i2).astype(F32)
    both = oh1 + oh2
    r_i = lax.broadcasted_iota(jnp.int32, (tm, tm), 0)
    c_i = lax.broadcasted_iota(jnp.int32, (tm, tm), 1)
    before = jnp.dot((c_i < r_i).astype(BF16), both.astype(BF16), preferred_element_type=F32) + running[...]
    rank1 = jnp.sum(before * oh1, axis=-1, keepdims=True)
    rank2 = jnp.sum(before * oh2, axis=-1, keepdims=True)
    running[...] = running[...] + jnp.sum(both, axis=0, keepdims=True)
    cnt_ref[...] = running[...]
    o_ref[...] = jnp.where(lane == 0, id1, jnp.where(lane == 1, id2, jnp.where(lane == 2, g1, jnp.where(
        lane == 3, g2, jnp.where(lane == 4, rank1, rank2)))))


def moe_router(h, w_group, b_group, w_expert, b_expert, tm=ROUTER_TM):
    m, d = h.shape
    tm = min(tm, m)
    pad = LANES - MOE_GROUPS - MOE_EXPERTS
    w = jnp.pad(jnp.concatenate([w_group, w_expert], axis=1), ((0, 0), (0, pad)))
    b = jnp.pad(jnp.concatenate([b_group, b_expert]), (0, pad)).reshape(1, LANES)
    out, cnt = pl.pallas_call(
        _router_kernel,
        out_shape=(jax.ShapeDtypeStruct((m, LANES), F32), jax.ShapeDtypeStruct((1, LANES), F32)),
        grid=(m // tm,),
        in_specs=[pl.BlockSpec((tm, d), lambda i: (i, 0)),
                  pl.BlockSpec((d, 2 * LANES), lambda i: (0, 0)),
                  pl.BlockSpec((1, LANES), lambda i: (0, 0))],
        out_specs=(pl.BlockSpec((tm, LANES), lambda i: (i, 0)), pl.BlockSpec((1, LANES), lambda i: (0, 0))),
        scratch_shapes=[pltpu.VMEM((1, LANES), F32)],
        compiler_params=_params("arbitrary"),
        name="moe_router",
    )(h, _hi_lo_columns(w), b)
    k = MOE_TOP_K
    counts = cnt[0, MOE_GROUPS:MOE_GROUPS + MOE_EXPERTS].astype(jnp.int32)
    return out[:, :k].astype(jnp.int32), out[:, k:2 * k], out[:, 2 * k:3 * k].astype(jnp.int32), counts


def _gather_rows_start(src_hbm, idx_ref, idx_base, buf, sem, unrolled, row_tiles=None):
    if row_tiles is None:
        groups = buf.shape[0]

        def copy(g, j):
            pltpu.make_async_copy(src_hbm.at[pl.ds(idx_ref[idx_base + g * SUBLANES + j], 1)],
                                  buf.at[g, pl.ds(j, 1)], sem).start()
    else:
        w = row_tiles
        groups = buf.shape[0] // (w * SUBLANES)

        def copy(g, j):
            r = g * SUBLANES + j
            pltpu.make_async_copy(src_hbm.at[pl.ds(pl.multiple_of(idx_ref[idx_base + r], w), w)],
                                  buf.at[pl.ds(pl.multiple_of(r * w, w), w)], sem).start()

    if unrolled:
        for g in range(groups):
            for j in range(SUBLANES):
                copy(g, j)
    else:
        def body(g, carry):
            for j in range(SUBLANES):
                copy(g, j)
            return carry
        lax.fori_loop(0, groups, body, 0)


def _gather_rows_wait(buf, sem):
    pltpu.make_async_copy(buf, buf, sem).wait()


def _gathered_blocks(src_hbm, idx_ref, bufs, sems, compute, row_tiles=None, n_active=None):
    i = pl.program_id(0)
    n = pl.num_programs(0) if n_active is None else n_active
    if row_tiles is None:
        rows = bufs[0].shape[0] * bufs[0].shape[1]
        load = lambda buf: buf[...].reshape(rows, buf.shape[2])
    else:
        rows = bufs[0].shape[0] // row_tiles
        load = lambda buf: _load_row_contiguous(buf, rows)

    @pl.when(i == 0)
    def _():
        _gather_rows_start(src_hbm, idx_ref, 0, bufs[0], sems.at[0], False, row_tiles)

    def step(s):
        _gather_rows_wait(bufs[s], sems.at[s])
        _gather_rows_start(src_hbm, idx_ref, jnp.minimum(i + 1, n - 1) * rows, bufs[1 - s], sems.at[1 - s],
                           True, row_tiles)
        compute(load(bufs[s]))

        @pl.when(i == n - 1)
        def _():
            _gather_rows_wait(bufs[1 - s], sems.at[1 - s])

    for s in range(2):
        pl.when((i % 2 == s) & (i < n))(functools.partial(step, s))


def _expert_kernel(bexp_ref, rtok_ref, nused_ref, h_hbm, w1_ref, w3_ref, w2_ref, y_ref,
                   xbuf_a, xbuf_b, sems, wb1, wb3, wb2):
    i = pl.program_id(0)

    @pl.when((i < nused_ref[0]) & ((i == 0) | (bexp_ref[i] != bexp_ref[jnp.maximum(i - 1, 0)])))
    def _():
        wb1[...] = w1_ref[0, 0].astype(BF16)
        wb3[...] = w3_ref[0, 0].astype(BF16)
        wb2[...] = w2_ref[0, 0].astype(BF16)

    def compute(x):
        x = x.astype(BF16)
        h1 = jnp.dot(x, wb1[...], preferred_element_type=F32)
        h3 = jnp.dot(x, wb3[...], preferred_element_type=F32)
        hmid = (h1 * jax.nn.sigmoid(h1) * h3).astype(BF16)
        y_ref[...] = jnp.dot(hmid, wb2[...], preferred_element_type=F32)

    _gathered_blocks(h_hbm, rtok_ref, (xbuf_a, xbuf_b), sems, compute, row_tiles=w1_ref.shape[2] // LANES,
                     n_active=nused_ref[0])

    @pl.when(i >= nused_ref[0])
    def _():
        y_ref[...] = jnp.zeros(y_ref.shape, y_ref.dtype)


def moe_experts(h_rows, row_token, block_expert, n_used, w1, w3, w2, layer, tm=MOE_TM):
    n_rows = row_token.shape[0]
    n_blocks = n_rows // tm
    d = w1.shape[2]
    row_tiles = d // LANES
    ff = w1.shape[3]
    grid_spec = pltpu.PrefetchScalarGridSpec(
        num_scalar_prefetch=3,
        grid=(n_blocks,),
        in_specs=[
            pl.BlockSpec(memory_space=pl.ANY),
            pl.BlockSpec((1, 1, d, ff), lambda i, be, rt, nu: (layer, be[i], 0, 0)),
            pl.BlockSpec((1, 1, d, ff), lambda i, be, rt, nu: (layer, be[i], 0, 0)),
            pl.BlockSpec((1, 1, ff, d), lambda i, be, rt, nu: (layer, be[i], 0, 0)),
        ],
        out_specs=pl.BlockSpec((tm, d), lambda i, be, rt, nu: (i, 0)),
        scratch_shapes=[pltpu.VMEM((tm * row_tiles, LANES), F32), pltpu.VMEM((tm * row_tiles, LANES), F32),
                        pltpu.SemaphoreType.DMA((2,)),
                        pltpu.VMEM((d, ff), BF16), pltpu.VMEM((d, ff), BF16), pltpu.VMEM((ff, d), BF16)],
    )
    return pl.pallas_call(
        _expert_kernel,
        out_shape=jax.ShapeDtypeStruct((n_rows, d), F32),
        grid_spec=grid_spec,
        compiler_params=_params("arbitrary"),
        name="moe_experts",
    )(block_expert, row_token, n_used, h_rows, w1, w3, w2)


def _combine_kernel(dest_ref, y_hbm, gate_ref, h_ref, gain_ref, bias_ref, o32_ref, o16_ref, ybuf_a, ybuf_b, sems):
    tm = h_ref.shape[0]

    def compute(yy):
        gates = gate_ref[...]
        ffn = yy[0:tm, :] * gates[:, 0:1] + yy[tm:, :] * gates[:, 1:2]
        y = _layer_norm_rows(DEEPNORM_ALPHA * h_ref[...] + ffn, gain_ref[...], bias_ref[...])
        o32_ref[...] = y
        o16_ref[...] = y.astype(BF16)

    _gathered_blocks(y_hbm, dest_ref, (ybuf_a, ybuf_b), sems, compute)


def moe_combine_ln(y_rows, dest, gates, h, gain, bias, tm=COMBINE_TM):
    m, d = h.shape
    tm = min(tm, m)
    grid_spec = pltpu.PrefetchScalarGridSpec(
        num_scalar_prefetch=1,
        grid=(m // tm,),
        in_specs=[
            pl.BlockSpec(memory_space=pl.ANY),
            pl.BlockSpec((tm, MOE_TOP_K), lambda i, de: (i, 0)),
            pl.BlockSpec((tm, d), lambda i, de: (i, 0)),
            pl.BlockSpec((1, d), lambda i, de: (0, 0)),
            pl.BlockSpec((1, d), lambda i, de: (0, 0)),
        ],
        out_specs=(pl.BlockSpec((tm, d), lambda i, de: (i, 0)),
                   pl.BlockSpec((tm, d), lambda i, de: (i, 0))),
        scratch_shapes=[pltpu.VMEM((MOE_TOP_K * tm // SUBLANES, SUBLANES, d), F32),
                        pltpu.VMEM((MOE_TOP_K * tm // SUBLANES, SUBLANES, d), F32), pltpu.SemaphoreType.DMA((2,))],
    )
    return pl.pallas_call(
        _combine_kernel,
        out_shape=(jax.ShapeDtypeStruct((m, d), F32), jax.ShapeDtypeStruct((m, d), BF16)),
        grid_spec=grid_spec,
        compiler_params=_params("arbitrary"),
        name="moe_combine_ln",
    )(dest.reshape(-1), y_rows, gates, h, gain.reshape(1, d), bias.reshape(1, d))


def moe_layer(h32, h_rows, w_group, b_group, w_expert, b_expert, w1, w3, w2, layer, gain, bias, tm=MOE_TM):
    t = h32.shape[0]
    ids, gates, ranks, counts = moe_router(h32, w_group, b_group, w_expert, b_expert)
    eid = ids.reshape(-1)
    n_pairs = eid.shape[0]
    padded = (counts + tm - 1) // tm * tm
    seg_end = jnp.cumsum(padded)
    seg_start = seg_end - padded
    onehot = eid[:, None] == jnp.arange(MOE_EXPERTS, dtype=jnp.int32)[None, :]
    dest = (jnp.sum(jnp.where(onehot, seg_start[None, :], 0), axis=1) + ranks.reshape(-1)).astype(jnp.int32)
    n_blocks = -(-n_pairs // tm) + MOE_EXPERTS
    n_rows = n_blocks * tm
    token_of_pair = (jnp.arange(n_pairs, dtype=jnp.int32) // MOE_TOP_K)
    filler = jnp.arange(n_rows, dtype=jnp.int32) % t
    row_token = filler.at[dest].set(token_of_pair) * (h32.shape[1] // LANES)
    block_start = jnp.arange(n_blocks, dtype=jnp.int32) * tm
    block_expert = jnp.minimum(jnp.sum((seg_end[None, :] <= block_start[:, None]).astype(jnp.int32), axis=1),
                               MOE_EXPERTS - 1)
    n_used = (seg_end[-1:] // tm).astype(jnp.int32)
    y_rows = moe_experts(h_rows, row_token, block_expert, n_used, w1, w3, w2, layer, tm=tm)
    ctm = min(COMBINE_TM, t)
    dest_tiles = jnp.transpose(dest.reshape(t // ctm, ctm, MOE_TOP_K), (0, 2, 1))
    return moe_combine_ln(y_rows, dest_tiles, gates, h32, gain, bias, tm=ctm)


def kernel(x, gdn_w_in, gdn_conv, gdn_a_log, gdn_dt_bias, gdn_norm, gdn_w_out, kv_w_k, kv_w_v, dil_w_q, dil_w_o,
           rel_bias, ln_gain, ln_bias, moe_w_group, moe_b_group, moe_w_expert, moe_b_expert, moe_w1, moe_w3, moe_w2):
    batch, seq, d = x.shape
    assert all(window // dil == DIL_BLOCK for window, dil in DIL_CONFIGS)
    t = batch * seq
    h32 = x.reshape(t, d)
    h16 = None
    k = v = None
    biases = None
    for layer in range(DEPTH):
        if layer < N_A_LAYERS:
            h32, h16, h_rows = gated_deltanet_layer(
                h32, gdn_w_in[layer], gdn_conv[layer], gdn_a_log[layer], gdn_dt_bias[layer],
                gdn_norm[layer], gdn_w_out[layer], ln_gain[layer, 0], ln_bias[layer, 0], batch, seq)
        else:
            if layer == N_A_LAYERS:
                k = matmul(h16, kv_w_k.astype(BF16), out_dtype=BF16)
                v = matmul(h16, kv_w_v.astype(BF16), out_dtype=BF16)
                biases = [_band_bias(rel_bias[:, gi * DIL_HEADS:(gi + 1) * DIL_HEADS], dil)
                          for gi, (_, dil) in enumerate(DIL_CONFIGS)]
            j = layer - N_A_LAYERS
            outs, lses = [], []
            for gi, (_, dil) in enumerate(DIL_CONFIGS):
                q = matmul(h16, dil_w_q[j][:, gi * DIL_KV_DIM:(gi + 1) * DIL_KV_DIM].astype(BF16), out_dtype=BF16)
                o, lse = dilated_group_attention(q, k, v, biases[gi], dil, batch, seq)
                outs.append(o)
                lses.append(lse)
            h32, h16, h_rows = dilated_out(outs, lses, dil_w_o[j].astype(BF16), h32, ln_gain[layer, 0], ln_bias[layer, 0])
        h32, h16 = moe_layer(h32, h_rows, moe_w_group[layer], moe_b_group[layer], moe_w_expert[layer], moe_b_expert[layer],
                             moe_w1, moe_w3, moe_w2, layer, ln_gain[layer, 1], ln_bias[layer, 1])
    return h32.reshape(batch, seq, d)
```
